```python
import math
import jax, jax.numpy as jnp
from jax import lax
import numpy as np

D_MODEL = 1024
BATCH = 4
SEQ = 4096
DEPTH = 4

D_MIX = D_MODEL
N_MIXERS = 4
GROUP_W = D_MIX // N_MIXERS
HEAD_DIM = 64
N_HEADS_G = GROUP_W // HEAD_DIM
SSD_STATE = 128
SSD_CONV = 4
SSD_CHUNK = 128
RWKV_DECAY_LORA = 32
RWKV_A_LORA = 32
RWKV_GATE_LORA = 64
RWKV_GN_EPS = 64e-5
GLA_DK = GROUP_W // 2
GLA_HEAD_K = GLA_DK // N_HEADS_G
GLA_GATE_LORA = 16
GLA_GATE_TAU = 16.0
GLA_CHUNK = 64
MLSTM_CONV = 4
MLSTM_CHUNK = 64
N_EXPERTS = 32
TOP_K = 4
D_EXPERT = D_MODEL
SWIGLU_LIMIT = 7.0
SWIGLU_ALPHA = 1.702
EXPERT_BLOCK = 128
PLE_DIM = 256
DEEPNORM_ALPHA = (2 * DEPTH) ** 0.25
DEEPNORM_BETA = (8 * DEPTH) ** -0.25
LN_EPS = 1e-5
NORM_EPS = 1e-5

SSD_SPLITS = (GROUP_W, GROUP_W, SSD_STATE, SSD_STATE, N_HEADS_G)
RWKV_SPLITS = (GROUP_W, GROUP_W, GROUP_W, RWKV_DECAY_LORA, RWKV_A_LORA, RWKV_GATE_LORA)
GLA_SPLITS = (GLA_DK, GLA_DK, GROUP_W, GLA_GATE_LORA, GROUP_W)
MLSTM_SPLITS = (GROUP_W, GROUP_W, GROUP_W, N_HEADS_G, N_HEADS_G, GROUP_W)
SSD_W = sum(SSD_SPLITS)
RWKV_W = sum(RWKV_SPLITS)
GLA_W = sum(GLA_SPLITS)
MLSTM_W = sum(MLSTM_SPLITS)
D_IN_PROJ = SSD_W + RWKV_W + GLA_W + MLSTM_W

kernel_name = "hybrid_ssd_rwkv7_gla_mlstm_moe_deepnorm"


def _split(u, sizes):
    idx = np.cumsum(sizes)[:-1].tolist()
    return jnp.split(u, idx, axis=-1)


def _causal_mask(n):
    return jnp.tril(jnp.ones((n, n), dtype=bool))


def _causal_dwconv(u, w, b):
    k, c = w.shape
    y = lax.conv_general_dilated(u, w[:, None, :].astype(u.dtype), window_strides=(1,),
                                 padding=[(k - 1, 0)], dimension_numbers=("NWC", "WIO", "NWC"),
                                 feature_group_count=c)
    return y + b


def _token_shift(u):
    return jnp.pad(u, ((0, 0), (1, 0), (0, 0)))[:, :-1]


def _heads(u, n_heads):
    return u.reshape(*u.shape[:-1], n_heads, u.shape[-1] // n_heads)


def _to_chunks(u, size):
    b, s, h = u.shape[:3]
    u = u.reshape(b, s // size, size, h, *u.shape[3:])
    return jnp.moveaxis(u, 3, 1)


def _from_chunks(y):
    y = jnp.moveaxis(y, 1, 3)
    return y.reshape(y.shape[0], y.shape[1] * y.shape[2], *y.shape[3:])


def _chunk_states(decay, contrib):
    d = jnp.moveaxis(decay, 2, 0)
    u = jnp.moveaxis(contrib, 2, 0)

    def step(s, inp):
        dc, uc = inp
        return dc * s + uc, s

    _, prev = lax.scan(step, jnp.zeros_like(u[0]), (d, u))
    return jnp.moveaxis(prev, 0, 2)


def _layer_norm(x, g, b):
    xf = x.astype(jnp.float32)
    mu = jnp.mean(xf, -1, keepdims=True)
    xc = xf - mu
    var = jnp.mean(xc * xc, -1, keepdims=True)
    return (xc * lax.rsqrt(var + LN_EPS) * g + b).astype(x.dtype)


def _group_norm(y, n_heads, gain, bias, eps, center):
    yf = _heads(y.astype(jnp.float32), n_heads)
    if center:
        yf = yf - jnp.mean(yf, -1, keepdims=True)
    yf = yf * lax.rsqrt(jnp.mean(yf * yf, -1, keepdims=True) + eps)
    yf = yf.reshape(y.shape) * gain
    if bias is not None:
        yf = yf + bias
    return yf


def _ssd_mixer(u, conv_w, conv_b, a_log, dt_bias, d_skip, norm_g):
    bsz, seq, _ = u.shape
    z, xr, bm, cm, dt = _split(u, SSD_SPLITS)
    xbc = jax.nn.silu(_causal_dwconv(jnp.concatenate([xr, bm, cm], -1), conv_w, conv_b))
    xs, bm, cm = _split(xbc, (GROUP_W, SSD_STATE, SSD_STATE))
    xs = _heads(xs, N_HEADS_G)
    dt = jax.nn.softplus(dt + dt_bias)
    nc = seq // SSD_CHUNK
    xc = _to_chunks(xs * dt[..., None], SSD_CHUNK)
    adt = _to_chunks(dt * (-jnp.exp(a_log)), SSD_CHUNK)
    bc = bm.reshape(bsz, nc, SSD_CHUNK, SSD_STATE)
    cc = cm.reshape(bsz, nc, SSD_CHUNK, SSD_STATE)
    acum = jnp.cumsum(adt, axis=-1)
    lmat = jnp.exp(jnp.where(_causal_mask(SSD_CHUNK), acum[..., :, None] - acum[..., None, :], -jnp.inf))
    y_diag = jnp.einsum("bcln,bcsn,bhcls,bhcsp->bhclp", cc, bc, lmat, xc)
    decay_states = jnp.exp(acum[..., -1:] - acum)
    states = jnp.einsum("bcsn,bhcs,bhcsp->bhcpn", bc, decay_states, xc)
    prev = _chunk_states(jnp.exp(acum[..., -1])[..., None, None], states)
    y_off = jnp.einsum("bcln,bhcpn,bhcl->bhclp", cc, prev, jnp.exp(acum))
    y = _from_chunks(y_diag + y_off) + xs * d_skip[:, None]
    y = y.reshape(bsz, seq, GROUP_W) * jax.nn.silu(z)
    return _group_norm(y, 1, norm_g, None, NORM_EPS, center=False)


def _rwkv7_mixer(u, mu, w0, w_up, a0, a_up, g_up, k_k, k_a, r_k, ln_g, ln_b):
    bsz, seq, _ = u.shape
    u = u + (_token_shift(u) - u) * mu
    r, k, v, wd, ad, gd = _split(u, RWKV_SPLITS)
    w = -jax.nn.softplus(-(w0 + jnp.tanh(wd) @ w_up)) - 0.5
    decay = jnp.exp(-jnp.exp(w))
    a = jax.nn.sigmoid(a0 + ad @ a_up)
    g = jax.nn.sigmoid(gd) @ g_up
    kk = _heads(k * k_k, N_HEADS_G)
    kk = kk / jnp.maximum(jnp.linalg.norm(kk, axis=-1, keepdims=True), 1e-12)
    k = k * (1.0 + (a - 1.0) * k_a)
    rh, wh, kh, vh, ah = (_heads(t, N_HEADS_G) for t in (r, decay, k, v, a))
    seq_major = [jnp.moveaxis(t, 1, 0) for t in (rh, wh, kh, vh, -kk, kk * ah)]

    def step(state, inp):
        r_t, w_t, k_t, v_t, a_t, b_t = inp
        sa = jnp.einsum("bhvk,bhk->bhv", state, a_t)
        state = (state * w_t[:, :, None, :] + sa[..., None] * b_t[:, :, None, :]
                 + v_t[..., None] * k_t[:, :, None, :])
        return state, jnp.einsum("bhvk,bhk->bhv", state, r_t)

    s0 = jnp.zeros((bsz, N_HEADS_G, HEAD_DIM, HEAD_DIM), rh.dtype)
    _, o = lax.scan(step, s0, tuple(seq_major))
    o = jnp.moveaxis(o, 0, 1).reshape(bsz, seq, GROUP_W)
    o = _group_norm(o, N_HEADS_G, ln_g, ln_b, RWKV_GN_EPS, center=True)
    bonus = jnp.sum(rh * kh * r_k, -1, keepdims=True) * vh
    return (o + bonus.reshape(bsz, seq, GROUP_W)) * g


def _gla_mixer(u, gate_up, gate_b, norm_g):
    bsz, seq, _ = u.shape
    q, k, v, gd, og = _split(u, GLA_SPLITS)
    log_a = jax.nn.log_sigmoid(gd @ gate_up + gate_b) / GLA_GATE_TAU
    qc = _to_chunks(_heads(q, N_HEADS_G), GLA_CHUNK) * GLA_HEAD_K ** -0.5
    kc = _to_chunks(_heads(k, N_HEADS_G), GLA_CHUNK)
    vc = _to_chunks(_heads(v, N_HEADS_G), GLA_CHUNK)
    bcum = jnp.cumsum(_to_chunks(_heads(log_a, N_HEADS_G), GLA_CHUNK), axis=3)
    q_dec = qc * jnp.exp(bcum)
    k_inv = kc * jnp.exp(-bcum)
    attn = jnp.where(_causal_mask(GLA_CHUNK), jnp.einsum("bhctd,bhcsd->bhcts", q_dec, k_inv), 0.0)
    o_intra = jnp.einsum("bhcts,bhcsv->bhctv", attn, vc)
    b_last = bcum[..., -1, :]
    contrib = jnp.einsum("bhcsd,bhcsv->bhcdv", kc * jnp.exp(b_last[..., None, :] - bcum), vc)
    prev = _chunk_states(jnp.exp(b_last)[..., None], contrib)
    o_inter = jnp.einsum("bhctd,bhcdv->bhctv", q_dec, prev)
    o = _from_chunks(o_intra + o_inter).reshape(bsz, seq, GROUP_W)
    return _group_norm(o, N_HEADS_G, norm_g, None, NORM_EPS, center=False) * jax.nn.silu(og)


def _mlstm_mixer(u, conv_w, conv_b, i_b, f_b, norm_g):
    bsz, seq, _ = u.shape
    q, k, v, ig, fg, og = _split(u, MLSTM_SPLITS)
    q, k = _split(jax.nn.silu(_causal_dwconv(jnp.concatenate([q, k], -1), conv_w, conv_b)), (GROUP_W, GROUP_W))
    qc = _to_chunks(_heads(q, N_HEADS_G), MLSTM_CHUNK) * HEAD_DIM ** -0.5
    kc = _to_chunks(_heads(k, N_HEADS_G), MLSTM_CHUNK)
    vc = _to_chunks(_heads(v, N_HEADS_G), MLSTM_CHUNK)
    i_pre = _to_chunks(ig + i_b, MLSTM_CHUNK)
    b = jnp.cumsum(_to_chunks(jax.nn.log_sigmoid(fg + f_b), MLSTM_CHUNK), axis=-1)
    b_last = b[..., -1]
    dmat = jnp.where(_causal_mask(MLSTM_CHUNK), b[..., :, None] - b[..., None, :] + i_pre[..., None, :], -jnp.inf)
    a_st = b_last[..., None] - b + i_pre
    m_loc = jnp.max(a_st, -1)
    w_st = jnp.exp(a_st - m_loc[..., None])
    c_contrib = jnp.einsum("bhcs,bhcsv,bhcsk->bhcvk", w_st, vc, kc)
    n_contrib = jnp.einsum("bhcs,bhcsk->bhck", w_st, kc)

    def step(carry, inp):
        c_s, n_s, m_s = carry
        bl, ml, cu, nu = inp
        m_new = jnp.maximum(bl + m_s, ml)
        s_old = jnp.exp(bl + m_s - m_new)
        s_new = jnp.exp(ml - m_new)
        c_new = s_old[..., None, None] * c_s + s_new[..., None, None] * cu
        n_new = s_old[..., None] * n_s + s_new[..., None] * nu
        return (c_new, n_new, m_new), (c_s, n_s, m_s)

    init = (jnp.zeros_like(c_contrib[:, :, 0]), jnp.zeros_like(n_contrib[:, :, 0]), jnp.zeros_like(b_last[:, :, 0]))
    xs = tuple(jnp.moveaxis(t, 2, 0) for t in (b_last, m_loc, c_contrib, n_contrib))
    _, (c_prev, n_prev, m_prev) = lax.scan(step, init, xs)
    c_prev, n_prev, m_prev = (jnp.moveaxis(t, 0, 2) for t in (c_prev, n_prev, m_prev))
    m_inter = b + m_prev[..., None]
    m_row = jnp.maximum(m_inter, jnp.max(dmat, -1))
    w_inter = jnp.exp(m_inter - m_row)
    scores = jnp.einsum("bhctk,bhcsk->bhcts", qc, kc) * jnp.exp(dmat - m_row[..., None])
    num = (w_inter[..., None] * jnp.einsum("bhcvk,bhctk->bhctv", c_prev, qc)
           + jnp.einsum("bhcts,bhcsv->bhctv", scores, vc))
    den = w_inter * jnp.einsum("bhck,bhctk->bhct", n_prev, qc) + jnp.sum(scores, -1)
    h = num / jnp.maximum(jnp.abs(den), jnp.exp(-m_row))[..., None]
    h = _from_chunks(h).reshape(bsz, seq, GROUP_W) * jax.nn.sigmoid(og)
    return _group_norm(h, N_HEADS_G, norm_g, None, NORM_EPS, center=True)


def _moe_ffn(x2d, router_w, router_b, w_gu, b_gu, w_down, b_down):
    t, d = x2d.shape
    n_assign = t * TOP_K
    logits = (x2d @ router_w + router_b).astype(jnp.float32)
    top_val, top_idx = lax.top_k(logits, TOP_K)
    gate = jax.nn.softmax(top_val, axis=-1)
    flat_e = top_idx.reshape(-1)
    flat_tok = jnp.arange(n_assign, dtype=jnp.int32) // TOP_K
    order = jnp.argsort(flat_e)
    e_s, tok_s, g_s = flat_e[order], flat_tok[order], gate.reshape(-1)[order]
    counts = jnp.zeros((N_EXPERTS,), jnp.int32).at[flat_e].add(1)
    padded = (counts + EXPERT_BLOCK - 1) // EXPERT_BLOCK * EXPERT_BLOCK
    start = jnp.cumsum(counts) - counts
    pend = jnp.cumsum(padded)
    pstart = pend - padded
    dest = pstart[e_s] + (jnp.arange(n_assign, dtype=jnp.int32) - start[e_s])
    n_blocks = (n_assign + N_EXPERTS * (EXPERT_BLOCK - 1) + EXPERT_BLOCK - 1) // EXPERT_BLOCK
    n_rows = n_blocks * EXPERT_BLOCK
    row_tok = jnp.full((n_rows,), t, jnp.int32).at[dest].set(tok_s)
    row_gate = jnp.zeros((n_rows,), jnp.float32).at[dest].set(g_s)
    block_expert = jnp.minimum(jnp.searchsorted(pend, jnp.arange(n_blocks) * EXPERT_BLOCK, side="right"),
                               N_EXPERTS - 1)
    x_pad = jnp.concatenate([x2d, jnp.zeros((1, d), x2d.dtype)], 0)
    xb = x_pad[row_tok].reshape(n_blocks, EXPERT_BLOCK, d)

    def expert_block(args):
        xblk, e = args
        hg, hl = jnp.split(xblk @ w_gu[e] + b_gu[e], 2, axis=-1)
        hg = jnp.minimum(hg, SWIGLU_LIMIT)
        hl = jnp.clip(hl, -SWIGLU_LIMIT, SWIGLU_LIMIT)
        glu = hg * jax.nn.sigmoid(hg * SWIGLU_ALPHA)
        return ((hl + 1.0) * glu) @ w_down[e] + b_down[e]

    yb = lax.map(expert_block, (xb, block_expert)).reshape(n_rows, d)
    y = (yb * row_gate[:, None]).astype(x2d.dtype)
    return jnp.zeros((t + 1, d), x2d.dtype).at[row_tok].add(y)[:t]


def setup_inputs(seed: int = 0) -> dict:
    key = jax.random.key(seed)
    ks = iter(jax.random.split(key, 64))
    f32 = jnp.float32
    L, W, H = DEPTH, GROUP_W, N_HEADS_G

    def nrm(shape, scale):
        return scale * jax.random.normal(next(ks), shape, f32)

    def uni(shape, lo, hi):
        return jax.random.uniform(next(ks), shape, f32, lo, hi)

    dt0 = jnp.exp(uni((L, H), math.log(1e-3), math.log(1e-1)))
    return {
        "x": nrm((BATCH, SEQ, D_MODEL), 1.0),
        "p": nrm((DEPTH, BATCH, SEQ, PLE_DIM), 1.0),
        "w_in": nrm((L, D_MODEL, D_IN_PROJ), D_MODEL ** -0.5),
        "w_out": nrm((L, D_MIX, D_MODEL), DEEPNORM_BETA * D_MIX ** -0.5),
        "ln1_g": 1.0 + nrm((L, D_MODEL), 0.02),
        "ln1_b": nrm((L, D_MODEL), 0.02),
        "ssd_conv_w": nrm((L, SSD_CONV, W + 2 * SSD_STATE), SSD_CONV ** -0.5),
        "ssd_conv_b": nrm((L, W + 2 * SSD_STATE), 0.02),
        "ssd_a_log": jnp.log(uni((L, H), 1.0, 16.0)),
        "ssd_dt_bias": dt0 + jnp.log(-jnp.expm1(-dt0)),
        "ssd_d": 1.0 + nrm((L, H), 0.1),
        "ssd_norm_g": 1.0 + nrm((L, W), 0.02),
        "rwkv_mu": uni((L, RWKV_W), 0.0, 1.0),
        "rwkv_w0": uni((L, W), -6.0, -1.0),
        "rwkv_w_up": nrm((L, RWKV_DECAY_LORA, W), 0.5 * RWKV_DECAY_LORA ** -0.5),
        "rwkv_a0": nrm((L, W), 0.1),
        "rwkv_a_up": nrm((L, RWKV_A_LORA, W), RWKV_A_LORA ** -0.5),
        "rwkv_g_up": nrm((L, RWKV_GATE_LORA, W), RWKV_GATE_LORA ** -0.5),
        "rwkv_k_k": 0.85 + nrm((L, W), 0.02),
        "rwkv_k_a": 1.0 + nrm((L, W), 0.02),
        "rwkv_r_k": nrm((L, H, HEAD_DIM), 0.1),
        "rwkv_ln_g": 1.0 + nrm((L, W), 0.02),
        "rwkv_ln_b": nrm((L, W), 0.02),
        "gla_gate_up": nrm((L, GLA_GATE_LORA, GLA_DK), GLA_GATE_LORA ** -0.5),
        "gla_gate_b": nrm((L, GLA_DK), 0.1),
        "gla_norm_g": 1.0 + nrm((L, W), 0.02),
        "mlstm_conv_w": nrm((L, MLSTM_CONV, 2 * W), MLSTM_CONV ** -0.5),
        "mlstm_conv_b": nrm((L, 2 * W), 0.02),
        "mlstm_i_b": nrm((L, H), 0.1),
        "mlstm_f_b": uni((L, H), 3.0, 6.0),
        "mlstm_norm_g": 1.0 + nrm((L, W), 0.02),
        "router_w": nrm((L, D_MODEL, N_EXPERTS), D_MODEL ** -0.5),
        "router_b": nrm((L, N_EXPERTS), 0.01),
        "exp_w_gu": nrm((L, N_EXPERTS, D_MODEL, 2 * D_EXPERT), D_MODEL ** -0.5),
        "exp_b_gu": nrm((L, N_EXPERTS, 2 * D_EXPERT), 0.01),
        "exp_w_down": nrm((L, N_EXPERTS, D_EXPERT, D_MODEL), DEEPNORM_BETA * D_EXPERT ** -0.5),
        "exp_b_down": nrm((L, N_EXPERTS, D_MODEL), 0.01),
        "ple_gate_w": nrm((L, D_MODEL, D_MODEL), D_MODEL ** -0.5),
        "ple_proj": nrm((L, PLE_DIM, D_MODEL), DEEPNORM_BETA * PLE_DIM ** -0.5),
        "ln2_g": 1.0 + nrm((L, D_MODEL), 0.02),
        "ln2_b": nrm((L, D_MODEL), 0.02),
    }


def reference(x, p, w_in, w_out, ln1_g, ln1_b,
              ssd_conv_w, ssd_conv_b, ssd_a_log, ssd_dt_bias, ssd_d, ssd_norm_g,
              rwkv_mu, rwkv_w0, rwkv_w_up, rwkv_a0, rwkv_a_up, rwkv_g_up, rwkv_k_k, rwkv_k_a,
              rwkv_r_k, rwkv_ln_g, rwkv_ln_b,
              gla_gate_up, gla_gate_b, gla_norm_g,
              mlstm_conv_w, mlstm_conv_b, mlstm_i_b, mlstm_f_b, mlstm_norm_g,
              router_w, router_b, exp_w_gu, exp_b_gu, exp_w_down, exp_b_down,
              ple_gate_w, ple_proj, ln2_g, ln2_b):
    bsz, seq, d = x.shape
    for i in range(DEPTH):
        u = (x @ w_in[i]).astype(jnp.float32)
        u_ssd, u_rwkv, u_gla, u_mlstm = _split(u, (SSD_W, RWKV_W, GLA_W, MLSTM_W))
        y_ssd = _ssd_mixer(u_ssd, ssd_conv_w[i], ssd_conv_b[i], ssd_a_log[i], ssd_dt_bias[i],
                           ssd_d[i], ssd_norm_g[i])
        y_rwkv = _rwkv7_mixer(u_rwkv, rwkv_mu[i], rwkv_w0[i], rwkv_w_up[i], rwkv_a0[i], rwkv_a_up[i],
                              rwkv_g_up[i], rwkv_k_k[i], rwkv_k_a[i], rwkv_r_k[i], rwkv_ln_g[i], rwkv_ln_b[i])
        y_gla = _gla_mixer(u_gla, gla_gate_up[i], gla_gate_b[i], gla_norm_g[i])
        y_mlstm = _mlstm_mixer(u_mlstm, mlstm_conv_w[i], mlstm_conv_b[i], mlstm_i_b[i], mlstm_f_b[i],
                               mlstm_norm_g[i])
        mix = jnp.concatenate([y_ssd, y_rwkv, y_gla, y_mlstm], -1).astype(x.dtype) @ w_out[i]
        x = _layer_norm(DEEPNORM_ALPHA * x + mix, ln1_g[i], ln1_b[i])
        ffn = _moe_ffn(x.reshape(bsz * seq, d), router_w[i], router_b[i], exp_w_gu[i], exp_b_gu[i],
                       exp_w_down[i], exp_b_down[i]).reshape(bsz, seq, d)
        h = DEEPNORM_ALPHA * x + ffn
        h = h + jax.nn.sigmoid(h @ ple_gate_w[i]) * (p[i] @ ple_proj[i])
        x = _layer_norm(h, ln2_g[i], ln2_b[i])
    return x
```

```python
import functools

import jax
import jax.numpy as jnp
import numpy as np
from jax import lax
from jax.experimental import pallas as pl
from jax.experimental.pallas import tpu as pltpu

F32 = jnp.float32
BF16 = jnp.bfloat16
HI = lax.Precision.HIGHEST

D_MODEL = 1024
DEPTH = 4
GROUP_W = 256
N_HEADS = 4
HEAD_DIM = 64
SSD_STATE = 128
SSD_CHUNK = 128
GLA_DK = 128
GLA_HEAD_K = 32
GLA_TAU = 16.0
CHUNK = 64
RWKV_GN_EPS = 64e-5
NORM_EPS = 1e-5
LN_EPS = 1e-5
N_EXPERTS = 32
TOP_K = 4
SWIGLU_LIMIT = 7.0
SWIGLU_ALPHA = 1.702
PLE_DIM = 256
DEEPNORM_ALPHA = (2 * DEPTH) ** 0.25

U_WIDTH = 3584
COL_SSD_Z, COL_SSD_X, COL_SSD_BC = 0, 256, 512
COL_RWKV = 768
COL_ML_QK = 1536
COL_ML_VO = 2048
COL_GLA_QK, COL_GLA_V, COL_GLA_OG = 2560, 2816, 3072
COL_LORA = 3328
COL_MISC = 3456
MISC_DT, MISC_GLA, MISC_I, MISC_F = 0, 4, 20, 24

MIXER_TB = 256
EXPERT_BLK = 256
VMEM_LIMIT = 56 * 1024 * 1024


def _cparams(sem):
    return pltpu.CompilerParams(dimension_semantics=sem, vmem_limit_bytes=VMEM_LIMIT)


def _dot(a, b, prec=None):
    return lax.dot_general(a, b, (((1,), (0,)), ((), ())), precision=prec,
                           preferred_element_type=F32)


def _dot_nt(a, b, prec=None):
    return lax.dot_general(a, b, (((1,), (1,)), ((), ())), precision=prec,
                           preferred_element_type=F32)


def _dot_tn(a, b, prec=None):
    return lax.dot_general(a, b, (((0,), (0,)), ((), ())), precision=prec,
                           preferred_element_type=F32)


def _bf(x):
    return x.astype(BF16)


def _sigmoid(x):
    return 1.0 / (1.0 + jnp.exp(-x))


def _silu(x):
    return x * _sigmoid(x)


def _softplus(x):
    return jnp.maximum(x, 0.0) + jnp.log1p(jnp.exp(-jnp.abs(x)))


def _log_sigmoid(x):
    return jnp.minimum(x, 0.0) - jnp.log1p(jnp.exp(-jnp.abs(x)))


def _iota(shape, dim):
    return lax.broadcasted_iota(jnp.int32, shape, dim)


def _tri_incl(n):
    return (_iota((n, n), 0) >= _iota((n, n), 1)).astype(F32)


def _seg_matrix(n, seg, value):
    sh = int(np.log2(seg))
    same = (lax.shift_right_logical(_iota((n, n), 0), sh)
            == lax.shift_right_logical(_iota((n, n), 1), sh))
    return jnp.where(same, value, 0.0).astype(F32)


def _head_mask(width, seg, h):
    lane = _iota((1, width), 1)
    return ((lane >= h * seg) & (lane < (h + 1) * seg)).astype(F32)


def _expand(cols, seg):
    n = len(cols)
    rows = cols[0].shape[0]
    width = n * seg
    lane = _iota((rows, width), 1)
    out = jnp.broadcast_to(cols[n - 1], (rows, width))
    for h in range(n - 2, -1, -1):
        out = jnp.where(lane < (h + 1) * seg, jnp.broadcast_to(cols[h], (rows, width)), out)
    return out


def _col_to_row(col):
    n = col.shape[0]
    eye = _iota((n, n), 0) == _iota((n, n), 1)
    return jnp.sum(jnp.where(eye, col, 0.0), axis=0, keepdims=True)


def _layer_norm(x, g, b):
    mu = jnp.mean(x, axis=-1, keepdims=True)
    xc = x - mu
    var = jnp.mean(xc * xc, axis=-1, keepdims=True)
    return xc * lax.rsqrt(var + LN_EPS) * g + b


def _causal_conv_silu(buf, w_ref, b_ref, rows):
    acc = b_ref[...]
    for k in range(4):
        acc = acc + buf[pl.ds(5 + k, rows), :] * w_ref[k:k + 1, :]
    return _silu(acc)


def _matmul_body(x_ref, w_ref, o_ref):
    o_ref[...] = jnp.dot(x_ref[...], w_ref[...], preferred_element_type=F32)


def _in_proj(x_bf, w_bf, tm=512, tn=1792):
    t, k = x_bf.shape
    n = w_bf.shape[1]
    return pl.pallas_call(
        _matmul_body,
        grid=(n // tn, t // tm),
        in_specs=[pl.BlockSpec((tm, k), lambda j, i: (i, 0)),
                  pl.BlockSpec((k, tn), lambda j, i: (0, j))],
        out_specs=pl.BlockSpec((tm, tn), lambda j, i: (i, j)),
        out_shape=jax.ShapeDtypeStruct((t, n), F32),
        compiler_params=_cparams(("arbitrary", "arbitrary")),
        name="in_proj",
    )(x_bf, w_bf)


def _ssd_body(z_ref, x_ref, bc_ref, misc_ref, cwx_ref, cwbc_ref, cbx_ref, cbbc_ref,
              alog_ref, dtb_ref, dskip_ref, ng_ref, o_ref, xbuf, bcbuf, st_ref, *, tb):
    j = pl.program_id(1)

    @pl.when(j == 0)
    def _():
        xbuf[0:8, :] = jnp.zeros((8, GROUP_W), F32)
        bcbuf[0:8, :] = jnp.zeros((8, GROUP_W), F32)
        st_ref[...] = jnp.zeros(st_ref.shape, F32)

    xbuf[8:8 + tb, :] = x_ref[...]
    bcbuf[8:8 + tb, :] = bc_ref[...]
    xs_all = _causal_conv_silu(xbuf, cwx_ref, cbx_ref, tb)
    bc_all = _causal_conv_silu(bcbuf, cwbc_ref, cbbc_ref, tb)
    xbuf[0:8, :] = xbuf[tb:tb + 8, :]
    bcbuf[0:8, :] = bcbuf[tb:tb + 8, :]

    dt_all = _softplus(misc_ref[...] + dtb_ref[...])
    adt_all = dt_all * (-jnp.exp(alog_ref[...]))
    z_all = z_ref[...]
    L = SSD_CHUNK
    tri = _tri_incl(L)
    causal = _iota((L, L), 0) >= _iota((L, L), 1)
    masks = [_head_mask(GROUP_W, HEAD_DIM, h) for h in range(N_HEADS)]

    for c in range(tb // L):
        rows = slice(c * L, (c + 1) * L)
        xs = xs_all[rows]
        bm = _bf(bc_all[rows, 0:SSD_STATE])
        cm = _bf(bc_all[rows, SSD_STATE:2 * SSD_STATE])
        dt = dt_all[rows]
        acum = _dot(tri, adt_all[rows], HI)
        cols = [acum[:, h:h + 1] for h in range(N_HEADS)]
        lasts = [acum[L - 1:L, h:h + 1] for h in range(N_HEADS)]
        xc = xs * _expand([dt[:, h:h + 1] for h in range(N_HEADS)], HEAD_DIM)
        g = _dot_nt(cm, bm)
        y = jnp.zeros((L, GROUP_W), F32)
        for h in range(N_HEADS):
            lmat = jnp.exp(jnp.where(causal, cols[h] - _col_to_row(cols[h]), -jnp.inf))
            y = y + _dot(_bf(g * lmat), _bf(xc * masks[h]))
        dec_states = _expand([jnp.exp(lasts[h] - cols[h]) for h in range(N_HEADS)], HEAD_DIM)
        st_prev = st_ref[...]
        y = y + _dot(cm, _bf(st_prev)) * _expand([jnp.exp(cols[h]) for h in range(N_HEADS)], HEAD_DIM)
        st_new = _dot_tn(bm, _bf(xc * dec_states))
        st_ref[...] = st_prev * _expand([jnp.exp(lasts[h]) for h in range(N_HEADS)], HEAD_DIM) + st_new
        y = y + xs * dskip_ref[...]
        y = y * _silu(z_all[rows])
        ms = jnp.mean(y * y, axis=-1, keepdims=True)
        o_ref[rows, :] = (y * lax.rsqrt(ms + NORM_EPS) * ng_ref[...]).astype(o_ref.dtype)


def _u_spec(tb, width, col, nb):
    return pl.BlockSpec((tb, width), lambda b, j: (b * nb + j, col // width))


def _const_spec(shape):
    return pl.BlockSpec(shape, lambda b, j: tuple(0 for _ in shape))


def _ssd_mixer(u, bsz, seq, conv_w, conv_b, a_log, dt_bias, d_skip, norm_g):
    tb = MIXER_TB
    nb = seq // tb
    pad4 = lambda v: jnp.zeros((1, 128), F32).at[0, :N_HEADS].set(v)
    args = (u, u, u, u,
            conv_w[:, :GROUP_W], conv_w[:, GROUP_W:],
            conv_b[None, :GROUP_W], conv_b[None, GROUP_W:],
            pad4(a_log), pad4(dt_bias),
            jnp.repeat(d_skip, HEAD_DIM)[None, :], norm_g[None, :])
    in_specs = [_u_spec(tb, GROUP_W, COL_SSD_Z, nb), _u_spec(tb, GROUP_W, COL_SSD_X, nb),
                _u_spec(tb, GROUP_W, COL_SSD_BC, nb), _u_spec(tb, 128, COL_MISC, nb)]
    in_specs += [_const_spec(a.shape) for a in args[4:]]
    return pl.pallas_call(
        functools.partial(_ssd_body, tb=tb),
        grid=(bsz, nb),
        in_specs=in_specs,
        out_specs=pl.BlockSpec((tb, GROUP_W), lambda b, j: (b * nb + j, 0)),
        out_shape=jax.ShapeDtypeStruct((bsz * seq, GROUP_W), BF16),
        scratch_shapes=[pltpu.VMEM((tb + 8, GROUP_W), F32), pltpu.VMEM((tb + 8, GROUP_W), F32),
                        pltpu.VMEM((SSD_STATE, GROUP_W), F32)],
        compiler_params=_cparams(("arbitrary", "arbitrary")),
        name="ssd_mixer",
    )(*args)


def _gla_body(qk_ref, v_ref, og_ref, misc_ref, gup_ref, gb_ref, ng_ref, o_ref, st_ref, *, tb):
    j = pl.program_id(1)

    @pl.when(j == 0)
    def _():
        st_ref[...] = jnp.zeros(st_ref.shape, F32)

    L = CHUNK
    q_all = qk_ref[:, 0:GLA_DK] * (GLA_HEAD_K ** -0.5)
    k_all = qk_ref[:, GLA_DK:2 * GLA_DK]
    v_all = v_ref[...]
    og_all = og_ref[...]
    gpre = _dot(_bf(misc_ref[...]), gup_ref[...]) + gb_ref[...]
    loga_all = _log_sigmoid(gpre) / GLA_TAU
    tri = _tri_incl(L)
    causal = _iota((L, L), 0) >= _iota((L, L), 1)
    kmasks = [_head_mask(GLA_DK, GLA_HEAD_K, h) for h in range(N_HEADS)]
    vmasks = [_head_mask(GROUP_W, HEAD_DIM, h) for h in range(N_HEADS)]
    bd = (lax.shift_right_logical(_iota((GROUP_W, GLA_DK), 0), 6)
          == lax.shift_right_logical(_iota((GROUP_W, GLA_DK), 1), 5))
    segmean = _seg_matrix(GROUP_W, HEAD_DIM, 1.0 / HEAD_DIM)

    for c in range(tb // L):
        rows = slice(c * L, (c + 1) * L)
        bcum = _dot(tri, loga_all[rows], HI)
        b_last = bcum[L - 1:L, :]
        q_dec = q_all[rows] * jnp.exp(bcum)
        k_inv = _bf(k_all[rows] * jnp.exp(-bcum))
        k_dec = _bf(k_all[rows] * jnp.exp(b_last - bcum))
        v = v_all[rows]
        st_prev = st_ref[...]
        o = _dot_nt(_bf(q_dec), _bf(st_prev))
        for h in range(N_HEADS):
            attn = jnp.where(causal, _dot_nt(_bf(q_dec * kmasks[h]), k_inv), 0.0)
            o = o + _dot(_bf(attn), _bf(v * vmasks[h]))
        contrib = jnp.where(bd, _dot_tn(_bf(v), k_dec), 0.0)
        st_ref[...] = st_prev * jnp.exp(b_last) + contrib
        ms = _dot(o * o, segmean, HI)
        out = o * lax.rsqrt(ms + NORM_EPS) * ng_ref[...] * _silu(og_all[rows])
        o_ref[rows, :] = out.astype(o_ref.dtype)


def _gla_mixer(u, bsz, seq, gate_up, gate_b, norm_g):
    tb = MIXER_TB
    nb = seq // tb
    gup = jnp.zeros((128, GLA_DK), F32).at[MISC_GLA:MISC_GLA + 16].set(gate_up).astype(BF16)
    args = (u, u, u, u, gup, gate_b[None, :], norm_g[None, :])
    in_specs = [_u_spec(tb, GROUP_W, COL_GLA_QK, nb), _u_spec(tb, GROUP_W, COL_GLA_V, nb),
                _u_spec(tb, GROUP_W, COL_GLA_OG, nb), _u_spec(tb, 128, COL_MISC, nb)]
    in_specs += [_const_spec(a.shape) for a in args[4:]]
    return pl.pallas_call(
        functools.partial(_gla_body, tb=tb),
        grid=(bsz, nb),
        in_specs=in_specs,
        out_specs=pl.BlockSpec((tb, GROUP_W), lambda b, j: (b * nb + j, 0)),
        out_shape=jax.ShapeDtypeStruct((bsz * seq, GROUP_W), BF16),
        scratch_shapes=[pltpu.VMEM((GROUP_W, GLA_DK), F32)],
        compiler_params=_cparams(("arbitrary", "arbitrary")),
        name="gla_mixer",
    )(*args)


def _mlstm_body(qk_ref, vo_ref, misc_ref, cw_ref, cb_ref, ib_ref, fb_ref, ng_ref, o_ref,
                qkbuf, c_ref, n_ref, m_ref, *, tb):
    j = pl.program_id(1)
    W = GROUP_W

    @pl.when(j == 0)
    def _():
        qkbuf[0:8, :] = jnp.zeros((8, 2 * W), F32)
        c_ref[...] = jnp.zeros(c_ref.shape, F32)
        n_ref[...] = jnp.zeros(n_ref.shape, F32)
        m_ref[...] = jnp.zeros(m_ref.shape, F32)

    qkbuf[8:8 + tb, :] = qk_ref[...]
    qk = _causal_conv_silu(qkbuf, cw_ref, cb_ref, tb)
    qkbuf[0:8, :] = qkbuf[tb:tb + 8, :]
    q_all = qk[:, 0:W] * (HEAD_DIM ** -0.5)
    k_all = qk[:, W:2 * W]
    v_all = vo_ref[:, 0:W]
    og_all = vo_ref[:, W:2 * W]
    misc = misc_ref[...]
    ipre_all = misc + ib_ref[...]
    lf_all = _log_sigmoid(misc + fb_ref[...])

    L = CHUNK
    tri = _tri_incl(L)
    causal = _iota((L, L), 0) >= _iota((L, L), 1)
    masks = [_head_mask(W, HEAD_DIM, h) for h in range(N_HEADS)]
    bd = _seg_matrix(W, HEAD_DIM, 1.0)
    segmean = _seg_matrix(W, HEAD_DIM, 1.0 / HEAD_DIM)

    for c in range(tb // L):
        rows = slice(c * L, (c + 1) * L)
        q, k, v = q_all[rows], k_all[rows], v_all[rows]
        kb = _bf(k)
        bcs = _dot(tri, lf_all[rows], HI)
        ipre = ipre_all[rows]
        m_state = m_ref[...]
        num = jnp.zeros((L, W), F32)
        w_inter, rowsum, eneg, w_st, s_old, s_new, m_new = [], [], [], [], [], [], []
        for h in range(N_HEADS):
            b_col = bcs[:, MISC_F + h:MISC_F + h + 1]
            i_col = ipre[:, MISC_I + h:MISC_I + h + 1]
            b_last = bcs[L - 1:L, MISC_F + h:MISC_F + h + 1]
            m_prev = m_state[:, h:h + 1]
            dmat = jnp.where(causal, b_col - _col_to_row(b_col) + _col_to_row(i_col), -jnp.inf)
            a_st = b_last - b_col + i_col
            m_loc = jnp.max(a_st, axis=0, keepdims=True)
            w_st.append(jnp.exp(a_st - m_loc))
            mn = jnp.maximum(b_last + m_prev, m_loc)
            m_new.append(mn)
            s_old.append(jnp.exp(b_last + m_prev - mn))
            s_new.append(jnp.exp(m_loc - mn))
            m_inter = b_col + m_prev
            m_row = jnp.maximum(m_inter, jnp.max(dmat, axis=-1, keepdims=True))
            w_inter.append(jnp.exp(m_inter - m_row))
            eneg.append(jnp.exp(-m_row))
            scores = _dot_nt(_bf(q * masks[h]), kb) * jnp.exp(dmat - m_row)
            rowsum.append(jnp.sum(scores, axis=-1, keepdims=True))
            num = num + _dot(_bf(scores), _bf(v * masks[h]))
        c_prev = c_ref[...]
        n_prev = n_ref[...]
        wi = _expand(w_inter, HEAD_DIM)
        num = num + wi * _dot_nt(_bf(q), _bf(c_prev))
        den = wi * _dot(q * n_prev, bd, HI) + _expand(rowsum, HEAD_DIM)
        hval = num / jnp.maximum(jnp.abs(den), _expand(eneg, HEAD_DIM))
        so = _expand(s_old, HEAD_DIM)
        sn = _expand(s_new, HEAD_DIM)
        kw = k * _expand(w_st, HEAD_DIM)
        vw = v * _expand(w_st, HEAD_DIM)
        c_ref[...] = so * c_prev + sn * (bd * _dot_tn(_bf(vw), kb))
        n_ref[...] = so * n_prev + sn * jnp.sum(kw, axis=0, keepdims=True)
        lane = _iota((1, 128), 1)
        m_vec = jnp.zeros((1, 128), F32)
        for h in range(N_HEADS):
            m_vec = jnp.where(lane == h, m_new[h], m_vec)
        m_ref[...] = m_vec
        hv = hval * _sigmoid(og_all[rows])
        mean = _dot(hv, segmean, HI)
        xc = hv - mean
        var = _dot(xc * xc, segmean, HI)
        o_ref[rows, :] = (xc * lax.rsqrt(var + NORM_EPS) * ng_ref[...]).astype(o_ref.dtype)


def _mlstm_mixer(u, bsz, seq, conv_w, conv_b, i_b, f_b, norm_g):
    tb = MIXER_TB
    nb = seq // tb
    ib = jnp.zeros((1, 128), F32).at[0, MISC_I:MISC_I + N_HEADS].set(i_b)
    fb = jnp.zeros((1, 128), F32).at[0, MISC_F:MISC_F + N_HEADS].set(f_b)
    args = (u, u, u, conv_w, conv_b[None, :], ib, fb, norm_g[None, :])
    in_specs = [_u_spec(tb, 2 * GROUP_W, COL_ML_QK, nb), _u_spec(tb, 2 * GROUP_W, COL_ML_VO, nb),
                _u_spec(tb, 128, COL_MISC, nb)]
    in_specs += [_const_spec(a.shape) for a in args[3:]]
    return pl.pallas_call(
        functools.partial(_mlstm_body, tb=tb),
        grid=(bsz, nb),
        in_specs=in_specs,
        out_specs=pl.BlockSpec((tb, GROUP_W), lambda b, j: (b * nb + j, 0)),
        out_shape=jax.ShapeDtypeStruct((bsz * seq, GROUP_W), BF16),
        scratch_shapes=[pltpu.VMEM((tb + 8, 2 * GROUP_W), F32),
                        pltpu.VMEM((GROUP_W, GROUP_W), F32),
                        pltpu.VMEM((1, GROUP_W), F32),
                        pltpu.VMEM((1, 128), F32)],
        compiler_params=_cparams(("arbitrary", "arbitrary")),
        name="mlstm_mixer",
    )(*args)


def _stack_heads(x, masks):
    return jnp.concatenate([x * m for m in masks], axis=0)


def _rwkv_body(rkv_ref, lora_ref, mu_rkv_ref, mu_lora_ref, w0_ref, wup_ref, a0_ref, aup_ref,
               gup_ref, kk_ref, ka_ref, rk_ref, lng_ref, lnb_ref, o_ref,
               rbuf, lbuf, s_ref, *, tb):
    j = pl.program_id(1)
    W = GROUP_W

    @pl.when(j == 0)
    def _():
        rbuf[0:8, :] = jnp.zeros((8, 3 * W), F32)
        lbuf[0:8, :] = jnp.zeros((8, 128), F32)
        s_ref[...] = jnp.zeros(s_ref.shape, F32)

    rbuf[8:8 + tb, :] = rkv_ref[...]
    lbuf[8:8 + tb, :] = lora_ref[...]
    rkv = rkv_ref[...]
    lora = lora_ref[...]
    rkv = rkv + (rbuf[pl.ds(7, tb), :] - rkv) * mu_rkv_ref[...]
    lora = lora + (lbuf[pl.ds(7, tb), :] - lora) * mu_lora_ref[...]
    rbuf[0:8, :] = rbuf[tb:tb + 8, :]
    lbuf[0:8, :] = lbuf[tb:tb + 8, :]

    r_all, k_all, v_all = rkv[:, 0:W], rkv[:, W:2 * W], rkv[:, 2 * W:3 * W]
    wpre = w0_ref[...] + _dot(jnp.tanh(lora), wup_ref[...], HI)
    lw_all = -jnp.exp(-_softplus(-wpre) - 0.5)
    a_all = _sigmoid(a0_ref[...] + _dot(lora, aup_ref[...], HI))
    g_all = _dot(_sigmoid(lora), gup_ref[...], HI)
    segsum = _seg_matrix(W, HEAD_DIM, 1.0)
    segmean = _seg_matrix(W, HEAD_DIM, 1.0 / HEAD_DIM)
    kk = k_all * kk_ref[...]
    kk = kk / jnp.maximum(jnp.sqrt(_dot(kk * kk, segsum, HI)), 1e-12)
    k2_all = k_all * (1.0 + (a_all - 1.0) * ka_ref[...])
    av_all = -kk
    bv_all = kk * a_all

    L = CHUNK
    HL = N_HEADS * L
    tri = _tri_incl(L)
    masks = [_head_mask(W, HEAD_DIM, h) for h in range(N_HEADS)]
    t_idx = _iota((L, HL), 0)
    s_idx = jnp.bitwise_and(_iota((L, HL), 1), L - 1)
    strict = s_idx < t_idx
    incl = s_idx <= t_idx
    eye = (_iota((HL, HL), 0) == _iota((HL, HL), 1)).astype(F32)

    for c in range(tb // L):
        rows = slice(c * L, (c + 1) * L)
        lw = lw_all[rows]
        cum = _dot(tri, lw, HI)
        cum_last = cum[L - 1:L, :]
        w_inv = jnp.exp(-cum)
        w_dec = jnp.exp(cum_last - cum)
        r, k2, v = r_all[rows], k2_all[rows], v_all[rows]
        rt = r * jnp.exp(cum)
        at = av_all[rows] * jnp.exp(cum - lw)
        kt = k2 * w_inv
        bt = bv_all[rows] * w_inv
        bt_bd = _stack_heads(bt, masks)
        kt_bd = _stack_heads(kt, masks)
        v_bd = _stack_heads(v, masks)
        ab = jnp.where(strict, _dot_nt(at, bt_bd, HI), 0.0)
        ak = jnp.where(strict, _dot_nt(at, kt_bd, HI), 0.0)
        rb = jnp.where(incl, _dot_nt(rt, bt_bd, HI), 0.0)
        rk = jnp.where(incl, _dot_nt(rt, kt_bd, HI), 0.0)
        p = _stack_heads(ab, masks)
        minv = eye + p
        for _ in range(5):
            p = _dot(p, p, HI)
            minv = minv + _dot(minv, p, HI)
        s_prev = s_ref[...]
        rhs = _dot_nt(_stack_heads(at, masks), s_prev, HI) + _dot(_stack_heads(ak, masks), v_bd, HI)
        sa_bd = _dot(minv, rhs, HI)
        o = _dot_nt(rt, s_prev, HI) + _dot(rb, sa_bd, HI) + _dot(rk, v_bd, HI)
        s_ref[...] = (s_prev * jnp.exp(cum_last)
                      + _dot_tn(sa_bd, _stack_heads(bv_all[rows] * w_dec, masks), HI)
                      + _dot_tn(v_bd, _stack_heads(k2 * w_dec, masks), HI))
        mean = _dot(o, segmean, HI)
        oc = o - mean
        var = _dot(oc * oc, segmean, HI)
        on = oc * lax.rsqrt(var + RWKV_GN_EPS) * lng_ref[...] + lnb_ref[...]
        bonus = _dot(r * k2 * rk_ref[...], segsum, HI) * v
        o_ref[rows, :] = ((on + bonus) * g_all[rows]).astype(o_ref.dtype)


def _rwkv_mixer(u, bsz, seq, mu, w0, w_up, a0, a_up, g_up, k_k, k_a, r_k, ln_g, ln_b):
    tb = MIXER_TB
    nb = seq // tb
    W = GROUP_W
    wup = jnp.zeros((128, W), F32).at[0:32].set(w_up)
    aup = jnp.zeros((128, W), F32).at[32:64].set(a_up)
    gup = jnp.zeros((128, W), F32).at[64:128].set(g_up)
    args = (u, u, mu[None, :3 * W], mu[None, 3 * W:], w0[None, :], wup, a0[None, :], aup, gup,
            k_k[None, :], k_a[None, :], r_k.reshape(1, W), ln_g[None, :], ln_b[None, :])
    in_specs = [_u_spec(tb, 3 * W, COL_RWKV, nb), _u_spec(tb, 128, COL_LORA, nb)]
    in_specs += [_const_spec(a.shape) for a in args[2:]]
    return pl.pallas_call(
        functools.partial(_rwkv_body, tb=tb),
        grid=(bsz, nb),
        in_specs=in_specs,
        out_specs=pl.BlockSpec((tb, W), lambda b, j: (b * nb + j, 0)),
        out_shape=jax.ShapeDtypeStruct((bsz * seq, W), BF16),
        scratch_shapes=[pltpu.VMEM((tb + 8, 3 * W), F32), pltpu.VMEM((tb + 8, 128), F32),
                        pltpu.VMEM((W, W), F32)],
        compiler_params=_cparams(("arbitrary", "arbitrary")),
        name="rwkv_mixer",
    )(*args)


def _outproj_body(x_ref, y0_ref, y1_ref, y2_ref, y3_ref, w_ref, g_ref, b_ref, of_ref, ob_ref):
    W = GROUP_W
    mix = _dot(y0_ref[...], w_ref[0:W, :])
    mix = mix + _dot(y1_ref[...], w_ref[W:2 * W, :])
    mix = mix + _dot(y2_ref[...], w_ref[2 * W:3 * W, :])
    mix = mix + _dot(y3_ref[...], w_ref[3 * W:4 * W, :])
    out = _layer_norm(DEEPNORM_ALPHA * x_ref[...] + mix, g_ref[...], b_ref[...])
    of_ref[...] = out
    ob_ref[...] = out.astype(BF16)


def _out_proj_ln(x, ys, w_bf, g, b, tm=512):
    t, d = x.shape
    row = lambda i: (i, 0)
    const = lambda i: (0, 0)
    return pl.pallas_call(
        _outproj_body,
        grid=(t // tm,),
        in_specs=[pl.BlockSpec((tm, d), row)] + [pl.BlockSpec((tm, GROUP_W), row)] * 4
        + [pl.BlockSpec((d, d), const), pl.BlockSpec((1, d), const), pl.BlockSpec((1, d), const)],
        out_specs=[pl.BlockSpec((tm, d), row), pl.BlockSpec((tm, d), row)],
        out_shape=[jax.ShapeDtypeStruct((t, d), F32), jax.ShapeDtypeStruct((t, d), BF16)],
        compiler_params=_cparams(("arbitrary",)),
        name="out_proj_ln",
    )(x, *ys, w_bf, g[None, :], b[None, :])


def _router_body(x_ref, wt_ref, b_ref, idx_ref, gate_ref, rank_ref, cnt_ref, carry_ref, *, tr):
    i = pl.program_id(0)

    @pl.when(i == 0)
    def _():
        carry_ref[...] = jnp.zeros(carry_ref.shape, F32)

    logits = _dot_nt(wt_ref[...], x_ref[...], HI) + b_ref[...][:, 0:1]
    e_iota = _iota((N_EXPERTS, tr), 0)
    work = logits
    onehot = jnp.zeros((N_EXPERTS, tr), F32)
    sels, vals, idxs = [], [], []
    for _ in range(TOP_K):
        m = jnp.max(work, axis=0, keepdims=True)
        idx = jnp.min(jnp.where(work == m, e_iota, N_EXPERTS), axis=0, keepdims=True)
        sel = e_iota == idx
        work = jnp.where(sel, -jnp.inf, work)
        onehot = onehot + sel.astype(F32)
        sels.append(sel)
        vals.append(m)
        idxs.append(idx)
    exps = [jnp.exp(v - vals[0]) for v in vals]
    tot = exps[0] + exps[1] + exps[2] + exps[3]
    upper = (_iota((tr, tr), 0) < _iota((tr, tr), 1)).astype(BF16)
    carry = carry_ref[...][:, 0:1]
    before = _dot(_bf(onehot), upper) + carry
    ranks = [jnp.sum(jnp.where(s, before, 0.0), axis=0, keepdims=True) for s in sels]
    idx_ref[...] = jnp.concatenate(idxs, axis=0)
    gate_ref[...] = jnp.concatenate([e / tot for e in exps], axis=0)
    rank_ref[...] = jnp.concatenate(ranks, axis=0).astype(jnp.int32)
    new_carry = carry + jnp.sum(onehot, axis=1, keepdims=True)
    carry_ref[...] = jnp.broadcast_to(new_carry, carry_ref.shape)
    cnt_ref[...] = jnp.broadcast_to(new_carry, cnt_ref.shape)


def _router(x, w_t, b, tr=512):
    t, d = x.shape
    return pl.pallas_call(
        functools.partial(_router_body, tr=tr),
        grid=(t // tr,),
        in_specs=[pl.BlockSpec((tr, d), lambda i: (i, 0)),
                  pl.BlockSpec((N_EXPERTS, d), lambda i: (0, 0)),
                  pl.BlockSpec((N_EXPERTS, 128), lambda i: (0, 0))],
        out_specs=[pl.BlockSpec((TOP_K, tr), lambda i: (0, i)),
                   pl.BlockSpec((TOP_K, tr), lambda i: (0, i)),
                   pl.BlockSpec((TOP_K, tr), lambda i: (0, i)),
                   pl.BlockSpec((N_EXPERTS, 128), lambda i: (0, 0))],
        out_shape=[jax.ShapeDtypeStruct((TOP_K, t), jnp.int32),
                   jax.ShapeDtypeStruct((TOP_K, t), F32),
                   jax.ShapeDtypeStruct((TOP_K, t), jnp.int32),
                   jax.ShapeDtypeStruct((N_EXPERTS, 128), F32)],
        scratch_shapes=[pltpu.VMEM((N_EXPERTS, 128), F32)],
        compiler_params=_cparams(("arbitrary",)),
        name="router",
    )(x, w_t, jnp.broadcast_to(b[:, None], (N_EXPERTS, 128)))


def _route_plan_body(idx_ref, rank_ref, cnt_ref, dest_ref, be_ref, nv_ref, *, blk, nb_pad):
    E = N_EXPERTS
    cnt = cnt_ref[...][:, 0:1]
    padded = jnp.floor((cnt + (blk - 1)) / blk) * blk
    lower = (_iota((E, E), 1) < _iota((E, E), 0)).astype(F32)
    pstart = jnp.sum(lower * _col_to_row(padded), axis=1, keepdims=True)
    pend = pstart + padded
    idx = idx_ref[...]
    dest = rank_ref[...].astype(F32)
    for e in range(E):
        dest = dest + jnp.where(idx == e, pstart[e:e + 1, 0:1], 0.0)
    dest_ref[...] = dest.astype(jnp.int32)
    blk_start = (_iota((1, nb_pad), 1) * blk).astype(F32)
    be = jnp.sum((pend <= blk_start).astype(F32), axis=0, keepdims=True)
    be_ref[...] = jnp.minimum(be, E - 1).astype(jnp.int32)
    nv_ref[...] = jnp.broadcast_to(pend[E - 1:E, 0:1] / blk, nv_ref.shape).astype(jnp.int32)


def _route_plan(idx_t, rank_t, counts, blk, nb_pad):
    t = idx_t.shape[1]
    return pl.pallas_call(
        functools.partial(_route_plan_body, blk=blk, nb_pad=nb_pad),
        out_shape=[jax.ShapeDtypeStruct((TOP_K, t), jnp.int32),
                   jax.ShapeDtypeStruct((1, nb_pad), jnp.int32),
                   jax.ShapeDtypeStruct((1, 128), jnp.int32)],
        compiler_params=pltpu.CompilerParams(vmem_limit_bytes=VMEM_LIMIT),
        name="route_plan",
    )(idx_t, rank_t, counts)


def _dispatch_body(dest_ref, x_hbm, xs_in, xs_hbm, sem, *, td):
    del xs_in
    base = pl.program_id(0) * td

    def issue(t, carry):
        for k in range(TOP_K):
            pltpu.make_async_copy(x_hbm.at[pl.ds(base + t, 1)],
                                  xs_hbm.at[pl.ds(dest_ref[k, t], 1)], sem).start()
        return carry

    lax.fori_loop(0, td, issue, 0)

    def drain(t, carry):
        for k in range(TOP_K):
            pltpu.make_async_copy(x_hbm.at[pl.ds(0, 1)], xs_hbm.at[pl.ds(0, 1)], sem).wait()
        return carry

    lax.fori_loop(0, td, drain, 0)


def _dispatch(dest_t, x, n_rows, td=512):
    t, d = x.shape
    xs0 = jnp.zeros((n_rows, d), x.dtype)
    return pl.pallas_call(
        functools.partial(_dispatch_body, td=td),
        grid=(t // td,),
        in_specs=[pl.BlockSpec((TOP_K, td), lambda i: (0, i), memory_space=pltpu.SMEM),
                  pl.BlockSpec(memory_space=pl.ANY),
                  pl.BlockSpec(memory_space=pl.ANY)],
        out_specs=pl.BlockSpec(memory_space=pl.ANY),
        out_shape=jax.ShapeDtypeStruct((n_rows, d), x.dtype),
        scratch_shapes=[pltpu.SemaphoreType.DMA(())],
        input_output_aliases={2: 0},
        compiler_params=_cparams(("arbitrary",)),
        name="dispatch",
    )(dest_t, x, xs0)


def _expert_body(be_ref, nv_ref, xs_ref, wgu_ref, bgu_ref, wd_ref, bd_ref, ys_ref, wgu_bf, wd_bf):
    b = pl.program_id(0)
    d = D_MODEL

    @pl.when(b < nv_ref[0])
    def _():
        prev = be_ref[jnp.maximum(b - 1, 0)]

        @pl.when((b == 0) | (be_ref[b] != prev))
        def _():
            wgu_bf[...] = wgu_ref[...].astype(BF16)
            wd_bf[...] = wd_ref[...].astype(BF16)

        h = _dot(_bf(xs_ref[...]), wgu_bf[...]) + bgu_ref[...]
        hg = jnp.minimum(h[:, 0:d], SWIGLU_LIMIT)
        hl = jnp.clip(h[:, d:2 * d], -SWIGLU_LIMIT, SWIGLU_LIMIT)
        act = (hl + 1.0) * (hg * _sigmoid(hg * SWIGLU_ALPHA))
        ys_ref[...] = _dot(_bf(act), wd_bf[...]) + bd_ref[...]

    @pl.when(b >= nv_ref[0])
    def _():
        ys_ref[...] = jnp.zeros(ys_ref.shape, F32)


def _experts(be, nv, xs, w_gu, b_gu, w_down, b_down, layer, blk):
    n_rows, d = xs.shape
    nb = n_rows // blk

    def row(b, be_r, nv_r):
        return (jnp.minimum(b, nv_r[0] - 1), 0)

    def wsel(b, be_r, nv_r):
        return (layer, be_r[jnp.minimum(b, nv_r[0] - 1)], 0, 0)

    grid_spec = pltpu.PrefetchScalarGridSpec(
        num_scalar_prefetch=2,
        grid=(nb,),
        in_specs=[pl.BlockSpec((blk, d), row),
                  pl.BlockSpec((None, None, d, 2 * d), wsel),
                  pl.BlockSpec((None, None, 1, 2 * d), wsel),
                  pl.BlockSpec((None, None, d, d), wsel),
                  pl.BlockSpec((None, None, 1, d), wsel)],
        out_specs=pl.BlockSpec((blk, d), lambda b, be_r, nv_r: (b, 0)),
        scratch_shapes=[pltpu.VMEM((d, 2 * d), BF16), pltpu.VMEM((d, d), BF16)],
    )
    return pl.pallas_call(
        _expert_body,
        grid_spec=grid_spec,
        out_shape=jax.ShapeDtypeStruct((n_rows, d), F32),
        compiler_params=_cparams(("arbitrary",)),
        name="experts",
    )(be, nv, xs, w_gu, b_gu, w_down, b_down)


def _combine_body(dcur_ref, dnext_ref, ys_hbm, x_ref, gate_ref, p_ref, wg_ref, wp_ref, g_ref, b_ref,
                  of_ref, ob_ref, ybuf, sem, *, tc):
    i = pl.program_id(0)
    n = pl.num_programs(0)
    slot = lax.rem(i, 2)

    def issue(d_ref, s):
        def body(t, carry):
            for k in range(TOP_K):
                pltpu.make_async_copy(ys_hbm.at[pl.ds(d_ref[k, t], 1)],
                                      ybuf.at[s, k, pl.ds(t, 1)], sem.at[s]).start()
            return carry
        lax.fori_loop(0, tc, body, 0)

    @pl.when(i == 0)
    def _():
        issue(dcur_ref, 0)

    @pl.when(i + 1 < n)
    def _():
        issue(dnext_ref, 1 - slot)

    def drain(t, carry):
        for k in range(TOP_K):
            pltpu.make_async_copy(ys_hbm.at[pl.ds(0, 1)], ybuf.at[slot, k, pl.ds(0, 1)],
                                  sem.at[slot]).wait()
        return carry

    lax.fori_loop(0, tc, drain, 0)

    gate = gate_ref[...]
    ffn = ybuf[slot, 0] * gate[:, 0:1]
    for k in range(1, TOP_K):
        ffn = ffn + ybuf[slot, k] * gate[:, k:k + 1]
    h = DEEPNORM_ALPHA * x_ref[...] + ffn
    ple = _dot(_bf(p_ref[...]), wp_ref[...])
    h = h + _sigmoid(_dot(_bf(h), wg_ref[...])) * ple
    out = _layer_norm(h, g_ref[...], b_ref[...])
    of_ref[...] = out
    ob_ref[...] = out.astype(BF16)


def _combine(dest_t, ys, x, gate, p, wg_bf, wp_bf, g, b, layer, tc=256):
    t, d = x.shape
    n = t // tc
    row = lambda i: (i, 0)
    const = lambda i: (0, 0)
    return pl.pallas_call(
        functools.partial(_combine_body, tc=tc),
        grid=(n,),
        in_specs=[pl.BlockSpec((TOP_K, tc), lambda i: (0, i), memory_space=pltpu.SMEM),
                  pl.BlockSpec((TOP_K, tc), lambda i: (0, jnp.minimum(i + 1, n - 1)),
                               memory_space=pltpu.SMEM),
                  pl.BlockSpec(memory_space=pl.ANY),
                  pl.BlockSpec((tc, d), row),
                  pl.BlockSpec((tc, TOP_K), row),
                  pl.BlockSpec((None, tc, PLE_DIM), lambda i: (layer, i, 0)),
                  pl.BlockSpec((d, d), const),
                  pl.BlockSpec((PLE_DIM, d), const),
                  pl.BlockSpec((1, d), const),
                  pl.BlockSpec((1, d), const)],
        out_specs=[pl.BlockSpec((tc, d), row), pl.BlockSpec((tc, d), row)],
        out_shape=[jax.ShapeDtypeStruct((t, d), F32), jax.ShapeDtypeStruct((t, d), BF16)],
        scratch_shapes=[pltpu.VMEM((2, TOP_K, tc, d), F32), pltpu.SemaphoreType.DMA((2,))],
        compiler_params=_cparams(("arbitrary",)),
        name="combine",
    )(dest_t, dest_t, ys, x, gate, p, wg_bf, wp_bf, g[None, :], b[None, :])


def _pack_w_in(w):
    o_rwkv = 772
    o_gla = o_rwkv + 896
    o_ml = o_gla + 784
    z = lambda n: jnp.zeros((w.shape[0], n), w.dtype)
    parts = [
        w[:, 0:768],
        w[:, o_rwkv:o_rwkv + 768],
        w[:, o_ml:o_ml + 512],
        w[:, o_ml + 512:o_ml + 768],
        w[:, o_ml + 776:o_ml + 1032],
        w[:, o_gla:o_gla + 512],
        w[:, o_gla + 528:o_gla + 784],
        w[:, o_rwkv + 768:o_rwkv + 896],
        w[:, 768:772],
        w[:, o_gla + 512:o_gla + 528],
        w[:, o_ml + 768:o_ml + 776],
        z(128 - 28),
    ]
    return jnp.concatenate(parts, axis=1).astype(BF16)


def kernel(x, p, w_in, w_out, ln1_g, ln1_b, ssd_conv_w, ssd_conv_b, ssd_a_log, ssd_dt_bias, ssd_d, ssd_norm_g, rwkv_mu, rwkv_w0, rwkv_w_up, rwkv_a0, rwkv_a_up, rwkv_g_up, rwkv_k_k, rwkv_k_a, rwkv_r_k, rwkv_ln_g, rwkv_ln_b, gla_gate_up, gla_gate_b, gla_norm_g, mlstm_conv_w, mlstm_conv_b, mlstm_i_b, mlstm_f_b, mlstm_norm_g, router_w, router_b, exp_w_gu, exp_b_gu, exp_w_down, exp_b_down, ple_gate_w, ple_proj, ln2_g, ln2_b):
    bsz, seq, d = x.shape
    t = bsz * seq
    depth = w_in.shape[0]
    blk = EXPERT_BLK
    n_blocks = -(-(t * TOP_K + N_EXPERTS * (blk - 1)) // blk)
    n_rows = n_blocks * blk
    nb_pad = -(-n_blocks // 128) * 128

    xf = x.reshape(t, d)
    xb = xf.astype(BF16)
    p2 = p.reshape(depth, t, PLE_DIM)
    b_gu4 = exp_b_gu[:, :, None, :]
    b_down4 = exp_b_down[:, :, None, :]
    for i in range(depth):
        u = _in_proj(xb, _pack_w_in(w_in[i]))
        y_ssd = _ssd_mixer(u, bsz, seq, ssd_conv_w[i], ssd_conv_b[i], ssd_a_log[i], ssd_dt_bias[i],
                           ssd_d[i], ssd_norm_g[i])
        y_rwkv = _rwkv_mixer(u, bsz, seq, rwkv_mu[i], rwkv_w0[i], rwkv_w_up[i], rwkv_a0[i],
                             rwkv_a_up[i], rwkv_g_up[i], rwkv_k_k[i], rwkv_k_a[i], rwkv_r_k[i],
                             rwkv_ln_g[i], rwkv_ln_b[i])
        y_gla = _gla_mixer(u, bsz, seq, gla_gate_up[i], gla_gate_b[i], gla_norm_g[i])
        y_ml = _mlstm_mixer(u, bsz, seq, mlstm_conv_w[i], mlstm_conv_b[i], mlstm_i_b[i],
                            mlstm_f_b[i], mlstm_norm_g[i])
        x1f, _ = _out_proj_ln(xf, (y_ssd, y_rwkv, y_gla, y_ml), w_out[i].astype(BF16),
                              ln1_g[i], ln1_b[i])
        idx_t, gate_t, rank_t, counts = _router(x1f, router_w[i].T, router_b[i])
        dest_t, be, nv = _route_plan(idx_t, rank_t, counts, blk, nb_pad)
        xs = _dispatch(dest_t, x1f, n_rows)
        ys = _experts(be.reshape(nb_pad), nv.reshape(128), xs, exp_w_gu, b_gu4, exp_w_down,
                      b_down4, i, blk)
        xf, xb = _combine(dest_t, ys, x1f, gate_t.T, p2, ple_gate_w[i].astype(BF16),
                          ple_proj[i].astype(BF16), ln2_g[i], ln2_b[i], i)
    return xf.reshape(bsz, seq, d)
```

```python
import functools

import jax
import jax.numpy as jnp
import numpy as np
from jax import lax
from jax.experimental import pallas as pl
from jax.experimental.pallas import tpu as pltpu

F32 = jnp.float32
BF16 = jnp.bfloat16
HI = lax.Precision.HIGHEST

D_MODEL = 1024
DEPTH = 4
GROUP_W = 256
N_HEADS = 4
HEAD_DIM = 64
SSD_STATE = 128
SSD_CHUNK = 128
GLA_DK = 128
GLA_HEAD_K = 32
GLA_TAU = 16.0
CHUNK = 64
RWKV_P_SCORE = 1
RWKV_P_INV = 1
RWKV_P_RHS = 1
RWKV_P_OUT = 1
RWKV_P_STATE = 1
RWKV_GN_EPS = 64e-5
NORM_EPS = 1e-5
LN_EPS = 1e-5
N_EXPERTS = 32
TOP_K = 4
SWIGLU_LIMIT = 7.0
SWIGLU_ALPHA = 1.702
PLE_DIM = 256
DEEPNORM_ALPHA = (2 * DEPTH) ** 0.25

U_WIDTH = 3584
COL_SSD_Z, COL_SSD_X, COL_SSD_BC = 0, 256, 512
COL_RWKV = 768
COL_ML_QK = 1536
COL_ML_VO = 2048
COL_GLA_QK, COL_GLA_V, COL_GLA_OG = 2560, 2816, 3072
COL_LORA = 3328
COL_MISC = 3456
MISC_DT, MISC_GLA, MISC_I, MISC_F = 0, 4, 20, 24

MIXER_TB = 256
EXPERT_BLK = 256
VMEM_LIMIT = 56 * 1024 * 1024


def _cparams(sem):
    return pltpu.CompilerParams(dimension_semantics=sem, vmem_limit_bytes=VMEM_LIMIT)


def _dot(a, b, prec=None):
    return lax.dot_general(a, b, (((1,), (0,)), ((), ())), precision=prec,
                           preferred_element_type=F32)


def _dot_nt(a, b, prec=None):
    return lax.dot_general(a, b, (((1,), (1,)), ((), ())), precision=prec,
                           preferred_element_type=F32)


def _dot_tn(a, b, prec=None):
    return lax.dot_general(a, b, (((0,), (0,)), ((), ())), precision=prec,
                           preferred_element_type=F32)


def _bf(x):
    return x.astype(BF16)


_DIMS = {"nn": (((1,), (0,)), ((), ())), "nt": (((1,), (1,)), ((), ())),
         "tn": (((0,), (0,)), ((), ()))}


def _split_bf(x):
    hi = x.astype(BF16)
    return hi, (x - hi.astype(F32)).astype(BF16)


def _pdot(a, b, passes, kind="nn"):
    dn = _DIMS[kind]
    if passes == 6:
        return lax.dot_general(a, b, dn, precision=HI, preferred_element_type=F32)
    dg = lambda x, y: lax.dot_general(x, y, dn, preferred_element_type=F32)
    if passes == 1:
        return dg(_bf(a), _bf(b))
    ah, al = _split_bf(a)
    bh, bl = _split_bf(b)
    return dg(ah, bh) + (dg(ah, bl) + dg(al, bh))


def _pieces(x, n):
    out, rem = [], x
    for _ in range(n):
        part = rem.astype(BF16)
        out.append(part)
        rem = rem - part.astype(F32)
    return out


def _dot_lx(a_exact, b, n):
    ab = _bf(a_exact)
    acc = None
    for part in _pieces(b, n):
        term = lax.dot_general(ab, part, _DIMS["nn"], preferred_element_type=F32)
        acc = term if acc is None else acc + term
    return acc


def _dot_rx(a, b_exact, n):
    bb = _bf(b_exact)
    acc = None
    for part in _pieces(a, n):
        term = lax.dot_general(part, bb, _DIMS["nn"], preferred_element_type=F32)
        acc = term if acc is None else acc + term
    return acc


def _sigmoid(x):
    return 1.0 / (1.0 + jnp.exp(-x))


def _silu(x):
    return x * _sigmoid(x)


def _softplus(x):
    return jnp.maximum(x, 0.0) + jnp.log1p(jnp.exp(-jnp.abs(x)))


def _log_sigmoid(x):
    return jnp.minimum(x, 0.0) - jnp.log1p(jnp.exp(-jnp.abs(x)))


def _iota(shape, dim):
    return lax.broadcasted_iota(jnp.int32, shape, dim)


def _tri_incl(n):
    return (_iota((n, n), 0) >= _iota((n, n), 1)).astype(F32)


def _seg_matrix(n, seg, value):
    sh = int(np.log2(seg))
    same = (lax.shift_right_logical(_iota((n, n), 0), sh)
            == lax.shift_right_logical(_iota((n, n), 1), sh))
    return jnp.where(same, value, 0.0).astype(F32)


def _head_mask(width, seg, h):
    lane = _iota((1, width), 1)
    return ((lane >= h * seg) & (lane < (h + 1) * seg)).astype(F32)


def _expand(cols, seg):
    n = len(cols)
    rows = cols[0].shape[0]
    width = n * seg
    lane = _iota((rows, width), 1)
    out = jnp.broadcast_to(cols[n - 1], (rows, width))
    for h in range(n - 2, -1, -1):
        out = jnp.where(lane < (h + 1) * seg, jnp.broadcast_to(cols[h], (rows, width)), out)
    return out


def _col_to_row(col):
    n = col.shape[0]
    eye = _iota((n, n), 0) == _iota((n, n), 1)
    return jnp.sum(jnp.where(eye, col, 0.0), axis=0, keepdims=True)


def _layer_norm(x, g, b):
    mu = jnp.mean(x, axis=-1, keepdims=True)
    xc = x - mu
    var = jnp.mean(xc * xc, axis=-1, keepdims=True)
    return xc * lax.rsqrt(var + LN_EPS) * g + b


def _causal_conv_silu(buf, w_ref, b_ref, rows):
    acc = b_ref[...]
    for k in range(4):
        acc = acc + buf[pl.ds(5 + k, rows), :] * w_ref[k:k + 1, :]
    return _silu(acc)


def _matmul_body(x_ref, w_ref, o_ref):
    o_ref[...] = jnp.dot(x_ref[...], w_ref[...], preferred_element_type=F32)


def _in_proj(x_bf, w_bf, tm=512, tn=1792):
    t, k = x_bf.shape
    n = w_bf.shape[1]
    return pl.pallas_call(
        _matmul_body,
        grid=(n // tn, t // tm),
        in_specs=[pl.BlockSpec((tm, k), lambda j, i: (i, 0)),
                  pl.BlockSpec((k, tn), lambda j, i: (0, j))],
        out_specs=pl.BlockSpec((tm, tn), lambda j, i: (i, j)),
        out_shape=jax.ShapeDtypeStruct((t, n), F32),
        compiler_params=_cparams(("arbitrary", "arbitrary")),
        name="in_proj",
    )(x_bf, w_bf)


def _ssd_body(z_ref, x_ref, bc_ref, misc_ref, cwx_ref, cwbc_ref, cbx_ref, cbbc_ref,
              alog_ref, dtb_ref, dskip_ref, ng_ref, o_ref, xbuf, bcbuf, st_ref, *, tb):
    j = pl.program_id(1)

    @pl.when(j == 0)
    def _():
        xbuf[0:8, :] = jnp.zeros((8, GROUP_W), F32)
        bcbuf[0:8, :] = jnp.zeros((8, GROUP_W), F32)
        st_ref[...] = jnp.zeros(st_ref.shape, F32)

    xbuf[8:8 + tb, :] = x_ref[...]
    bcbuf[8:8 + tb, :] = bc_ref[...]
    xs_all = _causal_conv_silu(xbuf, cwx_ref, cbx_ref, tb)
    bc_all = _causal_conv_silu(bcbuf, cwbc_ref, cbbc_ref, tb)
    xbuf[0:8, :] = xbuf[tb:tb + 8, :]
    bcbuf[0:8, :] = bcbuf[tb:tb + 8, :]

    dt_all = _softplus(misc_ref[...] + dtb_ref[...])
    adt_all = dt_all * (-jnp.exp(alog_ref[...]))
    z_all = z_ref[...]
    L = SSD_CHUNK
    tri = _tri_incl(L)
    causal = _iota((L, L), 0) >= _iota((L, L), 1)
    masks = [_head_mask(GROUP_W, HEAD_DIM, h) for h in range(N_HEADS)]

    for c in range(tb // L):
        rows = slice(c * L, (c + 1) * L)
        xs = xs_all[rows]
        bm = _bf(bc_all[rows, 0:SSD_STATE])
        cm = _bf(bc_all[rows, SSD_STATE:2 * SSD_STATE])
        dt = dt_all[rows]
        acum = _dot_lx(tri, adt_all[rows], 3)
        cols = [acum[:, h:h + 1] for h in range(N_HEADS)]
        lasts = [acum[L - 1:L, h:h + 1] for h in range(N_HEADS)]
        xc = xs * _expand([dt[:, h:h + 1] for h in range(N_HEADS)], HEAD_DIM)
        g = _dot_nt(cm, bm)
        y = jnp.zeros((L, GROUP_W), F32)
        for h in range(N_HEADS):
            lmat = jnp.exp(jnp.where(causal, cols[h] - _col_to_row(cols[h]), -jnp.inf))
            y = y + _dot(_bf(g * lmat), _bf(xc * masks[h]))
        dec_states = _expand([jnp.exp(lasts[h] - cols[h]) for h in range(N_HEADS)], HEAD_DIM)
        st_prev = st_ref[...]
        y = y + _dot(cm, _bf(st_prev)) * _expand([jnp.exp(cols[h]) for h in range(N_HEADS)], HEAD_DIM)
        st_new = _dot_tn(bm, _bf(xc * dec_states))
        st_ref[...] = st_prev * _expand([jnp.exp(lasts[h]) for h in range(N_HEADS)], HEAD_DIM) + st_new
        y = y + xs * dskip_ref[...]
        y = y * _silu(z_all[rows])
        ms = jnp.mean(y * y, axis=-1, keepdims=True)
        o_ref[rows, :] = (y * lax.rsqrt(ms + NORM_EPS) * ng_ref[...]).astype(o_ref.dtype)


def _u_spec(tb, width, col, nb):
    return pl.BlockSpec((tb, width), lambda b, j: (b * nb + j, col // width))


def _const_spec(shape):
    return pl.BlockSpec(shape, lambda b, j: tuple(0 for _ in shape))


def _ssd_mixer(u, bsz, seq, conv_w, conv_b, a_log, dt_bias, d_skip, norm_g):
    tb = MIXER_TB
    nb = seq // tb
    pad4 = lambda v: jnp.zeros((1, 128), F32).at[0, :N_HEADS].set(v)
    args = (u, u, u, u,
            conv_w[:, :GROUP_W], conv_w[:, GROUP_W:],
            conv_b[None, :GROUP_W], conv_b[None, GROUP_W:],
            pad4(a_log), pad4(dt_bias),
            jnp.repeat(d_skip, HEAD_DIM)[None, :], norm_g[None, :])
    in_specs = [_u_spec(tb, GROUP_W, COL_SSD_Z, nb), _u_spec(tb, GROUP_W, COL_SSD_X, nb),
                _u_spec(tb, GROUP_W, COL_SSD_BC, nb), _u_spec(tb, 128, COL_MISC, nb)]
    in_specs += [_const_spec(a.shape) for a in args[4:]]
    return pl.pallas_call(
        functools.partial(_ssd_body, tb=tb),
        grid=(bsz, nb),
        in_specs=in_specs,
        out_specs=pl.BlockSpec((tb, GROUP_W), lambda b, j: (b * nb + j, 0)),
        out_shape=jax.ShapeDtypeStruct((bsz * seq, GROUP_W), BF16),
        scratch_shapes=[pltpu.VMEM((tb + 8, GROUP_W), F32), pltpu.VMEM((tb + 8, GROUP_W), F32),
                        pltpu.VMEM((SSD_STATE, GROUP_W), F32)],
        compiler_params=_cparams(("arbitrary", "arbitrary")),
        name="ssd_mixer",
    )(*args)


def _gla_body(qk_ref, v_ref, og_ref, misc_ref, gup_ref, gb_ref, ng_ref, o_ref, st_ref, *, tb):
    j = pl.program_id(1)

    @pl.when(j == 0)
    def _():
        st_ref[...] = jnp.zeros(st_ref.shape, F32)

    L = CHUNK
    q_all = qk_ref[:, 0:GLA_DK] * (GLA_HEAD_K ** -0.5)
    k_all = qk_ref[:, GLA_DK:2 * GLA_DK]
    v_all = v_ref[...]
    og_all = og_ref[...]
    gpre = _dot(_bf(misc_ref[...]), gup_ref[...]) + gb_ref[...]
    loga_all = _log_sigmoid(gpre) / GLA_TAU
    tri = _tri_incl(L)
    causal = _iota((L, L), 0) >= _iota((L, L), 1)
    kmasks = [_head_mask(GLA_DK, GLA_HEAD_K, h) for h in range(N_HEADS)]
    vmasks = [_head_mask(GROUP_W, HEAD_DIM, h) for h in range(N_HEADS)]
    bd = (lax.shift_right_logical(_iota((GROUP_W, GLA_DK), 0), 6)
          == lax.shift_right_logical(_iota((GROUP_W, GLA_DK), 1), 5))
    segmean = _seg_matrix(GROUP_W, HEAD_DIM, 1.0 / HEAD_DIM)

    for c in range(tb // L):
        rows = slice(c * L, (c + 1) * L)
        bcum = _dot_lx(tri, loga_all[rows], 3)
        b_last = bcum[L - 1:L, :]
        q_dec = q_all[rows] * jnp.exp(bcum)
        k_inv = _bf(k_all[rows] * jnp.exp(-bcum))
        k_dec = _bf(k_all[rows] * jnp.exp(b_last - bcum))
        v = v_all[rows]
        st_prev = st_ref[...]
        o = _dot_nt(_bf(q_dec), _bf(st_prev))
        for h in range(N_HEADS):
            attn = jnp.where(causal, _dot_nt(_bf(q_dec * kmasks[h]), k_inv), 0.0)
            o = o + _dot(_bf(attn), _bf(v * vmasks[h]))
        contrib = jnp.where(bd, _dot_tn(_bf(v), k_dec), 0.0)
        st_ref[...] = st_prev * jnp.exp(b_last) + contrib
        ms = _dot_rx(o * o, segmean, 2)
        out = o * lax.rsqrt(ms + NORM_EPS) * ng_ref[...] * _silu(og_all[rows])
        o_ref[rows, :] = out.astype(o_ref.dtype)


def _gla_mixer(u, bsz, seq, gate_up, gate_b, norm_g):
    tb = MIXER_TB
    nb = seq // tb
    gup = jnp.zeros((128, GLA_DK), F32).at[MISC_GLA:MISC_GLA + 16].set(gate_up).astype(BF16)
    args = (u, u, u, u, gup, gate_b[None, :], norm_g[None, :])
    in_specs = [_u_spec(tb, GROUP_W, COL_GLA_QK, nb), _u_spec(tb, GROUP_W, COL_GLA_V, nb),
                _u_spec(tb, GROUP_W, COL_GLA_OG, nb), _u_spec(tb, 128, COL_MISC, nb)]
    in_specs += [_const_spec(a.shape) for a in args[4:]]
    return pl.pallas_call(
        functools.partial(_gla_body, tb=tb),
        grid=(bsz, nb),
        in_specs=in_specs,
        out_specs=pl.BlockSpec((tb, GROUP_W), lambda b, j: (b * nb + j, 0)),
        out_shape=jax.ShapeDtypeStruct((bsz * seq, GROUP_W), BF16),
        scratch_shapes=[pltpu.VMEM((GROUP_W, GLA_DK), F32)],
        compiler_params=_cparams(("arbitrary", "arbitrary")),
        name="gla_mixer",
    )(*args)


def _mlstm_body(qk_ref, vo_ref, misc_ref, cw_ref, cb_ref, ib_ref, fb_ref, ng_ref, o_ref,
                qkbuf, c_ref, n_ref, m_ref, *, tb):
    j = pl.program_id(1)
    W = GROUP_W

    @pl.when(j == 0)
    def _():
        qkbuf[0:8, :] = jnp.zeros((8, 2 * W), F32)
        c_ref[...] = jnp.zeros(c_ref.shape, F32)
        n_ref[...] = jnp.zeros(n_ref.shape, F32)
        m_ref[...] = jnp.zeros(m_ref.shape, F32)

    qkbuf[8:8 + tb, :] = qk_ref[...]
    qk = _causal_conv_silu(qkbuf, cw_ref, cb_ref, tb)
    qkbuf[0:8, :] = qkbuf[tb:tb + 8, :]
    q_all = qk[:, 0:W] * (HEAD_DIM ** -0.5)
    k_all = qk[:, W:2 * W]
    v_all = vo_ref[:, 0:W]
    og_all = vo_ref[:, W:2 * W]
    misc = misc_ref[...]
    ipre_all = misc + ib_ref[...]
    lf_all = _log_sigmoid(misc + fb_ref[...])

    L = CHUNK
    tri = _tri_incl(L)
    causal = _iota((L, L), 0) >= _iota((L, L), 1)
    masks = [_head_mask(W, HEAD_DIM, h) for h in range(N_HEADS)]
    bd = _seg_matrix(W, HEAD_DIM, 1.0)
    segmean = _seg_matrix(W, HEAD_DIM, 1.0 / HEAD_DIM)

    for c in range(tb // L):
        rows = slice(c * L, (c + 1) * L)
        q, k, v = q_all[rows], k_all[rows], v_all[rows]
        kb = _bf(k)
        bcs = _dot_lx(tri, lf_all[rows], 3)
        ipre = ipre_all[rows]
        m_state = m_ref[...]
        num = jnp.zeros((L, W), F32)
        w_inter, rowsum, eneg, w_st, s_old, s_new, m_new = [], [], [], [], [], [], []
        for h in range(N_HEADS):
            b_col = bcs[:, MISC_F + h:MISC_F + h + 1]
            i_col = ipre[:, MISC_I + h:MISC_I + h + 1]
            b_last = bcs[L - 1:L, MISC_F + h:MISC_F + h + 1]
            m_prev = m_state[:, h:h + 1]
            dmat = jnp.where(causal, b_col - _col_to_row(b_col) + _col_to_row(i_col), -jnp.inf)
            a_st = b_last - b_col + i_col
            m_loc = jnp.max(a_st, axis=0, keepdims=True)
            w_st.append(jnp.exp(a_st - m_loc))
            mn = jnp.maximum(b_last + m_prev, m_loc)
            m_new.append(mn)
            s_old.append(jnp.exp(b_last + m_prev - mn))
            s_new.append(jnp.exp(m_loc - mn))
            m_inter = b_col + m_prev
            m_row = jnp.maximum(m_inter, jnp.max(dmat, axis=-1, keepdims=True))
            w_inter.append(jnp.exp(m_inter - m_row))
            eneg.append(jnp.exp(-m_row))
            scores = _dot_nt(_bf(q * masks[h]), kb) * jnp.exp(dmat - m_row)
            rowsum.append(jnp.sum(scores, axis=-1, keepdims=True))
            num = num + _dot(_bf(scores), _bf(v * masks[h]))
        c_prev = c_ref[...]
        n_prev = n_ref[...]
        wi = _expand(w_inter, HEAD_DIM)
        num = num + wi * _dot_nt(_bf(q), _bf(c_prev))
        den = wi * _dot_rx(q * n_prev, bd, 3) + _expand(rowsum, HEAD_DIM)
        hval = num / jnp.maximum(jnp.abs(den), _expand(eneg, HEAD_DIM))
        so = _expand(s_old, HEAD_DIM)
        sn = _expand(s_new, HEAD_DIM)
        kw = k * _expand(w_st, HEAD_DIM)
        vw = v * _expand(w_st, HEAD_DIM)
        c_ref[...] = so * c_prev + sn * (bd * _dot_tn(_bf(vw), kb))
        n_ref[...] = so * n_prev + sn * jnp.sum(kw, axis=0, keepdims=True)
        lane = _iota((1, 128), 1)
        m_vec = jnp.zeros((1, 128), F32)
        for h in range(N_HEADS):
            m_vec = jnp.where(lane == h, m_new[h], m_vec)
        m_ref[...] = m_vec
        hv = hval * _sigmoid(og_all[rows])
        mean = _dot_rx(hv, segmean, 2)
        xc = hv - mean
        var = _dot_rx(xc * xc, segmean, 2)
        o_ref[rows, :] = (xc * lax.rsqrt(var + NORM_EPS) * ng_ref[...]).astype(o_ref.dtype)


def _mlstm_mixer(u, bsz, seq, conv_w, conv_b, i_b, f_b, norm_g):
    tb = MIXER_TB
    nb = seq // tb
    ib = jnp.zeros((1, 128), F32).at[0, MISC_I:MISC_I + N_HEADS].set(i_b)
    fb = jnp.zeros((1, 128), F32).at[0, MISC_F:MISC_F + N_HEADS].set(f_b)
    args = (u, u, u, conv_w, conv_b[None, :], ib, fb, norm_g[None, :])
    in_specs = [_u_spec(tb, 2 * GROUP_W, COL_ML_QK, nb), _u_spec(tb, 2 * GROUP_W, COL_ML_VO, nb),
                _u_spec(tb, 128, COL_MISC, nb)]
    in_specs += [_const_spec(a.shape) for a in args[3:]]
    return pl.pallas_call(
        functools.partial(_mlstm_body, tb=tb),
        grid=(bsz, nb),
        in_specs=in_specs,
        out_specs=pl.BlockSpec((tb, GROUP_W), lambda b, j: (b * nb + j, 0)),
        out_shape=jax.ShapeDtypeStruct((bsz * seq, GROUP_W), BF16),
        scratch_shapes=[pltpu.VMEM((tb + 8, 2 * GROUP_W), F32),
                        pltpu.VMEM((GROUP_W, GROUP_W), F32),
                        pltpu.VMEM((1, GROUP_W), F32),
                        pltpu.VMEM((1, 128), F32)],
        compiler_params=_cparams(("arbitrary", "arbitrary")),
        name="mlstm_mixer",
    )(*args)


def _stack_heads(x, masks):
    return jnp.concatenate([x * m for m in masks], axis=0)


def _rwkv_body(rkv_ref, lora_ref, mu_rkv_ref, mu_lora_ref, w0_ref, wup_ref, a0_ref, aup_ref,
               gup_ref, kk_ref, ka_ref, rk_ref, lng_ref, lnb_ref, o_ref,
               rbuf, lbuf, s_ref, *, tb):
    j = pl.program_id(1)
    W = GROUP_W

    @pl.when(j == 0)
    def _():
        rbuf[0:8, :] = jnp.zeros((8, 3 * W), F32)
        lbuf[0:8, :] = jnp.zeros((8, 128), F32)
        s_ref[...] = jnp.zeros(s_ref.shape, F32)

    rbuf[8:8 + tb, :] = rkv_ref[...]
    lbuf[8:8 + tb, :] = lora_ref[...]
    rkv = rkv_ref[...]
    lora = lora_ref[...]
    rkv = rkv + (rbuf[pl.ds(7, tb), :] - rkv) * mu_rkv_ref[...]
    lora = lora + (lbuf[pl.ds(7, tb), :] - lora) * mu_lora_ref[...]
    rbuf[0:8, :] = rbuf[tb:tb + 8, :]
    lbuf[0:8, :] = lbuf[tb:tb + 8, :]

    r_all, k_all, v_all = rkv[:, 0:W], rkv[:, W:2 * W], rkv[:, 2 * W:3 * W]
    wpre = w0_ref[...] + _pdot(jnp.tanh(lora), wup_ref[...], 3)
    lw_all = -jnp.exp(-_softplus(-wpre) - 0.5)
    a_all = _sigmoid(a0_ref[...] + _pdot(lora, aup_ref[...], 3))
    g_all = _pdot(_sigmoid(lora), gup_ref[...], 3)
    segsum = _seg_matrix(W, HEAD_DIM, 1.0)
    segmean = _seg_matrix(W, HEAD_DIM, 1.0 / HEAD_DIM)
    kk = k_all * kk_ref[...]
    kk = kk / jnp.maximum(jnp.sqrt(_dot_rx(kk * kk, segsum, 2)), 1e-12)
    k2_all = k_all * (1.0 + (a_all - 1.0) * ka_ref[...])
    av_all = -kk
    bv_all = kk * a_all

    L = CHUNK
    HL = N_HEADS * L
    tri = _tri_incl(L)
    masks = [_head_mask(W, HEAD_DIM, h) for h in range(N_HEADS)]
    t_idx = _iota((L, HL), 0)
    s_idx = jnp.bitwise_and(_iota((L, HL), 1), L - 1)
    strict = s_idx < t_idx
    incl = s_idx <= t_idx
    eye = (_iota((HL, HL), 0) == _iota((HL, HL), 1)).astype(F32)

    for c in range(tb // L):
        rows = slice(c * L, (c + 1) * L)
        lw = lw_all[rows]
        cum = _dot_lx(tri, lw, 3)
        cum_last = cum[L - 1:L, :]
        w_inv = jnp.exp(-cum)
        w_dec = jnp.exp(cum_last - cum)
        r, k2, v = r_all[rows], k2_all[rows], v_all[rows]
        rt = r * jnp.exp(cum)
        at = av_all[rows] * jnp.exp(cum - lw)
        kt = k2 * w_inv
        bt = bv_all[rows] * w_inv
        bt_bd = _stack_heads(bt, masks)
        kt_bd = _stack_heads(kt, masks)
        v_bd = _stack_heads(v, masks)
        ab = jnp.where(strict, _pdot(at, bt_bd, RWKV_P_SCORE, "nt"), 0.0)
        ak = jnp.where(strict, _pdot(at, kt_bd, RWKV_P_SCORE, "nt"), 0.0)
        rb = jnp.where(incl, _pdot(rt, bt_bd, RWKV_P_SCORE, "nt"), 0.0)
        rk = jnp.where(incl, _pdot(rt, kt_bd, RWKV_P_SCORE, "nt"), 0.0)
        p = _stack_heads(ab, masks)
        minv = eye + p
        for _ in range(5):
            p = _pdot(p, p, RWKV_P_INV)
            minv = minv + _pdot(minv, p, RWKV_P_INV)
        s_prev = s_ref[...]
        rhs = (_pdot(_stack_heads(at, masks), s_prev, RWKV_P_RHS, "nt")
               + _pdot(_stack_heads(ak, masks), v_bd, RWKV_P_RHS))
        sa_bd = _pdot(minv, rhs, RWKV_P_RHS)
        o = (_pdot(rt, s_prev, RWKV_P_OUT, "nt") + _pdot(rb, sa_bd, RWKV_P_OUT)
             + _pdot(rk, v_bd, RWKV_P_OUT))
        s_ref[...] = (s_prev * jnp.exp(cum_last)
                      + _pdot(sa_bd, _stack_heads(bv_all[rows] * w_dec, masks), RWKV_P_STATE, "tn")
                      + _pdot(v_bd, _stack_heads(k2 * w_dec, masks), RWKV_P_STATE, "tn"))
        mean = _dot_rx(o, segmean, 2)
        oc = o - mean
        var = _dot_rx(oc * oc, segmean, 2)
        on = oc * lax.rsqrt(var + RWKV_GN_EPS) * lng_ref[...] + lnb_ref[...]
        bonus = _dot_rx(r * k2 * rk_ref[...], segsum, 2) * v
        o_ref[rows, :] = ((on + bonus) * g_all[rows]).astype(o_ref.dtype)


def _rwkv_mixer(u, bsz, seq, mu, w0, w_up, a0, a_up, g_up, k_k, k_a, r_k, ln_g, ln_b):
    tb = MIXER_TB
    nb = seq // tb
    W = GROUP_W
    wup = jnp.zeros((128, W), F32).at[0:32].set(w_up)
    aup = jnp.zeros((128, W), F32).at[32:64].set(a_up)
    gup = jnp.zeros((128, W), F32).at[64:128].set(g_up)
    args = (u, u, mu[None, :3 * W], mu[None, 3 * W:], w0[None, :], wup, a0[None, :], aup, gup,
            k_k[None, :], k_a[None, :], r_k.reshape(1, W), ln_g[None, :], ln_b[None, :])
    in_specs = [_u_spec(tb, 3 * W, COL_RWKV, nb), _u_spec(tb, 128, COL_LORA, nb)]
    in_specs += [_const_spec(a.shape) for a in args[2:]]
    return pl.pallas_call(
        functools.partial(_rwkv_body, tb=tb),
        grid=(bsz, nb),
        in_specs=in_specs,
        out_specs=pl.BlockSpec((tb, W), lambda b, j: (b * nb + j, 0)),
        out_shape=jax.ShapeDtypeStruct((bsz * seq, W), BF16),
        scratch_shapes=[pltpu.VMEM((tb + 8, 3 * W), F32), pltpu.VMEM((tb + 8, 128), F32),
                        pltpu.VMEM((W, W), F32)],
        compiler_params=_cparams(("arbitrary", "arbitrary")),
        name="rwkv_mixer",
    )(*args)


def _outproj_body(x_ref, y0_ref, y1_ref, y2_ref, y3_ref, w_ref, g_ref, b_ref, of_ref, ob_ref):
    W = GROUP_W
    mix = _dot(y0_ref[...], w_ref[0:W, :])
    mix = mix + _dot(y1_ref[...], w_ref[W:2 * W, :])
    mix = mix + _dot(y2_ref[...], w_ref[2 * W:3 * W, :])
    mix = mix + _dot(y3_ref[...], w_ref[3 * W:4 * W, :])
    out = _layer_norm(DEEPNORM_ALPHA * x_ref[...] + mix, g_ref[...], b_ref[...])
    of_ref[...] = out
    ob_ref[...] = out.astype(BF16)


def _out_proj_ln(x, ys, w_bf, g, b, tm=512):
    t, d = x.shape
    row = lambda i: (i, 0)
    const = lambda i: (0, 0)
    return pl.pallas_call(
        _outproj_body,
        grid=(t // tm,),
        in_specs=[pl.BlockSpec((tm, d), row)] + [pl.BlockSpec((tm, GROUP_W), row)] * 4
        + [pl.BlockSpec((d, d), const), pl.BlockSpec((1, d), const), pl.BlockSpec((1, d), const)],
        out_specs=[pl.BlockSpec((tm, d), row), pl.BlockSpec((tm, d), row)],
        out_shape=[jax.ShapeDtypeStruct((t, d), F32), jax.ShapeDtypeStruct((t, d), BF16)],
        compiler_params=_cparams(("arbitrary",)),
        name="out_proj_ln",
    )(x, *ys, w_bf, g[None, :], b[None, :])


def _router_body(x_ref, wt_ref, b_ref, idx_ref, gate_ref, rank_ref, cnt_ref, carry_ref, *, tr):
    i = pl.program_id(0)

    @pl.when(i == 0)
    def _():
        carry_ref[...] = jnp.zeros(carry_ref.shape, F32)

    logits = _pdot(wt_ref[...], x_ref[...], 3, "nt") + b_ref[...][:, 0:1]
    e_iota = _iota((N_EXPERTS, tr), 0)
    work = logits
    onehot = jnp.zeros((N_EXPERTS, tr), F32)
    sels, vals, idxs = [], [], []
    for _ in range(TOP_K):
        m = jnp.max(work, axis=0, keepdims=True)
        idx = jnp.min(jnp.where(work == m, e_iota, N_EXPERTS), axis=0, keepdims=True)
        sel = e_iota == idx
        work = jnp.where(sel, -jnp.inf, work)
        onehot = onehot + sel.astype(F32)
        sels.append(sel)
        vals.append(m)
        idxs.append(idx)
    exps = [jnp.exp(v - vals[0]) for v in vals]
    tot = exps[0] + exps[1] + exps[2] + exps[3]
    upper = (_iota((tr, tr), 0) < _iota((tr, tr), 1)).astype(BF16)
    carry = carry_ref[...][:, 0:1]
    before = _dot(_bf(onehot), upper) + carry
    ranks = [jnp.sum(jnp.where(s, before, 0.0), axis=0, keepdims=True) for s in sels]
    idx_ref[...] = jnp.concatenate(idxs, axis=0)
    gate_ref[...] = jnp.concatenate([e / tot for e in exps], axis=0)
    rank_ref[...] = jnp.concatenate(ranks, axis=0).astype(jnp.int32)
    new_carry = carry + jnp.sum(onehot, axis=1, keepdims=True)
    carry_ref[...] = jnp.broadcast_to(new_carry, carry_ref.shape)
    cnt_ref[...] = jnp.broadcast_to(new_carry, cnt_ref.shape)


def _router(x, w_t, b, tr=512):
    t, d = x.shape
    return pl.pallas_call(
        functools.partial(_router_body, tr=tr),
        grid=(t // tr,),
        in_specs=[pl.BlockSpec((tr, d), lambda i: (i, 0)),
                  pl.BlockSpec((N_EXPERTS, d), lambda i: (0, 0)),
                  pl.BlockSpec((N_EXPERTS, 128), lambda i: (0, 0))],
        out_specs=[pl.BlockSpec((TOP_K, tr), lambda i: (0, i)),
                   pl.BlockSpec((TOP_K, tr), lambda i: (0, i)),
                   pl.BlockSpec((TOP_K, tr), lambda i: (0, i)),
                   pl.BlockSpec((N_EXPERTS, 128), lambda i: (0, 0))],
        out_shape=[jax.ShapeDtypeStruct((TOP_K, t), jnp.int32),
                   jax.ShapeDtypeStruct((TOP_K, t), F32),
                   jax.ShapeDtypeStruct((TOP_K, t), jnp.int32),
                   jax.ShapeDtypeStruct((N_EXPERTS, 128), F32)],
        scratch_shapes=[pltpu.VMEM((N_EXPERTS, 128), F32)],
        compiler_params=_cparams(("arbitrary",)),
        name="router",
    )(x, w_t, jnp.broadcast_to(b[:, None], (N_EXPERTS, 128)))


def _route_plan_body(idx_ref, rank_ref, cnt_ref, dest_ref, be_ref, nv_ref, *, blk, nb_pad):
    E = N_EXPERTS
    cnt = cnt_ref[...][:, 0:1]
    padded = jnp.floor((cnt + (blk - 1)) / blk) * blk
    lower = (_iota((E, E), 1) < _iota((E, E), 0)).astype(F32)
    pstart = jnp.sum(lower * _col_to_row(padded), axis=1, keepdims=True)
    pend = pstart + padded
    idx = idx_ref[...]
    dest = rank_ref[...].astype(F32)
    for e in range(E):
        dest = dest + jnp.where(idx == e, pstart[e:e + 1, 0:1], 0.0)
    dest_ref[...] = dest.astype(jnp.int32)
    blk_start = (_iota((1, nb_pad), 1) * blk).astype(F32)
    be = jnp.sum((pend <= blk_start).astype(F32), axis=0, keepdims=True)
    be_ref[...] = jnp.minimum(be, E - 1).astype(jnp.int32)
    nv_ref[...] = jnp.broadcast_to(pend[E - 1:E, 0:1] / blk, nv_ref.shape).astype(jnp.int32)


def _route_plan(idx_t, rank_t, counts, blk, nb_pad):
    t = idx_t.shape[1]
    return pl.pallas_call(
        functools.partial(_route_plan_body, blk=blk, nb_pad=nb_pad),
        out_shape=[jax.ShapeDtypeStruct((TOP_K, t), jnp.int32),
                   jax.ShapeDtypeStruct((1, nb_pad), jnp.int32),
                   jax.ShapeDtypeStruct((1, 128), jnp.int32)],
        compiler_params=pltpu.CompilerParams(vmem_limit_bytes=VMEM_LIMIT),
        name="route_plan",
    )(idx_t, rank_t, counts)


def _dispatch_body(dest_ref, x_ref, xs_in, xs_hbm, sem, *, td):
    del xs_in

    def issue(t, carry):
        for k in range(TOP_K):
            pltpu.make_async_copy(x_ref.at[pl.ds(t, 1)],
                                  xs_hbm.at[pl.ds(dest_ref[k, t], 1)], sem).start()
        return carry

    lax.fori_loop(0, td, issue, 0, unroll=8)
    for _ in range(TOP_K):
        pltpu.make_async_copy(x_ref, xs_hbm.at[pl.ds(0, td)], sem).wait()


def _dispatch(dest_t, x, n_rows, td=512):
    t, d = x.shape
    xs0 = jnp.zeros((n_rows, d), x.dtype)
    return pl.pallas_call(
        functools.partial(_dispatch_body, td=td),
        grid=(t // td,),
        in_specs=[pl.BlockSpec((TOP_K, td), lambda i: (0, i), memory_space=pltpu.SMEM),
                  pl.BlockSpec((td, d), lambda i: (i, 0)),
                  pl.BlockSpec(memory_space=pl.ANY)],
        out_specs=pl.BlockSpec(memory_space=pl.ANY),
        out_shape=jax.ShapeDtypeStruct((n_rows, d), x.dtype),
        scratch_shapes=[pltpu.SemaphoreType.DMA(())],
        input_output_aliases={2: 0},
        compiler_params=_cparams(("arbitrary",)),
        name="dispatch",
    )(dest_t, x, xs0)


def _expert_body(be_ref, nv_ref, xs_ref, wgu_ref, bgu_ref, wd_ref, bd_ref, ys_ref, wgu_bf, wd_bf):
    b = pl.program_id(0)
    d = D_MODEL

    @pl.when(b < nv_ref[0])
    def _():
        prev = be_ref[jnp.maximum(b - 1, 0)]

        @pl.when((b == 0) | (be_ref[b] != prev))
        def _():
            wgu_bf[...] = wgu_ref[...].astype(BF16)
            wd_bf[...] = wd_ref[...].astype(BF16)

        h = _dot(_bf(xs_ref[...]), wgu_bf[...]) + bgu_ref[...]
        hg = jnp.minimum(h[:, 0:d], SWIGLU_LIMIT)
        hl = jnp.clip(h[:, d:2 * d], -SWIGLU_LIMIT, SWIGLU_LIMIT)
        act = (hl + 1.0) * (hg * _sigmoid(hg * SWIGLU_ALPHA))
        ys_ref[...] = _dot(_bf(act), wd_bf[...]) + bd_ref[...]

    @pl.when(b >= nv_ref[0])
    def _():
        ys_ref[...] = jnp.zeros(ys_ref.shape, F32)


def _experts(be, nv, xs, w_gu, b_gu, w_down, b_down, layer, blk):
    n_rows, d = xs.shape
    nb = n_rows // blk

    def row(b, be_r, nv_r):
        return (jnp.minimum(b, nv_r[0] - 1), 0)

    def wsel(b, be_r, nv_r):
        return (layer, be_r[jnp.minimum(b, nv_r[0] - 1)], 0, 0)

    grid_spec = pltpu.PrefetchScalarGridSpec(
        num_scalar_prefetch=2,
        grid=(nb,),
        in_specs=[pl.BlockSpec((blk, d), row),
                  pl.BlockSpec((None, None, d, 2 * d), wsel),
                  pl.BlockSpec((None, None, 1, 2 * d), wsel),
                  pl.BlockSpec((None, None, d, d), wsel),
                  pl.BlockSpec((None, None, 1, d), wsel)],
        out_specs=pl.BlockSpec((blk, d), lambda b, be_r, nv_r: (b, 0)),
        scratch_shapes=[pltpu.VMEM((d, 2 * d), BF16), pltpu.VMEM((d, d), BF16)],
    )
    return pl.pallas_call(
        _expert_body,
        grid_spec=grid_spec,
        out_shape=jax.ShapeDtypeStruct((n_rows, d), F32),
        compiler_params=_cparams(("arbitrary",)),
        name="experts",
    )(be, nv, xs, w_gu, b_gu, w_down, b_down)


def _combine_body(dcur_ref, dnext_ref, ys_hbm, x_ref, gate_ref, p_ref, wg_ref, wp_ref, g_ref, b_ref,
                  of_ref, ob_ref, ybuf, sem, *, tc):
    i = pl.program_id(0)
    n = pl.num_programs(0)
    slot = lax.rem(i, 2)

    def issue(d_ref, s):
        def body(t, carry):
            for k in range(TOP_K):
                pltpu.make_async_copy(ys_hbm.at[pl.ds(d_ref[k, t], 1)],
                                      ybuf.at[s, k, pl.ds(t, 1)], sem.at[s]).start()
            return carry
        lax.fori_loop(0, tc, body, 0, unroll=8)

    @pl.when(i == 0)
    def _():
        issue(dcur_ref, 0)

    @pl.when(i + 1 < n)
    def _():
        issue(dnext_ref, 1 - slot)

    for k in range(TOP_K):
        pltpu.make_async_copy(ys_hbm.at[pl.ds(0, tc)], ybuf.at[slot, k], sem.at[slot]).wait()

    gate = gate_ref[...]
    ffn = ybuf[slot, 0] * gate[:, 0:1]
    for k in range(1, TOP_K):
        ffn = ffn + ybuf[slot, k] * gate[:, k:k + 1]
    h = DEEPNORM_ALPHA * x_ref[...] + ffn
    ple = _dot(_bf(p_ref[...]), wp_ref[...])
    h = h + _sigmoid(_dot(_bf(h), wg_ref[...])) * ple
    out = _layer_norm(h, g_ref[...], b_ref[...])
    of_ref[...] = out
    ob_ref[...] = out.astype(BF16)


def _combine(dest_t, ys, x, gate, p, wg_bf, wp_bf, g, b, layer, tc=256):
    t, d = x.shape
    n = t // tc
    row = lambda i: (i, 0)
    const = lambda i: (0, 0)
    return pl.pallas_call(
        functools.partial(_combine_body, tc=tc),
        grid=(n,),
        in_specs=[pl.BlockSpec((TOP_K, tc), lambda i: (0, i), memory_space=pltpu.SMEM),
                  pl.BlockSpec((TOP_K, tc), lambda i: (0, jnp.minimum(i + 1, n - 1)),
                               memory_space=pltpu.SMEM),
                  pl.BlockSpec(memory_space=pl.ANY),
                  pl.BlockSpec((tc, d), row),
                  pl.BlockSpec((tc, TOP_K), row),
                  pl.BlockSpec((None, tc, PLE_DIM), lambda i: (layer, i, 0)),
                  pl.BlockSpec((d, d), const),
                  pl.BlockSpec((PLE_DIM, d), const),
                  pl.BlockSpec((1, d), const),
                  pl.BlockSpec((1, d), const)],
        out_specs=[pl.BlockSpec((tc, d), row), pl.BlockSpec((tc, d), row)],
        out_shape=[jax.ShapeDtypeStruct((t, d), F32), jax.ShapeDtypeStruct((t, d), BF16)],
        scratch_shapes=[pltpu.VMEM((2, TOP_K, tc, d), F32), pltpu.SemaphoreType.DMA((2,))],
        compiler_params=_cparams(("arbitrary",)),
        name="combine",
    )(dest_t, dest_t, ys, x, gate, p, wg_bf, wp_bf, g[None, :], b[None, :])


def _pack_w_in(w):
    o_rwkv = 772
    o_gla = o_rwkv + 896
    o_ml = o_gla + 784
    z = lambda n: jnp.zeros((w.shape[0], n), w.dtype)
    parts = [
        w[:, 0:768],
        w[:, o_rwkv:o_rwkv + 768],
        w[:, o_ml:o_ml + 512],
        w[:, o_ml + 512:o_ml + 768],
        w[:, o_ml + 776:o_ml + 1032],
        w[:, o_gla:o_gla + 512],
        w[:, o_gla + 528:o_gla + 784],
        w[:, o_rwkv + 768:o_rwkv + 896],
        w[:, 768:772],
        w[:, o_gla + 512:o_gla + 528],
        w[:, o_ml + 768:o_ml + 776],
        z(128 - 28),
    ]
    return jnp.concatenate(parts, axis=1).astype(BF16)


def kernel(x, p, w_in, w_out, ln1_g, ln1_b, ssd_conv_w, ssd_conv_b, ssd_a_log, ssd_dt_bias, ssd_d, ssd_norm_g, rwkv_mu, rwkv_w0, rwkv_w_up, rwkv_a0, rwkv_a_up, rwkv_g_up, rwkv_k_k, rwkv_k_a, rwkv_r_k, rwkv_ln_g, rwkv_ln_b, gla_gate_up, gla_gate_b, gla_norm_g, mlstm_conv_w, mlstm_conv_b, mlstm_i_b, mlstm_f_b, mlstm_norm_g, router_w, router_b, exp_w_gu, exp_b_gu, exp_w_down, exp_b_down, ple_gate_w, ple_proj, ln2_g, ln2_b):
    bsz, seq, d = x.shape
    t = bsz * seq
    depth = w_in.shape[0]
    blk = EXPERT_BLK
    n_blocks = -(-(t * TOP_K + N_EXPERTS * (blk - 1)) // blk)
    n_rows = n_blocks * blk
    nb_pad = -(-n_blocks // 128) * 128

    xf = x.reshape(t, d)
    xb = xf.astype(BF16)
    p2 = p.reshape(depth, t, PLE_DIM)
    b_gu4 = exp_b_gu[:, :, None, :]
    b_down4 = exp_b_down[:, :, None, :]
    for i in range(depth):
        u = _in_proj(xb, _pack_w_in(w_in[i]))
        y_ssd = _ssd_mixer(u, bsz, seq, ssd_conv_w[i], ssd_conv_b[i], ssd_a_log[i], ssd_dt_bias[i],
                           ssd_d[i], ssd_norm_g[i])
        y_rwkv = _rwkv_mixer(u, bsz, seq, rwkv_mu[i], rwkv_w0[i], rwkv_w_up[i], rwkv_a0[i],
                             rwkv_a_up[i], rwkv_g_up[i], rwkv_k_k[i], rwkv_k_a[i], rwkv_r_k[i],
                             rwkv_ln_g[i], rwkv_ln_b[i])
        y_gla = _gla_mixer(u, bsz, seq, gla_gate_up[i], gla_gate_b[i], gla_norm_g[i])
        y_ml = _mlstm_mixer(u, bsz, seq, mlstm_conv_w[i], mlstm_conv_b[i], mlstm_i_b[i],
                            mlstm_f_b[i], mlstm_norm_g[i])
        x1f, _ = _out_proj_ln(xf, (y_ssd, y_rwkv, y_gla, y_ml), w_out[i].astype(BF16),
                              ln1_g[i], ln1_b[i])
        idx_t, gate_t, rank_t, counts = _router(x1f, router_w[i].T, router_b[i])
        dest_t, be, nv = _route_plan(idx_t, rank_t, counts, blk, nb_pad)
        xs = _dispatch(dest_t, x1f, n_rows)
        ys = _experts(be.reshape(nb_pad), nv.reshape(128), xs, exp_w_gu, b_gu4, exp_w_down,
                      b_down4, i, blk)
        xf, xb = _combine(dest_t, ys, x1f, gate_t.T, p2, ple_gate_w[i].astype(BF16),
                          ple_proj[i].astype(BF16), ln2_g[i], ln2_b[i], i)
    return xf.reshape(bsz, seq, d)
```

```python
import functools

import jax
import jax.numpy as jnp
import numpy as np
from jax import lax
from jax.experimental import pallas as pl
from jax.experimental.pallas import tpu as pltpu

F32 = jnp.float32
BF16 = jnp.bfloat16
HI = lax.Precision.HIGHEST

D_MODEL = 1024
DEPTH = 4
GROUP_W = 256
N_HEADS = 4
HEAD_DIM = 64
SSD_STATE = 128
SSD_CHUNK = 128
GLA_DK = 128
GLA_HEAD_K = 32
GLA_TAU = 16.0
CHUNK = 64
RWKV_P_SCORE = 1
RWKV_P_INV = 1
RWKV_P_RHS = 1
RWKV_P_OUT = 1
RWKV_P_STATE = 1
RWKV_GN_EPS = 64e-5
NORM_EPS = 1e-5
LN_EPS = 1e-5
N_EXPERTS = 32
TOP_K = 4
SWIGLU_LIMIT = 7.0
SWIGLU_ALPHA = 1.702
PLE_DIM = 256
DEEPNORM_ALPHA = (2 * DEPTH) ** 0.25

U_WIDTH = 3584
COL_SSD_Z, COL_SSD_X, COL_SSD_BC = 0, 256, 512
COL_RWKV = 768
COL_ML_QK = 1536
COL_ML_VO = 2048
COL_GLA_QK, COL_GLA_V, COL_GLA_OG = 2560, 2816, 3072
COL_LORA = 3328
COL_MISC = 3456
MISC_DT, MISC_GLA, MISC_I, MISC_F = 0, 4, 20, 24

MIXER_TB = 256
EXPERT_BLK = 512
VMEM_LIMIT = 56 * 1024 * 1024


def _cparams(sem):
    return pltpu.CompilerParams(dimension_semantics=sem, vmem_limit_bytes=VMEM_LIMIT)


def _dot(a, b, prec=None):
    return lax.dot_general(a, b, (((1,), (0,)), ((), ())), precision=prec,
                           preferred_element_type=F32)


def _dot_nt(a, b, prec=None):
    return lax.dot_general(a, b, (((1,), (1,)), ((), ())), precision=prec,
                           preferred_element_type=F32)


def _dot_tn(a, b, prec=None):
    return lax.dot_general(a, b, (((0,), (0,)), ((), ())), precision=prec,
                           preferred_element_type=F32)


def _bf(x):
    return x.astype(BF16)


_DIMS = {"nn": (((1,), (0,)), ((), ())), "nt": (((1,), (1,)), ((), ())),
         "tn": (((0,), (0,)), ((), ()))}


def _split_bf(x):
    hi = x.astype(BF16)
    return hi, (x - hi.astype(F32)).astype(BF16)


def _pdot(a, b, passes, kind="nn"):
    dn = _DIMS[kind]
    if passes == 6:
        return lax.dot_general(a, b, dn, precision=HI, preferred_element_type=F32)
    dg = lambda x, y: lax.dot_general(x, y, dn, preferred_element_type=F32)
    if passes == 1:
        return dg(_bf(a), _bf(b))
    ah, al = _split_bf(a)
    bh, bl = _split_bf(b)
    return dg(ah, bh) + (dg(ah, bl) + dg(al, bh))


def _pieces(x, n):
    out, rem = [], x
    for _ in range(n):
        part = rem.astype(BF16)
        out.append(part)
        rem = rem - part.astype(F32)
    return out


def _dot_lx(a_exact, b, n):
    ab = _bf(a_exact)
    acc = None
    for part in _pieces(b, n):
        term = lax.dot_general(ab, part, _DIMS["nn"], preferred_element_type=F32)
        acc = term if acc is None else acc + term
    return acc


def _dot_rx(a, b_exact, n):
    bb = _bf(b_exact)
    acc = None
    for part in _pieces(a, n):
        term = lax.dot_general(part, bb, _DIMS["nn"], preferred_element_type=F32)
        acc = term if acc is None else acc + term
    return acc


def _sigmoid(x):
    return 1.0 / (1.0 + jnp.exp(-x))


def _silu(x):
    return x * _sigmoid(x)


def _softplus(x):
    return jnp.maximum(x, 0.0) + jnp.log1p(jnp.exp(-jnp.abs(x)))


def _log_sigmoid(x):
    return jnp.minimum(x, 0.0) - jnp.log1p(jnp.exp(-jnp.abs(x)))


def _iota(shape, dim):
    return lax.broadcasted_iota(jnp.int32, shape, dim)


def _tri_incl(n):
    return (_iota((n, n), 0) >= _iota((n, n), 1)).astype(F32)


def _seg_matrix(n, seg, value):
    sh = int(np.log2(seg))
    same = (lax.shift_right_logical(_iota((n, n), 0), sh)
            == lax.shift_right_logical(_iota((n, n), 1), sh))
    return jnp.where(same, value, 0.0).astype(F32)


def _head_mask(width, seg, h):
    lane = _iota((1, width), 1)
    return ((lane >= h * seg) & (lane < (h + 1) * seg)).astype(F32)


def _expand(cols, seg):
    n = len(cols)
    rows = cols[0].shape[0]
    width = n * seg
    lane = _iota((rows, width), 1)
    out = jnp.broadcast_to(cols[n - 1], (rows, width))
    for h in range(n - 2, -1, -1):
        out = jnp.where(lane < (h + 1) * seg, jnp.broadcast_to(cols[h], (rows, width)), out)
    return out


def _col_to_row(col):
    n = col.shape[0]
    eye = _iota((n, n), 0) == _iota((n, n), 1)
    return jnp.sum(jnp.where(eye, col, 0.0), axis=0, keepdims=True)


def _layer_norm(x, g, b):
    mu = jnp.mean(x, axis=-1, keepdims=True)
    xc = x - mu
    var = jnp.mean(xc * xc, axis=-1, keepdims=True)
    return xc * lax.rsqrt(var + LN_EPS) * g + b


def _causal_conv_silu(buf, w_ref, b_ref, rows):
    acc = b_ref[...]
    for k in range(4):
        acc = acc + buf[pl.ds(5 + k, rows), :] * w_ref[k:k + 1, :]
    return _silu(acc)


_W_IN_COLS = 3484
_W_RWKV, _W_GLA, _W_ML = 772, 772 + 896, 772 + 896 + 784
_PACK_SEGMENTS = (
    (COL_SSD_Z, 0, 768),
    (COL_RWKV, _W_RWKV, 768),
    (COL_ML_QK, _W_ML, 768),
    (COL_ML_VO + 256, _W_ML + 776, 256),
    (COL_GLA_QK, _W_GLA, 512),
    (COL_GLA_OG, _W_GLA + 528, 256),
    (COL_LORA, _W_RWKV + 768, 128),
)
_PACK_MISC = ((768, MISC_DT, 4), (_W_GLA + 512, MISC_GLA, 16), (_W_ML + 768, MISC_I, 8))


def _pack_body(w_ref, o_ref):
    def cols(src, n):
        sh = src % 128
        a0 = src - sh
        if sh == 0:
            return w_ref[:, a0:a0 + n]
        wd = -(-(sh + n) // 128) * 128
        return pltpu.roll(w_ref[:, a0:a0 + wd], wd - sh, axis=1)[:, 0:n]

    for dst, src, n in _PACK_SEGMENTS:
        o_ref[:, dst:dst + n] = cols(src, n).astype(BF16)
    lane = _iota((w_ref.shape[0], 128), 1)
    misc = jnp.zeros((w_ref.shape[0], 128), F32)
    for src, lane0, n in _PACK_MISC:
        assert src % 128 == lane0
        a0 = src - lane0
        misc = jnp.where((lane >= lane0) & (lane < lane0 + n), w_ref[:, a0:a0 + 128], misc)
    o_ref[:, COL_MISC:COL_MISC + 128] = misc.astype(BF16)


def _pack_w_in(w_in, rb=256):
    depth, d, n = w_in.shape
    assert n == _W_IN_COLS
    return pl.pallas_call(
        _pack_body,
        grid=(depth, d // rb),
        in_specs=[pl.BlockSpec((None, rb, U_WIDTH), lambda l, r: (l, r, 0))],
        out_specs=pl.BlockSpec((None, rb, U_WIDTH), lambda l, r: (l, r, 0)),
        out_shape=jax.ShapeDtypeStruct((depth, d, U_WIDTH), BF16),
        compiler_params=_cparams(("arbitrary", "arbitrary")),
        name="pack_w_in",
    )(w_in)


def _in_proj_body(x_ref, w_ref, o_ref):
    o_ref[...] = jnp.dot(x_ref[...], w_ref[...], preferred_element_type=F32)


def _in_proj(x_bf, w_all, layer, tm=512, tn=1792):
    t, k = x_bf.shape
    n = w_all.shape[2]
    return pl.pallas_call(
        _in_proj_body,
        grid=(n // tn, t // tm),
        in_specs=[pl.BlockSpec((tm, k), lambda j, i: (i, 0)),
                  pl.BlockSpec((None, k, tn), lambda j, i: (layer, 0, j))],
        out_specs=pl.BlockSpec((tm, tn), lambda j, i: (i, j)),
        out_shape=jax.ShapeDtypeStruct((t, n), F32),
        compiler_params=_cparams(("arbitrary", "arbitrary")),
        name="in_proj",
    )(x_bf, w_all)


def _ssd_body(z_ref, x_ref, bc_ref, misc_ref, cwx_ref, cwbc_ref, cbx_ref, cbbc_ref,
              alog_ref, dtb_ref, dskip_ref, ng_ref, o_ref, xbuf, bcbuf, st_ref, *, tb):
    j = pl.program_id(1)

    @pl.when(j == 0)
    def _():
        xbuf[0:8, :] = jnp.zeros((8, GROUP_W), F32)
        bcbuf[0:8, :] = jnp.zeros((8, GROUP_W), F32)
        st_ref[...] = jnp.zeros(st_ref.shape, F32)

    xbuf[8:8 + tb, :] = x_ref[...]
    bcbuf[8:8 + tb, :] = bc_ref[...]
    xs_all = _causal_conv_silu(xbuf, cwx_ref, cbx_ref, tb)
    bc_all = _causal_conv_silu(bcbuf, cwbc_ref, cbbc_ref, tb)
    xbuf[0:8, :] = xbuf[tb:tb + 8, :]
    bcbuf[0:8, :] = bcbuf[tb:tb + 8, :]

    dt_all = _softplus(misc_ref[...] + dtb_ref[...])
    adt_all = dt_all * (-jnp.exp(alog_ref[...]))
    z_all = z_ref[...]
    L = SSD_CHUNK
    tri = _tri_incl(L)
    causal = _iota((L, L), 0) >= _iota((L, L), 1)
    masks = [_head_mask(GROUP_W, HEAD_DIM, h) for h in range(N_HEADS)]

    for c in range(tb // L):
        rows = slice(c * L, (c + 1) * L)
        xs = xs_all[rows]
        bm = _bf(bc_all[rows, 0:SSD_STATE])
        cm = _bf(bc_all[rows, SSD_STATE:2 * SSD_STATE])
        dt = dt_all[rows]
        acum = _dot_lx(tri, adt_all[rows], 3)
        cols = [acum[:, h:h + 1] for h in range(N_HEADS)]
        lasts = [acum[L - 1:L, h:h + 1] for h in range(N_HEADS)]
        xc = xs * _expand([dt[:, h:h + 1] for h in range(N_HEADS)], HEAD_DIM)
        g = _dot_nt(cm, bm)
        y = jnp.zeros((L, GROUP_W), F32)
        for h in range(N_HEADS):
            lmat = jnp.exp(jnp.where(causal, cols[h] - _col_to_row(cols[h]), -jnp.inf))
            y = y + _dot(_bf(g * lmat), _bf(xc * masks[h]))
        dec_states = _expand([jnp.exp(lasts[h] - cols[h]) for h in range(N_HEADS)], HEAD_DIM)
        st_prev = st_ref[...]
        y = y + _dot(cm, _bf(st_prev)) * _expand([jnp.exp(cols[h]) for h in range(N_HEADS)], HEAD_DIM)
        st_new = _dot_tn(bm, _bf(xc * dec_states))
        st_ref[...] = st_prev * _expand([jnp.exp(lasts[h]) for h in range(N_HEADS)], HEAD_DIM) + st_new
        y = y + xs * dskip_ref[...]
        y = y * _silu(z_all[rows])
        ms = jnp.mean(y * y, axis=-1, keepdims=True)
        o_ref[rows, :] = (y * lax.rsqrt(ms + NORM_EPS) * ng_ref[...]).astype(o_ref.dtype)


def _u_spec(tb, width, col, nb):
    return pl.BlockSpec((tb, width), lambda b, j: (b * nb + j, col // width))


def _const_spec(shape):
    return pl.BlockSpec(shape, lambda b, j: tuple(0 for _ in shape))


def _ssd_mixer(u, bsz, seq, conv_w, conv_b, a_log, dt_bias, d_skip, norm_g):
    tb = MIXER_TB
    nb = seq // tb
    pad4 = lambda v: jnp.zeros((1, 128), F32).at[0, :N_HEADS].set(v)
    args = (u, u, u, u,
            conv_w[:, :GROUP_W], conv_w[:, GROUP_W:],
            conv_b[None, :GROUP_W], conv_b[None, GROUP_W:],
            pad4(a_log), pad4(dt_bias),
            jnp.repeat(d_skip, HEAD_DIM)[None, :], norm_g[None, :])
    in_specs = [_u_spec(tb, GROUP_W, COL_SSD_Z, nb), _u_spec(tb, GROUP_W, COL_SSD_X, nb),
                _u_spec(tb, GROUP_W, COL_SSD_BC, nb), _u_spec(tb, 128, COL_MISC, nb)]
    in_specs += [_const_spec(a.shape) for a in args[4:]]
    return pl.pallas_call(
        functools.partial(_ssd_body, tb=tb),
        grid=(bsz, nb),
        in_specs=in_specs,
        out_specs=pl.BlockSpec((tb, GROUP_W), lambda b, j: (b * nb + j, 0)),
        out_shape=jax.ShapeDtypeStruct((bsz * seq, GROUP_W), BF16),
        scratch_shapes=[pltpu.VMEM((tb + 8, GROUP_W), F32), pltpu.VMEM((tb + 8, GROUP_W), F32),
                        pltpu.VMEM((SSD_STATE, GROUP_W), F32)],
        compiler_params=_cparams(("arbitrary", "arbitrary")),
        name="ssd_mixer",
    )(*args)


def _gla_body(qk_ref, v_ref, og_ref, misc_ref, gup_ref, gb_ref, ng_ref, o_ref, st_ref, obuf,
              *, tb):
    j = pl.program_id(1)

    @pl.when(j == 0)
    def _():
        st_ref[...] = jnp.zeros(st_ref.shape, F32)

    L = CHUNK
    q_all = qk_ref[:, 0:GLA_DK] * (GLA_HEAD_K ** -0.5)
    k_all = qk_ref[:, GLA_DK:2 * GLA_DK]
    v_all = v_ref[...]
    og_all = og_ref[...]
    gpre = _dot(_bf(misc_ref[...]), gup_ref[...]) + gb_ref[...]
    loga_all = _log_sigmoid(gpre) / GLA_TAU
    tri = _tri_incl(L)
    causal = _iota((L, L), 0) >= _iota((L, L), 1)
    kmasks = [_head_mask(GLA_DK, GLA_HEAD_K, h) for h in range(N_HEADS)]
    vmasks = [_head_mask(GROUP_W, HEAD_DIM, h) for h in range(N_HEADS)]
    bd = (lax.shift_right_logical(_iota((GROUP_W, GLA_DK), 0), 6)
          == lax.shift_right_logical(_iota((GROUP_W, GLA_DK), 1), 5))
    segmean = _seg_matrix(GROUP_W, HEAD_DIM, 1.0 / HEAD_DIM)

    for c in range(tb // L):
        rows = slice(c * L, (c + 1) * L)
        bcum = _dot_lx(tri, loga_all[rows], 3)
        b_last = bcum[L - 1:L, :]
        q_dec = q_all[rows] * jnp.exp(bcum)
        k_inv = _bf(k_all[rows] * jnp.exp(-bcum))
        k_dec = _bf(k_all[rows] * jnp.exp(b_last - bcum))
        v = v_all[rows]
        st_prev = st_ref[...]
        o = _dot_nt(_bf(q_dec), _bf(st_prev))
        for h in range(N_HEADS):
            attn = jnp.where(causal, _dot_nt(_bf(q_dec * kmasks[h]), k_inv), 0.0)
            o = o + _dot(_bf(attn), _bf(v * vmasks[h]))
        contrib = jnp.where(bd, _dot_tn(_bf(v), k_dec), 0.0)
        st_ref[...] = st_prev * jnp.exp(b_last) + contrib
        obuf[rows, :] = o

    o = obuf[...]
    ms = _dot_rx(o * o, segmean, 2)
    out = o * lax.rsqrt(ms + NORM_EPS) * ng_ref[...] * _silu(og_all)
    o_ref[...] = out.astype(o_ref.dtype)


def _gla_mixer(u, bsz, seq, gate_up, gate_b, norm_g):
    tb = MIXER_TB
    nb = seq // tb
    gup = jnp.zeros((128, GLA_DK), F32).at[MISC_GLA:MISC_GLA + 16].set(gate_up).astype(BF16)
    args = (u, u, u, u, gup, gate_b[None, :], norm_g[None, :])
    in_specs = [_u_spec(tb, GROUP_W, COL_GLA_QK, nb), _u_spec(tb, GROUP_W, COL_GLA_V, nb),
                _u_spec(tb, GROUP_W, COL_GLA_OG, nb), _u_spec(tb, 128, COL_MISC, nb)]
    in_specs += [_const_spec(a.shape) for a in args[4:]]
    return pl.pallas_call(
        functools.partial(_gla_body, tb=tb),
        grid=(bsz, nb),
        in_specs=in_specs,
        out_specs=pl.BlockSpec((tb, GROUP_W), lambda b, j: (b * nb + j, 0)),
        out_shape=jax.ShapeDtypeStruct((bsz * seq, GROUP_W), BF16),
        scratch_shapes=[pltpu.VMEM((GROUP_W, GLA_DK), F32), pltpu.VMEM((tb, GROUP_W), F32)],
        compiler_params=_cparams(("arbitrary", "arbitrary")),
        name="gla_mixer",
    )(*args)


def _mlstm_body(qk_ref, vo_ref, misc_ref, cw_ref, cb_ref, ib_ref, fb_ref, ng_ref, o_ref,
                qkbuf, c_ref, n_ref, m_ref, fin, *, tb):
    j = pl.program_id(1)
    W = GROUP_W

    @pl.when(j == 0)
    def _():
        qkbuf[0:8, :] = jnp.zeros((8, 2 * W), F32)
        c_ref[...] = jnp.zeros(c_ref.shape, F32)
        n_ref[...] = jnp.zeros(n_ref.shape, F32)
        m_ref[...] = jnp.zeros(m_ref.shape, F32)

    qkbuf[8:8 + tb, :] = qk_ref[...]
    qk = _causal_conv_silu(qkbuf, cw_ref, cb_ref, tb)
    qkbuf[0:8, :] = qkbuf[tb:tb + 8, :]
    q_all = qk[:, 0:W] * (HEAD_DIM ** -0.5)
    k_all = qk[:, W:2 * W]
    v_all = vo_ref[:, 0:W]
    og_all = vo_ref[:, W:2 * W]
    misc = misc_ref[...]
    ipre_all = misc + ib_ref[...]
    lf_all = _log_sigmoid(misc + fb_ref[...])

    L = CHUNK
    tri = _tri_incl(L)
    causal = _iota((L, L), 0) >= _iota((L, L), 1)
    masks = [_head_mask(W, HEAD_DIM, h) for h in range(N_HEADS)]
    bd = _seg_matrix(W, HEAD_DIM, 1.0)
    segmean = _seg_matrix(W, HEAD_DIM, 1.0 / HEAD_DIM)

    for c in range(tb // L):
        rows = slice(c * L, (c + 1) * L)
        q, k, v = q_all[rows], k_all[rows], v_all[rows]
        kb = _bf(k)
        bcs = _dot_lx(tri, lf_all[rows], 3)
        ipre = ipre_all[rows]
        m_state = m_ref[...]
        num = jnp.zeros((L, W), F32)
        w_inter, rowsum, eneg, w_st, s_old, s_new, m_new = [], [], [], [], [], [], []
        for h in range(N_HEADS):
            b_col = bcs[:, MISC_F + h:MISC_F + h + 1]
            i_col = ipre[:, MISC_I + h:MISC_I + h + 1]
            b_last = bcs[L - 1:L, MISC_F + h:MISC_F + h + 1]
            m_prev = m_state[:, h:h + 1]
            dmat = jnp.where(causal, b_col - _col_to_row(b_col) + _col_to_row(i_col), -jnp.inf)
            a_st = b_last - b_col + i_col
            m_loc = jnp.max(a_st, axis=0, keepdims=True)
            w_st.append(jnp.exp(a_st - m_loc))
            mn = jnp.maximum(b_last + m_prev, m_loc)
            m_new.append(mn)
            s_old.append(jnp.exp(b_last + m_prev - mn))
            s_new.append(jnp.exp(m_loc - mn))
            m_inter = b_col + m_prev
            m_row = jnp.maximum(m_inter, jnp.max(dmat, axis=-1, keepdims=True))
            w_inter.append(jnp.exp(m_inter - m_row))
            eneg.append(jnp.exp(-m_row))
            scores = _dot_nt(_bf(q * masks[h]), kb) * jnp.exp(dmat - m_row)
            rowsum.append(jnp.sum(scores, axis=-1, keepdims=True))
            num = num + _dot(_bf(scores), _bf(v * masks[h]))
        c_prev = c_ref[...]
        n_prev = n_ref[...]
        wi = _expand(w_inter, HEAD_DIM)
        fin[0, rows, :] = num + wi * _dot_nt(_bf(q), _bf(c_prev))
        fin[1, rows, :] = wi
        fin[2, rows, :] = q * n_prev
        fin[3, rows, :] = _expand(rowsum, HEAD_DIM)
        fin[4, rows, :] = _expand(eneg, HEAD_DIM)
        so = _expand(s_old, HEAD_DIM)
        sn = _expand(s_new, HEAD_DIM)
        kw = k * _expand(w_st, HEAD_DIM)
        vw = v * _expand(w_st, HEAD_DIM)
        c_ref[...] = so * c_prev + sn * (bd * _dot_tn(_bf(vw), kb))
        n_ref[...] = so * n_prev + sn * jnp.sum(kw, axis=0, keepdims=True)
        lane = _iota((1, 128), 1)
        m_vec = jnp.zeros((1, 128), F32)
        for h in range(N_HEADS):
            m_vec = jnp.where(lane == h, m_new[h], m_vec)
        m_ref[...] = m_vec

    den = fin[1] * _dot_rx(fin[2], bd, 3) + fin[3]
    hval = fin[0] / jnp.maximum(jnp.abs(den), fin[4])
    hv = hval * _sigmoid(og_all)
    mean = _dot_rx(hv, segmean, 2)
    xc = hv - mean
    var = _dot_rx(xc * xc, segmean, 2)
    o_ref[...] = (xc * lax.rsqrt(var + NORM_EPS) * ng_ref[...]).astype(o_ref.dtype)


def _mlstm_mixer(u, bsz, seq, conv_w, conv_b, i_b, f_b, norm_g):
    tb = MIXER_TB
    nb = seq // tb
    ib = jnp.zeros((1, 128), F32).at[0, MISC_I:MISC_I + N_HEADS].set(i_b)
    fb = jnp.zeros((1, 128), F32).at[0, MISC_F:MISC_F + N_HEADS].set(f_b)
    args = (u, u, u, conv_w, conv_b[None, :], ib, fb, norm_g[None, :])
    in_specs = [_u_spec(tb, 2 * GROUP_W, COL_ML_QK, nb), _u_spec(tb, 2 * GROUP_W, COL_ML_VO, nb),
                _u_spec(tb, 128, COL_MISC, nb)]
    in_specs += [_const_spec(a.shape) for a in args[3:]]
    return pl.pallas_call(
        functools.partial(_mlstm_body, tb=tb),
        grid=(bsz, nb),
        in_specs=in_specs,
        out_specs=pl.BlockSpec((tb, GROUP_W), lambda b, j: (b * nb + j, 0)),
        out_shape=jax.ShapeDtypeStruct((bsz * seq, GROUP_W), BF16),
        scratch_shapes=[pltpu.VMEM((tb + 8, 2 * GROUP_W), F32),
                        pltpu.VMEM((GROUP_W, GROUP_W), F32),
                        pltpu.VMEM((1, GROUP_W), F32),
                        pltpu.VMEM((1, 128), F32),
                        pltpu.VMEM((5, tb, GROUP_W), F32)],
        compiler_params=_cparams(("arbitrary", "arbitrary")),
        name="mlstm_mixer",
    )(*args)


def _stack_heads(x, masks):
    return jnp.concatenate([x * m for m in masks], axis=0)


def _rwkv_body(rkv_ref, lora_ref, mu_rkv_ref, mu_lora_ref, w0_ref, wup_ref, a0_ref, aup_ref,
               gup_ref, kk_ref, ka_ref, rk_ref, lng_ref, lnb_ref, o_ref,
               rbuf, lbuf, s_ref, obuf, *, tb):
    j = pl.program_id(1)
    W = GROUP_W

    @pl.when(j == 0)
    def _():
        rbuf[0:8, :] = jnp.zeros((8, 3 * W), F32)
        lbuf[0:8, :] = jnp.zeros((8, 128), F32)
        s_ref[...] = jnp.zeros(s_ref.shape, F32)

    rbuf[8:8 + tb, :] = rkv_ref[...]
    lbuf[8:8 + tb, :] = lora_ref[...]
    rkv = rkv_ref[...]
    lora = lora_ref[...]
    rkv = rkv + (rbuf[pl.ds(7, tb), :] - rkv) * mu_rkv_ref[...]
    lora = lora + (lbuf[pl.ds(7, tb), :] - lora) * mu_lora_ref[...]
    rbuf[0:8, :] = rbuf[tb:tb + 8, :]
    lbuf[0:8, :] = lbuf[tb:tb + 8, :]

    r_all, k_all, v_all = rkv[:, 0:W], rkv[:, W:2 * W], rkv[:, 2 * W:3 * W]
    wpre = w0_ref[...] + _pdot(jnp.tanh(lora), wup_ref[...], 3)
    lw_all = -jnp.exp(-_softplus(-wpre) - 0.5)
    a_all = _sigmoid(a0_ref[...] + _pdot(lora, aup_ref[...], 3))
    g_all = _pdot(_sigmoid(lora), gup_ref[...], 3)
    segsum = _seg_matrix(W, HEAD_DIM, 1.0)
    segmean = _seg_matrix(W, HEAD_DIM, 1.0 / HEAD_DIM)
    kk = k_all * kk_ref[...]
    kk = kk / jnp.maximum(jnp.sqrt(_dot_rx(kk * kk, segsum, 2)), 1e-12)
    k2_all = k_all * (1.0 + (a_all - 1.0) * ka_ref[...])
    av_all = -kk
    bv_all = kk * a_all

    L = CHUNK
    HL = N_HEADS * L
    tri = _tri_incl(L)
    masks = [_head_mask(W, HEAD_DIM, h) for h in range(N_HEADS)]
    t_idx = _iota((L, HL), 0)
    s_idx = jnp.bitwise_and(_iota((L, HL), 1), L - 1)
    strict = s_idx < t_idx
    incl = s_idx <= t_idx
    eye = (_iota((HL, HL), 0) == _iota((HL, HL), 1)).astype(F32)

    for c in range(tb // L):
        rows = slice(c * L, (c + 1) * L)
        lw = lw_all[rows]
        cum = _dot_lx(tri, lw, 3)
        cum_last = cum[L - 1:L, :]
        w_inv = jnp.exp(-cum)
        w_dec = jnp.exp(cum_last - cum)
        r, k2, v = r_all[rows], k2_all[rows], v_all[rows]
        rt = r * jnp.exp(cum)
        at = av_all[rows] * jnp.exp(cum - lw)
        kt = k2 * w_inv
        bt = bv_all[rows] * w_inv
        bt_bd = _stack_heads(bt, masks)
        kt_bd = _stack_heads(kt, masks)
        v_bd = _stack_heads(v, masks)
        sc = _pdot(jnp.concatenate([at, rt], axis=0), jnp.concatenate([bt_bd, kt_bd], axis=0),
                   RWKV_P_SCORE, "nt")
        ab = jnp.where(strict, sc[0:L, 0:HL], 0.0)
        ak = jnp.where(strict, sc[0:L, HL:2 * HL], 0.0)
        rb = jnp.where(incl, sc[L:2 * L, 0:HL], 0.0)
        rk = jnp.where(incl, sc[L:2 * L, HL:2 * HL], 0.0)
        p = _stack_heads(ab, masks)
        minv = eye + p
        for _ in range(5):
            p = _pdot(p, p, RWKV_P_INV)
            minv = minv + _pdot(minv, p, RWKV_P_INV)
        s_prev = s_ref[...]
        on_s = _pdot(jnp.concatenate([_stack_heads(at, masks), rt], axis=0), s_prev,
                     RWKV_P_RHS, "nt")
        on_v = _pdot(jnp.concatenate([_stack_heads(ak, masks), rk], axis=0), v_bd, RWKV_P_RHS)
        sa_bd = _pdot(minv, on_s[0:HL] + on_v[0:HL], RWKV_P_RHS)
        obuf[rows, :] = on_s[HL:HL + L] + on_v[HL:HL + L] + _pdot(rb, sa_bd, RWKV_P_OUT)
        dec_bd = jnp.concatenate([_stack_heads(bv_all[rows] * w_dec, masks),
                                  _stack_heads(k2 * w_dec, masks)], axis=0)
        s_ref[...] = s_prev * jnp.exp(cum_last) + _pdot(
            jnp.concatenate([sa_bd, v_bd], axis=0), dec_bd, RWKV_P_STATE, "tn")

    o = obuf[...]
    mean = _dot_rx(o, segmean, 2)
    oc = o - mean
    var = _dot_rx(oc * oc, segmean, 2)
    on = oc * lax.rsqrt(var + RWKV_GN_EPS) * lng_ref[...] + lnb_ref[...]
    bonus = _dot_rx(r_all * k2_all * rk_ref[...], segsum, 2) * v_all
    o_ref[...] = ((on + bonus) * g_all).astype(o_ref.dtype)


def _rwkv_mixer(u, bsz, seq, mu, w0, w_up, a0, a_up, g_up, k_k, k_a, r_k, ln_g, ln_b):
    tb = MIXER_TB
    nb = seq // tb
    W = GROUP_W
    wup = jnp.zeros((128, W), F32).at[0:32].set(w_up)
    aup = jnp.zeros((128, W), F32).at[32:64].set(a_up)
    gup = jnp.zeros((128, W), F32).at[64:128].set(g_up)
    args = (u, u, mu[None, :3 * W], mu[None, 3 * W:], w0[None, :], wup, a0[None, :], aup, gup,
            k_k[None, :], k_a[None, :], r_k.reshape(1, W), ln_g[None, :], ln_b[None, :])
    in_specs = [_u_spec(tb, 3 * W, COL_RWKV, nb), _u_spec(tb, 128, COL_LORA, nb)]
    in_specs += [_const_spec(a.shape) for a in args[2:]]
    return pl.pallas_call(
        functools.partial(_rwkv_body, tb=tb),
        grid=(bsz, nb),
        in_specs=in_specs,
        out_specs=pl.BlockSpec((tb, W), lambda b, j: (b * nb + j, 0)),
        out_shape=jax.ShapeDtypeStruct((bsz * seq, W), BF16),
        scratch_shapes=[pltpu.VMEM((tb + 8, 3 * W), F32), pltpu.VMEM((tb + 8, 128), F32),
                        pltpu.VMEM((W, W), F32), pltpu.VMEM((tb, W), F32)],
        compiler_params=_cparams(("arbitrary", "arbitrary")),
        name="rwkv_mixer",
    )(*args)


def _outproj_body(x_ref, y0_ref, y1_ref, y2_ref, y3_ref, w_ref, g_ref, b_ref, of_ref, ob_ref):
    W = GROUP_W
    mix = _dot(y0_ref[...], w_ref[0:W, :])
    mix = mix + _dot(y1_ref[...], w_ref[W:2 * W, :])
    mix = mix + _dot(y2_ref[...], w_ref[2 * W:3 * W, :])
    mix = mix + _dot(y3_ref[...], w_ref[3 * W:4 * W, :])
    out = _layer_norm(DEEPNORM_ALPHA * x_ref[...] + mix, g_ref[...], b_ref[...])
    of_ref[...] = out
    ob_ref[...] = out.astype(BF16)


def _out_proj_ln(x, ys, w_bf, g, b, tm=512):
    t, d = x.shape
    row = lambda i: (i, 0)
    const = lambda i: (0, 0)
    return pl.pallas_call(
        _outproj_body,
        grid=(t // tm,),
        in_specs=[pl.BlockSpec((tm, d), row)] + [pl.BlockSpec((tm, GROUP_W), row)] * 4
        + [pl.BlockSpec((d, d), const), pl.BlockSpec((1, d), const), pl.BlockSpec((1, d), const)],
        out_specs=[pl.BlockSpec((tm, d), row), pl.BlockSpec((tm, d), row)],
        out_shape=[jax.ShapeDtypeStruct((t, d), F32), jax.ShapeDtypeStruct((t, d), BF16)],
        compiler_params=_cparams(("arbitrary",)),
        name="out_proj_ln",
    )(x, *ys, w_bf, g[None, :], b[None, :])


def _router_body(x_ref, wt_ref, b_ref, idx_ref, gate_ref, rank_ref, cnt_ref, carry_ref, *, tr):
    i = pl.program_id(0)

    @pl.when(i == 0)
    def _():
        carry_ref[...] = jnp.zeros(carry_ref.shape, F32)

    logits = _pdot(wt_ref[...], x_ref[...], 3, "nt") + b_ref[...][:, 0:1]
    e_iota = _iota((N_EXPERTS, tr), 0)
    work = logits
    onehot = jnp.zeros((N_EXPERTS, tr), F32)
    sels, vals, idxs = [], [], []
    for _ in range(TOP_K):
        m = jnp.max(work, axis=0, keepdims=True)
        idx = jnp.min(jnp.where(work == m, e_iota, N_EXPERTS), axis=0, keepdims=True)
        sel = e_iota == idx
        work = jnp.where(sel, -jnp.inf, work)
        onehot = onehot + sel.astype(F32)
        sels.append(sel)
        vals.append(m)
        idxs.append(idx)
    exps = [jnp.exp(v - vals[0]) for v in vals]
    tot = exps[0] + exps[1] + exps[2] + exps[3]
    upper = (_iota((tr, tr), 0) < _iota((tr, tr), 1)).astype(BF16)
    carry = carry_ref[...][:, 0:1]
    before = _dot(_bf(onehot), upper) + carry
    ranks = [jnp.sum(jnp.where(s, before, 0.0), axis=0, keepdims=True) for s in sels]
    idx_ref[...] = jnp.concatenate(idxs, axis=0)
    gate_ref[...] = jnp.concatenate([e / tot for e in exps], axis=0)
    rank_ref[...] = jnp.concatenate(ranks, axis=0).astype(jnp.int32)
    new_carry = carry + jnp.sum(onehot, axis=1, keepdims=True)
    carry_ref[...] = jnp.broadcast_to(new_carry, carry_ref.shape)
    cnt_ref[...] = jnp.broadcast_to(new_carry, cnt_ref.shape)


def _router(x, w_t, b, tr=512):
    t, d = x.shape
    return pl.pallas_call(
        functools.partial(_router_body, tr=tr),
        grid=(t // tr,),
        in_specs=[pl.BlockSpec((tr, d), lambda i: (i, 0)),
                  pl.BlockSpec((N_EXPERTS, d), lambda i: (0, 0)),
                  pl.BlockSpec((N_EXPERTS, 128), lambda i: (0, 0))],
        out_specs=[pl.BlockSpec((TOP_K, tr), lambda i: (0, i)),
                   pl.BlockSpec((TOP_K, tr), lambda i: (0, i)),
                   pl.BlockSpec((TOP_K, tr), lambda i: (0, i)),
                   pl.BlockSpec((N_EXPERTS, 128), lambda i: (0, 0))],
        out_shape=[jax.ShapeDtypeStruct((TOP_K, t), jnp.int32),
                   jax.ShapeDtypeStruct((TOP_K, t), F32),
                   jax.ShapeDtypeStruct((TOP_K, t), jnp.int32),
                   jax.ShapeDtypeStruct((N_EXPERTS, 128), F32)],
        scratch_shapes=[pltpu.VMEM((N_EXPERTS, 128), F32)],
        compiler_params=_cparams(("arbitrary",)),
        name="router",
    )(x, w_t, jnp.broadcast_to(b[:, None], (N_EXPERTS, 128)))


def _route_plan_body(idx_ref, rank_ref, cnt_ref, dest_ref, be_ref, nv_ref, ps_ref, pl_ref,
                     *, blk, nb_pad):
    E = N_EXPERTS
    cnt = cnt_ref[...][:, 0:1]
    padded = jnp.floor((cnt + (blk - 1)) / blk) * blk
    lower = (_iota((E, E), 1) < _iota((E, E), 0)).astype(F32)
    pstart = jnp.sum(lower * _col_to_row(padded), axis=1, keepdims=True)
    pend = pstart + padded
    ps_ref[...] = jnp.broadcast_to(pstart + cnt, ps_ref.shape).astype(jnp.int32)
    pl_ref[...] = jnp.broadcast_to(padded - cnt, pl_ref.shape).astype(jnp.int32)
    idx = idx_ref[...]
    dest = rank_ref[...].astype(F32)
    for e in range(E):
        dest = dest + jnp.where(idx == e, pstart[e:e + 1, 0:1], 0.0)
    dest_ref[...] = dest.astype(jnp.int32)
    blk_start = (_iota((1, nb_pad), 1) * blk).astype(F32)
    be = jnp.sum((pend <= blk_start).astype(F32), axis=0, keepdims=True)
    be_ref[...] = jnp.minimum(be, E - 1).astype(jnp.int32)
    nv_ref[...] = jnp.broadcast_to(pend[E - 1:E, 0:1] / blk, nv_ref.shape).astype(jnp.int32)


def _route_plan(idx_t, rank_t, counts, blk, nb_pad):
    t = idx_t.shape[1]
    return pl.pallas_call(
        functools.partial(_route_plan_body, blk=blk, nb_pad=nb_pad),
        out_shape=[jax.ShapeDtypeStruct((TOP_K, t), jnp.int32),
                   jax.ShapeDtypeStruct((1, nb_pad), jnp.int32),
                   jax.ShapeDtypeStruct((1, 128), jnp.int32),
                   jax.ShapeDtypeStruct((N_EXPERTS, 128), jnp.int32),
                   jax.ShapeDtypeStruct((N_EXPERTS, 128), jnp.int32)],
        compiler_params=pltpu.CompilerParams(vmem_limit_bytes=VMEM_LIMIT),
        name="route_plan",
    )(idx_t, rank_t, counts)


def _dispatch_body(ps_ref, pl_ref, nv_ref, dest_ref, x_ref, xs_hbm, zbuf, sem, fsem,
                   *, td, blk, n_blocks):
    sizes = [blk >> (s + 1) for s in range(int(np.log2(blk)) - 3)]

    def pad_piece(e, s):
        n = pl_ref[e]
        first = ps_ref[e] + jnp.bitwise_and(n, 7)
        off = pl.multiple_of(first + jnp.bitwise_and(n, -2 * s), 8)
        return (jnp.bitwise_and(n, s) != 0,
                pltpu.make_async_copy(zbuf.at[pl.ds(0, s)], xs_hbm.at[pl.ds(off, s)], fsem))

    def pad_row(e, r):
        return (r < jnp.bitwise_and(pl_ref[e], 7),
                pltpu.make_async_copy(zbuf.at[pl.ds(0, 1)], xs_hbm.at[pl.ds(ps_ref[e] + r, 1)], fsem))

    def tail_copy(b):
        off = pl.multiple_of(b * blk, blk)
        return pltpu.make_async_copy(zbuf, xs_hbm.at[pl.ds(off, blk)], fsem)

    @pl.when(pl.program_id(0) == 0)
    def _():
        zbuf[...] = jnp.zeros(zbuf.shape, zbuf.dtype)

        def pads(start):
            def body(e, carry):
                for cond, cp in [pad_piece(e, s) for s in sizes] + [pad_row(e, r) for r in range(7)]:
                    @pl.when(cond)
                    def _():
                        cp.start() if start else cp.wait()
                return carry
            lax.fori_loop(0, N_EXPERTS, body, 0)

        def tails(start):
            def body(b, carry):
                tail_copy(b).start() if start else tail_copy(b).wait()
                return carry
            lax.fori_loop(nv_ref[0], n_blocks, body, 0)

        pads(True)
        tails(True)
        pads(False)
        tails(False)

    def issue(t, carry):
        for k in range(TOP_K):
            pltpu.make_async_copy(x_ref.at[pl.ds(t, 1)],
                                  xs_hbm.at[pl.ds(dest_ref[k, t], 1)], sem).start()
        return carry

    lax.fori_loop(0, td, issue, 0, unroll=8)
    for _ in range(TOP_K):
        pltpu.make_async_copy(x_ref, xs_hbm.at[pl.ds(0, td)], sem).wait()


def _dispatch(pad_start, pad_len, nv, dest_t, x, n_rows, blk, td=512):
    t, d = x.shape
    grid_spec = pltpu.PrefetchScalarGridSpec(
        num_scalar_prefetch=3,
        grid=(t // td,),
        in_specs=[pl.BlockSpec((TOP_K, td), lambda i, *_: (0, i), memory_space=pltpu.SMEM),
                  pl.BlockSpec((td, d), lambda i, *_: (i, 0))],
        out_specs=pl.BlockSpec(memory_space=pl.ANY),
        scratch_shapes=[pltpu.VMEM((blk, d), x.dtype), pltpu.SemaphoreType.DMA(()),
                        pltpu.SemaphoreType.DMA(())],
    )
    return pl.pallas_call(
        functools.partial(_dispatch_body, td=td, blk=blk, n_blocks=n_rows // blk),
        grid_spec=grid_spec,
        out_shape=jax.ShapeDtypeStruct((n_rows, d), x.dtype),
        compiler_params=_cparams(("arbitrary",)),
        name="dispatch",
    )(pad_start, pad_len, nv, dest_t, x)


def _expert_body(be_ref, nv_ref, xs_ref, wgu_ref, bgu_ref, wd_ref, bd_ref, ys_ref, wgu_bf, wd_bf):
    b = pl.program_id(0)
    d = D_MODEL

    @pl.when(b < nv_ref[0])
    def _():
        prev = be_ref[jnp.maximum(b - 1, 0)]

        @pl.when((b == 0) | (be_ref[b] != prev))
        def _():
            wgu_bf[...] = wgu_ref[...].astype(BF16)
            wd_bf[...] = wd_ref[...].astype(BF16)

        h = _dot(_bf(xs_ref[...]), wgu_bf[...]) + bgu_ref[...]
        hg = jnp.minimum(h[:, 0:d], SWIGLU_LIMIT)
        hl = jnp.clip(h[:, d:2 * d], -SWIGLU_LIMIT, SWIGLU_LIMIT)
        act = (hl + 1.0) * (hg * _sigmoid(hg * SWIGLU_ALPHA))
        ys_ref[...] = _dot(_bf(act), wd_bf[...]) + bd_ref[...]

    @pl.when(b >= nv_ref[0])
    def _():
        ys_ref[...] = jnp.zeros(ys_ref.shape, F32)


def _experts(be, nv, xs, w_gu, b_gu, w_down, b_down, layer, blk):
    n_rows, d = xs.shape
    nb = n_rows // blk

    def row(b, be_r, nv_r):
        return (jnp.minimum(b, nv_r[0] - 1), 0)

    def wsel(b, be_r, nv_r):
        return (layer, be_r[jnp.minimum(b, nv_r[0] - 1)], 0, 0)

    grid_spec = pltpu.PrefetchScalarGridSpec(
        num_scalar_prefetch=2,
        grid=(nb,),
        in_specs=[pl.BlockSpec((blk, d), row),
                  pl.BlockSpec((None, None, d, 2 * d), wsel),
                  pl.BlockSpec((None, None, 1, 2 * d), wsel),
                  pl.BlockSpec((None, None, d, d), wsel),
                  pl.BlockSpec((None, None, 1, d), wsel)],
        out_specs=pl.BlockSpec((blk, d), lambda b, be_r, nv_r: (b, 0)),
        scratch_shapes=[pltpu.VMEM((d, 2 * d), BF16), pltpu.VMEM((d, d), BF16)],
    )
    return pl.pallas_call(
        _expert_body,
        grid_spec=grid_spec,
        out_shape=jax.ShapeDtypeStruct((n_rows, d), F32),
        compiler_params=_cparams(("arbitrary",)),
        name="experts",
    )(be, nv, xs, w_gu, b_gu, w_down, b_down)


def _combine_body(dcur_ref, dnext_ref, ys_hbm, x_ref, gate_ref, p_ref, wg_ref, wp_ref, g_ref, b_ref,
                  of_ref, ob_ref, ybuf, sem, *, tc):
    i = pl.program_id(0)
    n = pl.num_programs(0)
    slot = lax.rem(i, 2)

    def issue(d_ref, s):
        def body(t, carry):
            for k in range(TOP_K):
                pltpu.make_async_copy(ys_hbm.at[pl.ds(d_ref[k, t], 1)],
                                      ybuf.at[s, k, pl.ds(t, 1)], sem.at[s]).start()
            return carry
        lax.fori_loop(0, tc, body, 0, unroll=8)

    @pl.when(i == 0)
    def _():
        issue(dcur_ref, 0)

    @pl.when(i + 1 < n)
    def _():
        issue(dnext_ref, 1 - slot)

    for k in range(TOP_K):
        pltpu.make_async_copy(ys_hbm.at[pl.ds(0, tc)], ybuf.at[slot, k], sem.at[slot]).wait()

    gate = gate_ref[...]
    ffn = ybuf[slot, 0] * gate[:, 0:1]
    for k in range(1, TOP_K):
        ffn = ffn + ybuf[slot, k] * gate[:, k:k + 1]
    h = DEEPNORM_ALPHA * x_ref[...] + ffn
    ple = _dot(_bf(p_ref[...]), wp_ref[...])
    h = h + _sigmoid(_dot(_bf(h), wg_ref[...])) * ple
    out = _layer_norm(h, g_ref[...], b_ref[...])
    of_ref[...] = out
    ob_ref[...] = out.astype(BF16)


def _combine(dest_t, ys, x, gate, p, wg_bf, wp_bf, g, b, layer, tc=256):
    t, d = x.shape
    n = t // tc
    row = lambda i: (i, 0)
    const = lambda i: (0, 0)
    return pl.pallas_call(
        functools.partial(_combine_body, tc=tc),
        grid=(n,),
        in_specs=[pl.BlockSpec((TOP_K, tc), lambda i: (0, i), memory_space=pltpu.SMEM),
                  pl.BlockSpec((TOP_K, tc), lambda i: (0, jnp.minimum(i + 1, n - 1)),
                               memory_space=pltpu.SMEM),
                  pl.BlockSpec(memory_space=pl.ANY),
                  pl.BlockSpec((tc, d), row),
                  pl.BlockSpec((tc, TOP_K), row),
                  pl.BlockSpec((None, tc, PLE_DIM), lambda i: (layer, i, 0)),
                  pl.BlockSpec((d, d), const),
                  pl.BlockSpec((PLE_DIM, d), const),
                  pl.BlockSpec((1, d), const),
                  pl.BlockSpec((1, d), const)],
        out_specs=[pl.BlockSpec((tc, d), row), pl.BlockSpec((tc, d), row)],
        out_shape=[jax.ShapeDtypeStruct((t, d), F32), jax.ShapeDtypeStruct((t, d), BF16)],
        scratch_shapes=[pltpu.VMEM((2, TOP_K, tc, d), F32), pltpu.SemaphoreType.DMA((2,))],
        compiler_params=_cparams(("arbitrary",)),
        name="combine",
    )(dest_t, dest_t, ys, x, gate, p, wg_bf, wp_bf, g[None, :], b[None, :])


def kernel(x, p, w_in, w_out, ln1_g, ln1_b, ssd_conv_w, ssd_conv_b, ssd_a_log, ssd_dt_bias, ssd_d, ssd_norm_g, rwkv_mu, rwkv_w0, rwkv_w_up, rwkv_a0, rwkv_a_up, rwkv_g_up, rwkv_k_k, rwkv_k_a, rwkv_r_k, rwkv_ln_g, rwkv_ln_b, gla_gate_up, gla_gate_b, gla_norm_g, mlstm_conv_w, mlstm_conv_b, mlstm_i_b, mlstm_f_b, mlstm_norm_g, router_w, router_b, exp_w_gu, exp_b_gu, exp_w_down, exp_b_down, ple_gate_w, ple_proj, ln2_g, ln2_b):
    bsz, seq, d = x.shape
    t = bsz * seq
    depth = w_in.shape[0]
    blk = EXPERT_BLK
    n_blocks = -(-(t * TOP_K + N_EXPERTS * (blk - 1)) // blk)
    n_rows = n_blocks * blk
    nb_pad = -(-n_blocks // 128) * 128

    xf = x.reshape(t, d)
    xb = xf.astype(BF16)
    p2 = p.reshape(depth, t, PLE_DIM)
    b_gu4 = exp_b_gu[:, :, None, :]
    b_down4 = exp_b_down[:, :, None, :]
    w_packed = _pack_w_in(w_in)
    for i in range(depth):
        u = _in_proj(xb, w_packed, i)
        y_ssd = _ssd_mixer(u, bsz, seq, ssd_conv_w[i], ssd_conv_b[i], ssd_a_log[i], ssd_dt_bias[i],
                           ssd_d[i], ssd_norm_g[i])
        y_rwkv = _rwkv_mixer(u, bsz, seq, rwkv_mu[i], rwkv_w0[i], rwkv_w_up[i], rwkv_a0[i],
                             rwkv_a_up[i], rwkv_g_up[i], rwkv_k_k[i], rwkv_k_a[i], rwkv_r_k[i],
                             rwkv_ln_g[i], rwkv_ln_b[i])
        y_gla = _gla_mixer(u, bsz, seq, gla_gate_up[i], gla_gate_b[i], gla_norm_g[i])
        y_ml = _mlstm_mixer(u, bsz, seq, mlstm_conv_w[i], mlstm_conv_b[i], mlstm_i_b[i],
                            mlstm_f_b[i], mlstm_norm_g[i])
        x1f, _ = _out_proj_ln(xf, (y_ssd, y_rwkv, y_gla, y_ml), w_out[i].astype(BF16),
                              ln1_g[i], ln1_b[i])
        idx_t, gate_t, rank_t, counts = _router(x1f, router_w[i].T, router_b[i])
        dest_t, be, nv, pad_start, pad_len = _route_plan(idx_t, rank_t, counts, blk, nb_pad)
        nv = nv.reshape(128)
        xs = _dispatch(pad_start[:, 0], pad_len[:, 0], nv, dest_t, x1f, n_rows, blk)
        ys = _experts(be.reshape(nb_pad), nv, xs, exp_w_gu, b_gu4, exp_w_down, b_down4, i, blk)
        xf, xb = _combine(dest_t, ys, x1f, gate_t.T, p2, ple_gate_w[i].astype(BF16),
                          ple_proj[i].astype(BF16), ln2_g[i], ln2_b[i], i)
    return xf.reshape(bsz, seq, d)
```

```python
import functools

import jax
import jax.numpy as jnp
import numpy as np
from jax import lax
from jax.experimental import pallas as pl
from jax.experimental.pallas import tpu as pltpu

F32 = jnp.float32
BF16 = jnp.bfloat16
HI = lax.Precision.HIGHEST

D_MODEL = 1024
DEPTH = 4
GROUP_W = 256
N_HEADS = 4
HEAD_DIM = 64
SSD_STATE = 128
SSD_CHUNK = 128
GLA_DK = 128
GLA_HEAD_K = 32
GLA_TAU = 16.0
CHUNK = 64
RWKV_P_SCORE = 1
RWKV_P_INV = 1
RWKV_P_RHS = 1
RWKV_P_OUT = 1
RWKV_P_STATE = 1
RWKV_GN_EPS = 64e-5
NORM_EPS = 1e-5
LN_EPS = 1e-5
N_EXPERTS = 32
TOP_K = 4
SWIGLU_LIMIT = 7.0
SWIGLU_ALPHA = 1.702
PLE_DIM = 256
DEEPNORM_ALPHA = (2 * DEPTH) ** 0.25

U_WIDTH = 3584
COL_SSD_Z, COL_SSD_X, COL_SSD_BC = 0, 256, 512
COL_RWKV = 768
COL_ML_QK = 1536
COL_ML_VO = 2048
COL_GLA_QK, COL_GLA_V, COL_GLA_OG = 2560, 2816, 3072
COL_LORA = 3328
COL_MISC = 3456
MISC_DT, MISC_GLA, MISC_I, MISC_F = 0, 4, 20, 24

SSD_TB = GLA_TB = RWKV_TB = 512
MLSTM_TB = 256
MIXER_NB = 1
EXPERT_BLK = 512
VMEM_LIMIT = 56 * 1024 * 1024


def _cparams(sem):
    return pltpu.CompilerParams(dimension_semantics=sem, vmem_limit_bytes=VMEM_LIMIT)


def _dot(a, b, prec=None):
    return lax.dot_general(a, b, (((1,), (0,)), ((), ())), precision=prec,
                           preferred_element_type=F32)


def _dot_nt(a, b, prec=None):
    return lax.dot_general(a, b, (((1,), (1,)), ((), ())), precision=prec,
                           preferred_element_type=F32)


def _dot_tn(a, b, prec=None):
    return lax.dot_general(a, b, (((0,), (0,)), ((), ())), precision=prec,
                           preferred_element_type=F32)


def _bf(x):
    return x.astype(BF16)


_DIMS = {"nn": (((1,), (0,)), ((), ())), "nt": (((1,), (1,)), ((), ())),
         "tn": (((0,), (0,)), ((), ()))}


def _split_bf(x):
    hi = x.astype(BF16)
    return hi, (x - hi.astype(F32)).astype(BF16)


def _pdot(a, b, passes, kind="nn"):
    dn = _DIMS[kind]
    if passes == 6:
        return lax.dot_general(a, b, dn, precision=HI, preferred_element_type=F32)
    dg = lambda x, y: lax.dot_general(x, y, dn, preferred_element_type=F32)
    if passes == 1:
        return dg(_bf(a), _bf(b))
    ah, al = _split_bf(a)
    bh, bl = _split_bf(b)
    return dg(ah, bh) + (dg(ah, bl) + dg(al, bh))


def _pieces(x, n):
    out, rem = [], x
    for _ in range(n):
        part = rem.astype(BF16)
        out.append(part)
        rem = rem - part.astype(F32)
    return out


def _dot_lx(a_exact, b, n):
    ab = _bf(a_exact)
    acc = None
    for part in _pieces(b, n):
        term = lax.dot_general(ab, part, _DIMS["nn"], preferred_element_type=F32)
        acc = term if acc is None else acc + term
    return acc


def _dot_rx(a, b_exact, n):
    bb = _bf(b_exact)
    acc = None
    for part in _pieces(a, n):
        term = lax.dot_general(part, bb, _DIMS["nn"], preferred_element_type=F32)
        acc = term if acc is None else acc + term
    return acc


def _sigmoid(x):
    return 1.0 / (1.0 + jnp.exp(-x))


def _silu(x):
    return x * _sigmoid(x)


def _softplus(x):
    return jnp.maximum(x, 0.0) + jnp.log1p(jnp.exp(-jnp.abs(x)))


def _log_sigmoid(x):
    return jnp.minimum(x, 0.0) - jnp.log1p(jnp.exp(-jnp.abs(x)))


def _iota(shape, dim):
    return lax.broadcasted_iota(jnp.int32, shape, dim)


def _tri_incl(n):
    return (_iota((n, n), 0) >= _iota((n, n), 1)).astype(F32)


def _seg_matrix(n, seg, value):
    sh = int(np.log2(seg))
    same = (lax.shift_right_logical(_iota((n, n), 0), sh)
            == lax.shift_right_logical(_iota((n, n), 1), sh))
    return jnp.where(same, value, 0.0).astype(F32)


def _head_mask(width, seg, h):
    lane = _iota((1, width), 1)
    return ((lane >= h * seg) & (lane < (h + 1) * seg)).astype(F32)


def _expand(cols, seg):
    n = len(cols)
    rows = cols[0].shape[0]
    width = n * seg
    lane = _iota((rows, width), 1)
    out = jnp.broadcast_to(cols[n - 1], (rows, width))
    for h in range(n - 2, -1, -1):
        out = jnp.where(lane < (h + 1) * seg, jnp.broadcast_to(cols[h], (rows, width)), out)
    return out


def _col_to_row(col):
    n = col.shape[0]
    eye = _iota((n, n), 0) == _iota((n, n), 1)
    return jnp.sum(jnp.where(eye, col, 0.0), axis=0, keepdims=True)


def _layer_norm(x, g, b):
    mu = jnp.mean(x, axis=-1, keepdims=True)
    xc = x - mu
    var = jnp.mean(xc * xc, axis=-1, keepdims=True)
    return xc * lax.rsqrt(var + LN_EPS) * g + b


def _causal_conv_silu(buf, w_ref, b_ref, rows):
    acc = b_ref[...]
    for k in range(4):
        acc = acc + buf[pl.ds(5 + k, rows), :] * w_ref[k:k + 1, :]
    return _silu(acc)


_W_IN_COLS = 3484
_W_RWKV, _W_GLA, _W_ML = 772, 772 + 896, 772 + 896 + 784
_PACK_SEGMENTS = (
    (COL_SSD_Z, 0, 768),
    (COL_RWKV, _W_RWKV, 768),
    (COL_ML_QK, _W_ML, 768),
    (COL_ML_VO + 256, _W_ML + 776, 256),
    (COL_GLA_QK, _W_GLA, 512),
    (COL_GLA_OG, _W_GLA + 528, 256),
    (COL_LORA, _W_RWKV + 768, 128),
)
_PACK_MISC = ((768, MISC_DT, 4), (_W_GLA + 512, MISC_GLA, 16), (_W_ML + 768, MISC_I, 8))


def _pack_body(w_ref, o_ref):
    def cols(src, n):
        sh = src % 128
        a0 = src - sh
        if sh == 0:
            return w_ref[:, a0:a0 + n]
        wd = -(-(sh + n) // 128) * 128
        return pltpu.roll(w_ref[:, a0:a0 + wd], wd - sh, axis=1)[:, 0:n]

    for dst, src, n in _PACK_SEGMENTS:
        o_ref[:, dst:dst + n] = cols(src, n).astype(BF16)
    lane = _iota((w_ref.shape[0], 128), 1)
    misc = jnp.zeros((w_ref.shape[0], 128), F32)
    for src, lane0, n in _PACK_MISC:
        assert src % 128 == lane0
        a0 = src - lane0
        misc = jnp.where((lane >= lane0) & (lane < lane0 + n), w_ref[:, a0:a0 + 128], misc)
    o_ref[:, COL_MISC:COL_MISC + 128] = misc.astype(BF16)


def _pack_w_in(w_in, rb=256):
    depth, d, n = w_in.shape
    assert n == _W_IN_COLS
    return pl.pallas_call(
        _pack_body,
        grid=(depth, d // rb),
        in_specs=[pl.BlockSpec((None, rb, U_WIDTH), lambda l, r: (l, r, 0))],
        out_specs=pl.BlockSpec((None, rb, U_WIDTH), lambda l, r: (l, r, 0)),
        out_shape=jax.ShapeDtypeStruct((depth, d, U_WIDTH), BF16),
        compiler_params=_cparams(("arbitrary", "arbitrary")),
        name="pack_w_in",
    )(w_in)


def _in_proj_body(x_ref, w_ref, o_ref):
    o_ref[...] = jnp.dot(x_ref[...], w_ref[...], preferred_element_type=F32)


def _in_proj(x_bf, w_all, layer, tm=512, tn=1792):
    t, k = x_bf.shape
    n = w_all.shape[2]
    return pl.pallas_call(
        _in_proj_body,
        grid=(n // tn, t // tm),
        in_specs=[pl.BlockSpec((tm, k), lambda j, i: (i, 0)),
                  pl.BlockSpec((None, k, tn), lambda j, i: (layer, 0, j))],
        out_specs=pl.BlockSpec((tm, tn), lambda j, i: (i, j)),
        out_shape=jax.ShapeDtypeStruct((t, n), F32),
        compiler_params=_cparams(("arbitrary", "arbitrary")),
        name="in_proj",
    )(x_bf, w_all)


def _ssd_body(z_ref, x_ref, bc_ref, misc_ref, cwx_ref, cwbc_ref, cbx_ref, cbbc_ref,
              alog_ref, dtb_ref, dskip_ref, ng_ref, o_ref, xbuf, bcbuf, st_ref, *, tb):
    xbuf[8:8 + tb, :] = x_ref[...]
    bcbuf[8:8 + tb, :] = bc_ref[...]
    xs_all = _causal_conv_silu(xbuf, cwx_ref, cbx_ref, tb)
    bc_all = _causal_conv_silu(bcbuf, cwbc_ref, cbbc_ref, tb)
    xbuf[0:8, :] = xbuf[tb:tb + 8, :]
    bcbuf[0:8, :] = bcbuf[tb:tb + 8, :]

    dt_all = _softplus(misc_ref[...] + dtb_ref[...])
    adt_all = dt_all * (-jnp.exp(alog_ref[...]))
    z_all = z_ref[...]
    L = SSD_CHUNK
    tri = _tri_incl(L)
    causal = _iota((L, L), 0) >= _iota((L, L), 1)
    masks = [_head_mask(GROUP_W, HEAD_DIM, h) for h in range(N_HEADS)]

    def chunk(c):
        rows = slice(c * L, (c + 1) * L)
        xs = xs_all[rows]
        bm = _bf(bc_all[rows, 0:SSD_STATE])
        cm = _bf(bc_all[rows, SSD_STATE:2 * SSD_STATE])
        dt = dt_all[rows]
        acum = _dot_lx(tri, adt_all[rows], 3)
        cols = [acum[:, h:h + 1] for h in range(N_HEADS)]
        lasts = [acum[L - 1:L, h:h + 1] for h in range(N_HEADS)]
        xc = xs * _expand([dt[:, h:h + 1] for h in range(N_HEADS)], HEAD_DIM)
        g = _dot_nt(cm, bm)
        y = jnp.zeros((L, GROUP_W), F32)
        for h in range(N_HEADS):
            lmat = jnp.exp(jnp.where(causal, cols[h] - _col_to_row(cols[h]), -jnp.inf))
            y = y + _dot(_bf(g * lmat), _bf(xc * masks[h]))
        dec_states = _expand([jnp.exp(lasts[h] - cols[h]) for h in range(N_HEADS)], HEAD_DIM)
        st_prev = st_ref[...]
        y = y + _dot(cm, _bf(st_prev)) * _expand([jnp.exp(cols[h]) for h in range(N_HEADS)], HEAD_DIM)
        st_new = _dot_tn(bm, _bf(xc * dec_states))
        st_ref[...] = st_prev * _expand([jnp.exp(lasts[h]) for h in range(N_HEADS)], HEAD_DIM) + st_new
        y = y + xs * dskip_ref[...]
        y = y * _silu(z_all[rows])
        ms = jnp.mean(y * y, axis=-1, keepdims=True)
        o_ref[rows, :] = (y * lax.rsqrt(ms + NORM_EPS) * ng_ref[...]).astype(o_ref.dtype)

    return tb // L, chunk, lambda: None


def _launch_mixer(body, name, u, bsz, seq, tb, u_blocks, consts, scratch):
    nbb = MIXER_NB
    u3 = u.reshape(bsz, seq, u.shape[-1])
    n_in, n_c = len(u_blocks), len(consts)

    n_s = len(scratch)

    def batched(*refs):
        ins, cs = refs[:n_in], refs[n_in:n_in + n_c]
        out, scr = refs[n_in + n_c], refs[n_in + n_c + 1:]

        @pl.when(pl.program_id(1) == 0)
        def _():
            for r in scr:
                r[...] = jnp.zeros(r.shape, F32)

        seqs = [body(*[r.at[bb] for r in ins], *cs, out.at[bb], *scr[bb * n_s:(bb + 1) * n_s], tb=tb)
                for bb in range(nbb)]
        for c in range(seqs[0][0]):
            for _, chunk, _ in seqs:
                chunk(c)
        for _, _, finish in seqs:
            finish()

    in_specs = [pl.BlockSpec((nbb, tb, w), functools.partial(lambda g, j, cb: (g, j, cb), cb=c // w))
                for w, c in u_blocks]
    in_specs += [pl.BlockSpec(a.shape, functools.partial(lambda g, j, nd: (0,) * nd, nd=a.ndim))
                 for a in consts]
    out = pl.pallas_call(
        batched,
        grid=(bsz // nbb, seq // tb),
        in_specs=in_specs,
        out_specs=pl.BlockSpec((nbb, tb, GROUP_W), lambda g, j: (g, j, 0)),
        out_shape=jax.ShapeDtypeStruct((bsz, seq, GROUP_W), BF16),
        scratch_shapes=[pltpu.VMEM(s, F32) for _ in range(nbb) for s in scratch],
        compiler_params=_cparams(("arbitrary", "arbitrary")),
        name=name,
    )(*([u3] * n_in), *consts)
    return out.reshape(bsz * seq, GROUP_W)


def _ssd_mixer(u, bsz, seq, conv_w, conv_b, a_log, dt_bias, d_skip, norm_g):
    tb = SSD_TB
    pad4 = lambda v: jnp.zeros((1, 128), F32).at[0, :N_HEADS].set(v)
    consts = (conv_w[:, :GROUP_W], conv_w[:, GROUP_W:],
              conv_b[None, :GROUP_W], conv_b[None, GROUP_W:],
              pad4(a_log), pad4(dt_bias),
              jnp.repeat(d_skip, HEAD_DIM)[None, :], norm_g[None, :])
    return _launch_mixer(
        _ssd_body, "ssd_mixer", u, bsz, seq, tb,
        [(GROUP_W, COL_SSD_Z), (GROUP_W, COL_SSD_X), (GROUP_W, COL_SSD_BC), (128, COL_MISC)],
        consts, [(tb + 8, GROUP_W), (tb + 8, GROUP_W), (SSD_STATE, GROUP_W)])


def _gla_body(qk_ref, v_ref, og_ref, misc_ref, gup_ref, gb_ref, ng_ref, o_ref, st_ref, obuf,
              *, tb):
    L = CHUNK
    q_all = qk_ref[:, 0:GLA_DK] * (GLA_HEAD_K ** -0.5)
    k_all = qk_ref[:, GLA_DK:2 * GLA_DK]
    v_all = v_ref[...]
    og_all = og_ref[...]
    gpre = _dot(_bf(misc_ref[...]), gup_ref[...]) + gb_ref[...]
    loga_all = _log_sigmoid(gpre) / GLA_TAU
    tri = _tri_incl(L)
    causal = _iota((L, L), 0) >= _iota((L, L), 1)
    kmasks = [_head_mask(GLA_DK, GLA_HEAD_K, h) for h in range(N_HEADS)]
    vmasks = [_head_mask(GROUP_W, HEAD_DIM, h) for h in range(N_HEADS)]
    bd = (lax.shift_right_logical(_iota((GROUP_W, GLA_DK), 0), 6)
          == lax.shift_right_logical(_iota((GROUP_W, GLA_DK), 1), 5))
    segmean = _seg_matrix(GROUP_W, HEAD_DIM, 1.0 / HEAD_DIM)

    def chunk(c):
        rows = slice(c * L, (c + 1) * L)
        bcum = _dot_lx(tri, loga_all[rows], 3)
        b_last = bcum[L - 1:L, :]
        q_dec = q_all[rows] * jnp.exp(bcum)
        k_inv = _bf(k_all[rows] * jnp.exp(-bcum))
        k_dec = _bf(k_all[rows] * jnp.exp(b_last - bcum))
        v = v_all[rows]
        st_prev = st_ref[...]
        o = _dot_nt(_bf(q_dec), _bf(st_prev))
        for h in range(N_HEADS):
            attn = jnp.where(causal, _dot_nt(_bf(q_dec * kmasks[h]), k_inv), 0.0)
            o = o + _dot(_bf(attn), _bf(v * vmasks[h]))
        contrib = jnp.where(bd, _dot_tn(_bf(v), k_dec), 0.0)
        st_ref[...] = st_prev * jnp.exp(b_last) + contrib
        obuf[rows, :] = o

    def finish():
        o = obuf[...]
        ms = _dot_rx(o * o, segmean, 2)
        out = o * lax.rsqrt(ms + NORM_EPS) * ng_ref[...] * _silu(og_all)
        o_ref[...] = out.astype(o_ref.dtype)

    return tb // L, chunk, finish


def _gla_mixer(u, bsz, seq, gate_up, gate_b, norm_g):
    gup = jnp.zeros((128, GLA_DK), F32).at[MISC_GLA:MISC_GLA + 16].set(gate_up).astype(BF16)
    return _launch_mixer(
        _gla_body, "gla_mixer", u, bsz, seq, GLA_TB,
        [(GROUP_W, COL_GLA_QK), (GROUP_W, COL_GLA_V), (GROUP_W, COL_GLA_OG), (128, COL_MISC)],
        (gup, gate_b[None, :], norm_g[None, :]),
        [(GROUP_W, GLA_DK), (GLA_TB, GROUP_W)])


def _mlstm_body(qk_ref, vo_ref, misc_ref, cw_ref, cb_ref, ib_ref, fb_ref, ng_ref, o_ref,
                qkbuf, c_ref, n_ref, m_ref, fin, *, tb):
    W = GROUP_W
    qkbuf[8:8 + tb, :] = qk_ref[...]
    qk = _causal_conv_silu(qkbuf, cw_ref, cb_ref, tb)
    qkbuf[0:8, :] = qkbuf[tb:tb + 8, :]
    q_all = qk[:, 0:W] * (HEAD_DIM ** -0.5)
    k_all = qk[:, W:2 * W]
    v_all = vo_ref[:, 0:W]
    og_all = vo_ref[:, W:2 * W]
    misc = misc_ref[...]
    ipre_all = misc + ib_ref[...]
    lf_all = _log_sigmoid(misc + fb_ref[...])

    L = CHUNK
    tri = _tri_incl(L)
    causal = _iota((L, L), 0) >= _iota((L, L), 1)
    masks = [_head_mask(W, HEAD_DIM, h) for h in range(N_HEADS)]
    bd = _seg_matrix(W, HEAD_DIM, 1.0)
    segmean = _seg_matrix(W, HEAD_DIM, 1.0 / HEAD_DIM)

    def prepare(c):
        rows = slice(c * L, (c + 1) * L)
        q, k, v = q_all[rows], k_all[rows], v_all[rows]
        kb = _bf(k)
        bcs = _dot_lx(tri, lf_all[rows], 3)
        ipre = ipre_all[rows]
        num = jnp.zeros((L, W), F32)
        heads, w_st = [], []
        for h in range(N_HEADS):
            b_col = bcs[:, MISC_F + h:MISC_F + h + 1]
            i_col = ipre[:, MISC_I + h:MISC_I + h + 1]
            b_last = bcs[L - 1:L, MISC_F + h:MISC_F + h + 1]
            dmat = jnp.where(causal, b_col - _col_to_row(b_col) + _col_to_row(i_col), -jnp.inf)
            m_dmat = jnp.max(dmat, axis=-1, keepdims=True)
            a_st = b_last - b_col + i_col
            m_loc = jnp.max(a_st, axis=0, keepdims=True)
            w_st.append(jnp.exp(a_st - m_loc))
            scores = _dot_nt(_bf(q * masks[h]), kb) * jnp.exp(dmat - m_dmat)
            num = num + _dot(_bf(scores), _bf(v * masks[h]))
            heads.append(dict(b_col=b_col, b_last=b_last, m_loc=m_loc, m_dmat=m_dmat,
                              rowsum=jnp.sum(scores, axis=-1, keepdims=True)))
        wst = _expand(w_st, HEAD_DIM)
        return dict(rows=rows, q=q, num=num, heads=heads,
                    c_new=bd * _dot_tn(_bf(v * wst), kb),
                    n_new=jnp.sum(k * wst, axis=0, keepdims=True))

    def chunk(_):
        for p in [prepare(c) for c in range(tb // L)]:
            rows, q = p["rows"], p["q"]
            m_state = m_ref[...]
            w_inter, corr, rowsum, eneg, s_old, s_new, m_new = [], [], [], [], [], [], []
            for h, hd in enumerate(p["heads"]):
                m_prev = m_state[:, h:h + 1]
                mn = jnp.maximum(hd["b_last"] + m_prev, hd["m_loc"])
                m_new.append(mn)
                s_old.append(jnp.exp(hd["b_last"] + m_prev - mn))
                s_new.append(jnp.exp(hd["m_loc"] - mn))
                m_inter = hd["b_col"] + m_prev
                m_row = jnp.maximum(m_inter, hd["m_dmat"])
                w_inter.append(jnp.exp(m_inter - m_row))
                cr = jnp.exp(hd["m_dmat"] - m_row)
                corr.append(cr)
                rowsum.append(cr * hd["rowsum"])
                eneg.append(jnp.exp(-m_row))
            c_prev = c_ref[...]
            n_prev = n_ref[...]
            wi = _expand(w_inter, HEAD_DIM)
            fin[0, rows, :] = _expand(corr, HEAD_DIM) * p["num"] + wi * _dot_nt(_bf(q), _bf(c_prev))
            fin[1, rows, :] = wi
            fin[2, rows, :] = q * n_prev
            fin[3, rows, :] = _expand(rowsum, HEAD_DIM)
            fin[4, rows, :] = _expand(eneg, HEAD_DIM)
            so = _expand(s_old, HEAD_DIM)
            sn = _expand(s_new, HEAD_DIM)
            c_ref[...] = so * c_prev + sn * p["c_new"]
            n_ref[...] = so * n_prev + sn * p["n_new"]
            lane = _iota((1, 128), 1)
            m_vec = jnp.zeros((1, 128), F32)
            for h in range(N_HEADS):
                m_vec = jnp.where(lane == h, m_new[h], m_vec)
            m_ref[...] = m_vec

    def finish():
        den = fin[1] * _dot_rx(fin[2], bd, 3) + fin[3]
        hval = fin[0] / jnp.maximum(jnp.abs(den), fin[4])
        hv = hval * _sigmoid(og_all)
        mean = _dot_rx(hv, segmean, 2)
        xc = hv - mean
        var = _dot_rx(xc * xc, segmean, 2)
        o_ref[...] = (xc * lax.rsqrt(var + NORM_EPS) * ng_ref[...]).astype(o_ref.dtype)

    return 1, chunk, finish


def _mlstm_mixer(u, bsz, seq, conv_w, conv_b, i_b, f_b, norm_g):
    tb = MLSTM_TB
    ib = jnp.zeros((1, 128), F32).at[0, MISC_I:MISC_I + N_HEADS].set(i_b)
    fb = jnp.zeros((1, 128), F32).at[0, MISC_F:MISC_F + N_HEADS].set(f_b)
    return _launch_mixer(
        _mlstm_body, "mlstm_mixer", u, bsz, seq, tb,
        [(2 * GROUP_W, COL_ML_QK), (2 * GROUP_W, COL_ML_VO), (128, COL_MISC)],
        (conv_w, conv_b[None, :], ib, fb, norm_g[None, :]),
        [(tb + 8, 2 * GROUP_W), (GROUP_W, GROUP_W), (1, GROUP_W), (1, 128), (5, tb, GROUP_W)])


def _stack_heads(x, masks):
    return jnp.concatenate([x * m for m in masks], axis=0)


def _rwkv_body(rkv_ref, lora_ref, mu_rkv_ref, mu_lora_ref, w0_ref, wup_ref, a0_ref, aup_ref,
               gup_ref, kk_ref, ka_ref, rk_ref, lng_ref, lnb_ref, o_ref,
               rbuf, lbuf, s_ref, obuf, *, tb):
    W = GROUP_W
    rbuf[8:8 + tb, :] = rkv_ref[...]
    lbuf[8:8 + tb, :] = lora_ref[...]
    rkv = rkv_ref[...]
    lora = lora_ref[...]
    rkv = rkv + (rbuf[pl.ds(7, tb), :] - rkv) * mu_rkv_ref[...]
    lora = lora + (lbuf[pl.ds(7, tb), :] - lora) * mu_lora_ref[...]
    rbuf[0:8, :] = rbuf[tb:tb + 8, :]
    lbuf[0:8, :] = lbuf[tb:tb + 8, :]

    r_all, k_all, v_all = rkv[:, 0:W], rkv[:, W:2 * W], rkv[:, 2 * W:3 * W]
    wpre = w0_ref[...] + _pdot(jnp.tanh(lora), wup_ref[...], 3)
    lw_all = -jnp.exp(-_softplus(-wpre) - 0.5)
    a_all = _sigmoid(a0_ref[...] + _pdot(lora, aup_ref[...], 3))
    g_all = _pdot(_sigmoid(lora), gup_ref[...], 3)
    segsum = _seg_matrix(W, HEAD_DIM, 1.0)
    segmean = _seg_matrix(W, HEAD_DIM, 1.0 / HEAD_DIM)
    kk = k_all * kk_ref[...]
    kk = kk / jnp.maximum(jnp.sqrt(_dot_rx(kk * kk, segsum, 2)), 1e-12)
    k2_all = k_all * (1.0 + (a_all - 1.0) * ka_ref[...])
    av_all = -kk
    bv_all = kk * a_all

    L = CHUNK
    HL = N_HEADS * L
    tri = _tri_incl(L)
    masks = [_head_mask(W, HEAD_DIM, h) for h in range(N_HEADS)]
    t_idx = _iota((L, HL), 0)
    s_idx = jnp.bitwise_and(_iota((L, HL), 1), L - 1)
    strict = s_idx < t_idx
    incl = s_idx <= t_idx
    eye = (_iota((HL, HL), 0) == _iota((HL, HL), 1)).astype(F32)

    def prepare(c):
        rows = slice(c * L, (c + 1) * L)
        lw = lw_all[rows]
        cum = _dot_lx(tri, lw, 3)
        cum_last = cum[L - 1:L, :]
        w_inv = jnp.exp(-cum)
        w_dec = jnp.exp(cum_last - cum)
        k2, v = k2_all[rows], v_all[rows]
        rt = r_all[rows] * jnp.exp(cum)
        at = av_all[rows] * jnp.exp(cum - lw)
        bt_bd = _stack_heads(bv_all[rows] * w_inv, masks)
        kt_bd = _stack_heads(k2 * w_inv, masks)
        v_bd = _stack_heads(v, masks)
        sc = _pdot(jnp.concatenate([at, rt], axis=0), jnp.concatenate([bt_bd, kt_bd], axis=0),
                   RWKV_P_SCORE, "nt")
        ab = jnp.where(strict, sc[0:L, 0:HL], 0.0)
        ak = jnp.where(strict, sc[0:L, HL:2 * HL], 0.0)
        rb = jnp.where(incl, sc[L:2 * L, 0:HL], 0.0)
        rk = jnp.where(incl, sc[L:2 * L, HL:2 * HL], 0.0)
        on_v = _pdot(jnp.concatenate([_stack_heads(ak, masks), rk], axis=0), v_bd, RWKV_P_RHS)
        dec_bd = jnp.concatenate([_stack_heads(bv_all[rows] * w_dec, masks),
                                  _stack_heads(k2 * w_dec, masks)], axis=0)
        return dict(rows=rows, p=_stack_heads(ab, masks), rb=rb, on_v=on_v, v_bd=v_bd,
                    dec_bd=dec_bd, s_decay=jnp.exp(cum_last),
                    lhs_s=jnp.concatenate([_stack_heads(at, masks), rt], axis=0))

    def chunk(_):
        n = tb // L
        cs = [prepare(c) for c in range(n)]
        ps = [c["p"] for c in cs]
        minvs = [eye + p for p in ps]
        for _ in range(5):
            ps = [_pdot(p, p, RWKV_P_INV) for p in ps]
            minvs = [m + _pdot(m, p, RWKV_P_INV) for m, p in zip(minvs, ps)]
        for c, minv in zip(cs, minvs):
            s_prev = s_ref[...]
            on_s = _pdot(c["lhs_s"], s_prev, RWKV_P_RHS, "nt")
            on = on_s + c["on_v"]
            sa_bd = _pdot(minv, on[0:HL], RWKV_P_RHS)
            obuf[c["rows"], :] = on[HL:HL + L] + _pdot(c["rb"], sa_bd, RWKV_P_OUT)
            s_ref[...] = s_prev * c["s_decay"] + _pdot(
                jnp.concatenate([sa_bd, c["v_bd"]], axis=0), c["dec_bd"], RWKV_P_STATE, "tn")

    def finish():
        o = obuf[...]
        mean = _dot_rx(o, segmean, 2)
        oc = o - mean
        var = _dot_rx(oc * oc, segmean, 2)
        on = oc * lax.rsqrt(var + RWKV_GN_EPS) * lng_ref[...] + lnb_ref[...]
        bonus = _dot_rx(r_all * k2_all * rk_ref[...], segsum, 2) * v_all
        o_ref[...] = ((on + bonus) * g_all).astype(o_ref.dtype)

    return 1, chunk, finish


def _rwkv_mixer(u, bsz, seq, mu, w0, w_up, a0, a_up, g_up, k_k, k_a, r_k, ln_g, ln_b):
    tb = RWKV_TB
    W = GROUP_W
    wup = jnp.zeros((128, W), F32).at[0:32].set(w_up)
    aup = jnp.zeros((128, W), F32).at[32:64].set(a_up)
    gup = jnp.zeros((128, W), F32).at[64:128].set(g_up)
    consts = (mu[None, :3 * W], mu[None, 3 * W:], w0[None, :], wup, a0[None, :], aup, gup,
              k_k[None, :], k_a[None, :], r_k.reshape(1, W), ln_g[None, :], ln_b[None, :])
    return _launch_mixer(
        _rwkv_body, "rwkv_mixer", u, bsz, seq, tb, [(3 * W, COL_RWKV), (128, COL_LORA)], consts,
        [(tb + 8, 3 * W), (tb + 8, 128), (W, W), (tb, W)])


def _outproj_body(x_ref, y0_ref, y1_ref, y2_ref, y3_ref, w_ref, g_ref, b_ref, of_ref, ob_ref):
    W = GROUP_W
    mix = _dot(y0_ref[...], w_ref[0:W, :])
    mix = mix + _dot(y1_ref[...], w_ref[W:2 * W, :])
    mix = mix + _dot(y2_ref[...], w_ref[2 * W:3 * W, :])
    mix = mix + _dot(y3_ref[...], w_ref[3 * W:4 * W, :])
    out = _layer_norm(DEEPNORM_ALPHA * x_ref[...] + mix, g_ref[...], b_ref[...])
    of_ref[...] = out
    ob_ref[...] = out.astype(BF16)


def _out_proj_ln(x, ys, w_bf, g, b, tm=512):
    t, d = x.shape
    row = lambda i: (i, 0)
    const = lambda i: (0, 0)
    return pl.pallas_call(
        _outproj_body,
        grid=(t // tm,),
        in_specs=[pl.BlockSpec((tm, d), row)] + [pl.BlockSpec((tm, GROUP_W), row)] * 4
        + [pl.BlockSpec((d, d), const), pl.BlockSpec((1, d), const), pl.BlockSpec((1, d), const)],
        out_specs=[pl.BlockSpec((tm, d), row), pl.BlockSpec((tm, d), row)],
        out_shape=[jax.ShapeDtypeStruct((t, d), F32), jax.ShapeDtypeStruct((t, d), BF16)],
        compiler_params=_cparams(("arbitrary",)),
        name="out_proj_ln",
    )(x, *ys, w_bf, g[None, :], b[None, :])


def _router_body(x_ref, wt_ref, b_ref, idx_ref, gate_ref, rank_ref, cnt_ref, carry_ref, *, tr):
    i = pl.program_id(0)

    @pl.when(i == 0)
    def _():
        carry_ref[...] = jnp.zeros(carry_ref.shape, F32)

    logits = _pdot(wt_ref[...], x_ref[...], 3, "nt") + b_ref[...][:, 0:1]
    e_iota = _iota((N_EXPERTS, tr), 0)
    work = logits
    onehot = jnp.zeros((N_EXPERTS, tr), F32)
    sels, vals, idxs = [], [], []
    for _ in range(TOP_K):
        m = jnp.max(work, axis=0, keepdims=True)
        idx = jnp.min(jnp.where(work == m, e_iota, N_EXPERTS), axis=0, keepdims=True)
        sel = e_iota == idx
        work = jnp.where(sel, -jnp.inf, work)
        onehot = onehot + sel.astype(F32)
        sels.append(sel)
        vals.append(m)
        idxs.append(idx)
    exps = [jnp.exp(v - vals[0]) for v in vals]
    tot = exps[0] + exps[1] + exps[2] + exps[3]
    upper = (_iota((tr, tr), 0) < _iota((tr, tr), 1)).astype(BF16)
    carry = carry_ref[...][:, 0:1]
    before = _dot(_bf(onehot), upper) + carry
    ranks = [jnp.sum(jnp.where(s, before, 0.0), axis=0, keepdims=True) for s in sels]
    idx_ref[...] = jnp.concatenate(idxs, axis=0)
    gate_ref[...] = jnp.concatenate([e / tot for e in exps], axis=0)
    rank_ref[...] = jnp.concatenate(ranks, axis=0).astype(jnp.int32)
    new_carry = carry + jnp.sum(onehot, axis=1, keepdims=True)
    carry_ref[...] = jnp.broadcast_to(new_carry, carry_ref.shape)
    cnt_ref[...] = jnp.broadcast_to(new_carry, cnt_ref.shape)


def _router(x, w_t, b, tr=512):
    t, d = x.shape
    return pl.pallas_call(
        functools.partial(_router_body, tr=tr),
        grid=(t // tr,),
        in_specs=[pl.BlockSpec((tr, d), lambda i: (i, 0)),
                  pl.BlockSpec((N_EXPERTS, d), lambda i: (0, 0)),
                  pl.BlockSpec((N_EXPERTS, 128), lambda i: (0, 0))],
        out_specs=[pl.BlockSpec((TOP_K, tr), lambda i: (0, i)),
                   pl.BlockSpec((TOP_K, tr), lambda i: (0, i)),
                   pl.BlockSpec((TOP_K, tr), lambda i: (0, i)),
                   pl.BlockSpec((N_EXPERTS, 128), lambda i: (0, 0))],
        out_shape=[jax.ShapeDtypeStruct((TOP_K, t), jnp.int32),
                   jax.ShapeDtypeStruct((TOP_K, t), F32),
                   jax.ShapeDtypeStruct((TOP_K, t), jnp.int32),
                   jax.ShapeDtypeStruct((N_EXPERTS, 128), F32)],
        scratch_shapes=[pltpu.VMEM((N_EXPERTS, 128), F32)],
        compiler_params=_cparams(("arbitrary",)),
        name="router",
    )(x, w_t, jnp.broadcast_to(b[:, None], (N_EXPERTS, 128)))


def _route_plan_body(idx_ref, rank_ref, cnt_ref, dest_ref, be_ref, nv_ref, ps_ref, pl_ref,
                     *, blk, nb_pad):
    E = N_EXPERTS
    cnt = cnt_ref[...][:, 0:1]
    padded = jnp.floor((cnt + (blk - 1)) / blk) * blk
    lower = (_iota((E, E), 1) < _iota((E, E), 0)).astype(F32)
    pstart = jnp.sum(lower * _col_to_row(padded), axis=1, keepdims=True)
    pend = pstart + padded
    ps_ref[...] = jnp.broadcast_to(pstart + cnt, ps_ref.shape).astype(jnp.int32)
    pl_ref[...] = jnp.broadcast_to(padded - cnt, pl_ref.shape).astype(jnp.int32)
    idx = idx_ref[...]
    dest = rank_ref[...].astype(F32)
    for e in range(E):
        dest = dest + jnp.where(idx == e, pstart[e:e + 1, 0:1], 0.0)
    dest_ref[...] = dest.astype(jnp.int32)
    blk_start = (_iota((1, nb_pad), 1) * blk).astype(F32)
    be = jnp.sum((pend <= blk_start).astype(F32), axis=0, keepdims=True)
    be_ref[...] = jnp.minimum(be, E - 1).astype(jnp.int32)
    nv_ref[...] = jnp.broadcast_to(pend[E - 1:E, 0:1] / blk, nv_ref.shape).astype(jnp.int32)


def _route_plan(idx_t, rank_t, counts, blk, nb_pad):
    t = idx_t.shape[1]
    return pl.pallas_call(
        functools.partial(_route_plan_body, blk=blk, nb_pad=nb_pad),
        out_shape=[jax.ShapeDtypeStruct((TOP_K, t), jnp.int32),
                   jax.ShapeDtypeStruct((1, nb_pad), jnp.int32),
                   jax.ShapeDtypeStruct((1, 128), jnp.int32),
                   jax.ShapeDtypeStruct((N_EXPERTS, 128), jnp.int32),
                   jax.ShapeDtypeStruct((N_EXPERTS, 128), jnp.int32)],
        compiler_params=pltpu.CompilerParams(vmem_limit_bytes=VMEM_LIMIT),
        name="route_plan",
    )(idx_t, rank_t, counts)


def _dispatch_body(ps_ref, pl_ref, nv_ref, dest_ref, x_ref, xs_hbm, zbuf, sem, fsem,
                   *, td, blk, n_blocks):
    sizes = [blk >> (s + 1) for s in range(int(np.log2(blk)) - 3)]

    def pad_piece(e, s):
        n = pl_ref[e]
        first = ps_ref[e] + jnp.bitwise_and(n, 7)
        off = pl.multiple_of(first + jnp.bitwise_and(n, -2 * s), 8)
        return (jnp.bitwise_and(n, s) != 0,
                pltpu.make_async_copy(zbuf.at[pl.ds(0, s)], xs_hbm.at[pl.ds(off, s)], fsem))

    def pad_row(e, r):
        return (r < jnp.bitwise_and(pl_ref[e], 7),
                pltpu.make_async_copy(zbuf.at[pl.ds(0, 1)], xs_hbm.at[pl.ds(ps_ref[e] + r, 1)], fsem))

    def tail_copy(b):
        off = pl.multiple_of(b * blk, blk)
        return pltpu.make_async_copy(zbuf, xs_hbm.at[pl.ds(off, blk)], fsem)

    @pl.when(pl.program_id(0) == 0)
    def _():
        zbuf[...] = jnp.zeros(zbuf.shape, zbuf.dtype)

        def pads(start):
            def body(e, carry):
                for cond, cp in [pad_piece(e, s) for s in sizes] + [pad_row(e, r) for r in range(7)]:
                    @pl.when(cond)
                    def _():
                        cp.start() if start else cp.wait()
                return carry
            lax.fori_loop(0, N_EXPERTS, body, 0)

        def tails(start):
            def body(b, carry):
                tail_copy(b).start() if start else tail_copy(b).wait()
                return carry
            lax.fori_loop(nv_ref[0], n_blocks, body, 0)

        pads(True)
        tails(True)
        pads(False)
        tails(False)

    def issue(t, carry):
        for k in range(TOP_K):
            pltpu.make_async_copy(x_ref.at[pl.ds(t, 1)],
                                  xs_hbm.at[pl.ds(dest_ref[k, t], 1)], sem).start(priority=k % 2)
        return carry

    lax.fori_loop(0, td, issue, 0, unroll=8)
    for _ in range(TOP_K):
        pltpu.make_async_copy(x_ref, xs_hbm.at[pl.ds(0, td)], sem).wait()


def _dispatch(pad_start, pad_len, nv, dest_t, x, n_rows, blk, td=512):
    t, d = x.shape
    grid_spec = pltpu.PrefetchScalarGridSpec(
        num_scalar_prefetch=3,
        grid=(t // td,),
        in_specs=[pl.BlockSpec((TOP_K, td), lambda i, *_: (0, i), memory_space=pltpu.SMEM),
                  pl.BlockSpec((td, d), lambda i, *_: (i, 0))],
        out_specs=pl.BlockSpec(memory_space=pl.ANY),
        scratch_shapes=[pltpu.VMEM((blk, d), x.dtype), pltpu.SemaphoreType.DMA(()),
                        pltpu.SemaphoreType.DMA(())],
    )
    return pl.pallas_call(
        functools.partial(_dispatch_body, td=td, blk=blk, n_blocks=n_rows // blk),
        grid_spec=grid_spec,
        out_shape=jax.ShapeDtypeStruct((n_rows, d), x.dtype),
        compiler_params=_cparams(("arbitrary",)),
        name="dispatch",
    )(pad_start, pad_len, nv, dest_t, x)


def _expert_body(be_ref, nv_ref, xs_ref, wgu_ref, bgu_ref, wd_ref, bd_ref, ys_ref, wgu_bf, wd_bf):
    b = pl.program_id(0)
    d = D_MODEL

    @pl.when(b < nv_ref[0])
    def _():
        prev = be_ref[jnp.maximum(b - 1, 0)]

        @pl.when((b == 0) | (be_ref[b] != prev))
        def _():
            wgu_bf[...] = wgu_ref[...].astype(BF16)
            wd_bf[...] = wd_ref[...].astype(BF16)

        h = _dot(_bf(xs_ref[...]), wgu_bf[...]) + bgu_ref[...]
        hg = jnp.minimum(h[:, 0:d], SWIGLU_LIMIT)
        hl = jnp.clip(h[:, d:2 * d], -SWIGLU_LIMIT, SWIGLU_LIMIT)
        act = (hl + 1.0) * (hg * _sigmoid(hg * SWIGLU_ALPHA))
        ys_ref[...] = _dot(_bf(act), wd_bf[...]) + bd_ref[...]

    @pl.when(b >= nv_ref[0])
    def _():
        ys_ref[...] = jnp.zeros(ys_ref.shape, F32)


def _experts(be, nv, xs, w_gu, b_gu, w_down, b_down, layer, blk):
    n_rows, d = xs.shape
    nb = n_rows // blk

    def row(b, be_r, nv_r):
        return (jnp.minimum(b, nv_r[0] - 1), 0)

    def wsel(b, be_r, nv_r):
        return (layer, be_r[jnp.minimum(b, nv_r[0] - 1)], 0, 0)

    grid_spec = pltpu.PrefetchScalarGridSpec(
        num_scalar_prefetch=2,
        grid=(nb,),
        in_specs=[pl.BlockSpec((blk, d), row),
                  pl.BlockSpec((None, None, d, 2 * d), wsel),
                  pl.BlockSpec((None, None, 1, 2 * d), wsel),
                  pl.BlockSpec((None, None, d, d), wsel),
                  pl.BlockSpec((None, None, 1, d), wsel)],
        out_specs=pl.BlockSpec((blk, d), lambda b, be_r, nv_r: (b, 0)),
        scratch_shapes=[pltpu.VMEM((d, 2 * d), BF16), pltpu.VMEM((d, d), BF16)],
    )
    return pl.pallas_call(
        _expert_body,
        grid_spec=grid_spec,
        out_shape=jax.ShapeDtypeStruct((n_rows, d), F32),
        compiler_params=_cparams(("arbitrary",)),
        name="experts",
    )(be, nv, xs, w_gu, b_gu, w_down, b_down)


def _combine_body(dcur_ref, dnext_ref, ys_hbm, x_ref, gate_ref, p_ref, wg_ref, wp_ref, g_ref, b_ref,
                  of_ref, ob_ref, ybuf, sem, *, tc):
    i = pl.program_id(0)
    n = pl.num_programs(0)
    slot = lax.rem(i, 2)

    def issue(d_ref, s):
        def body(t, carry):
            for k in range(TOP_K):
                pltpu.make_async_copy(ys_hbm.at[pl.ds(d_ref[k, t], 1)],
                                      ybuf.at[s, k, pl.ds(t, 1)], sem.at[s]).start(priority=k % 2)
            return carry
        lax.fori_loop(0, tc, body, 0, unroll=8)

    @pl.when(i == 0)
    def _():
        issue(dcur_ref, 0)

    @pl.when(i + 1 < n)
    def _():
        issue(dnext_ref, 1 - slot)

    for k in range(TOP_K):
        pltpu.make_async_copy(ys_hbm.at[pl.ds(0, tc)], ybuf.at[slot, k], sem.at[slot]).wait()

    gate = gate_ref[...]
    ffn = ybuf[slot, 0] * gate[:, 0:1]
    for k in range(1, TOP_K):
        ffn = ffn + ybuf[slot, k] * gate[:, k:k + 1]
    h = DEEPNORM_ALPHA * x_ref[...] + ffn
    ple = _dot(_bf(p_ref[...]), wp_ref[...])
    h = h + _sigmoid(_dot(_bf(h), wg_ref[...])) * ple
    out = _layer_norm(h, g_ref[...], b_ref[...])
    of_ref[...] = out
    ob_ref[...] = out.astype(BF16)


def _combine(dest_t, ys, x, gate, p, wg_bf, wp_bf, g, b, layer, tc=256):
    t, d = x.shape
    n = t // tc
    row = lambda i: (i, 0)
    const = lambda i: (0, 0)
    return pl.pallas_call(
        functools.partial(_combine_body, tc=tc),
        grid=(n,),
        in_specs=[pl.BlockSpec((TOP_K, tc), lambda i: (0, i), memory_space=pltpu.SMEM),
                  pl.BlockSpec((TOP_K, tc), lambda i: (0, jnp.minimum(i + 1, n - 1)),
                               memory_space=pltpu.SMEM),
                  pl.BlockSpec(memory_space=pl.ANY),
                  pl.BlockSpec((tc, d), row),
                  pl.BlockSpec((tc, TOP_K), row),
                  pl.BlockSpec((None, tc, PLE_DIM), lambda i: (layer, i, 0)),
                  pl.BlockSpec((d, d), const),
                  pl.BlockSpec((PLE_DIM, d), const),
                  pl.BlockSpec((1, d), const),
                  pl.BlockSpec((1, d), const)],
        out_specs=[pl.BlockSpec((tc, d), row), pl.BlockSpec((tc, d), row)],
        out_shape=[jax.ShapeDtypeStruct((t, d), F32), jax.ShapeDtypeStruct((t, d), BF16)],
        scratch_shapes=[pltpu.VMEM((2, TOP_K, tc, d), F32), pltpu.SemaphoreType.DMA((2,))],
        compiler_params=_cparams(("arbitrary",)),
        name="combine",
    )(dest_t, dest_t, ys, x, gate, p, wg_bf, wp_bf, g[None, :], b[None, :])


def kernel(x, p, w_in, w_out, ln1_g, ln1_b, ssd_conv_w, ssd_conv_b, ssd_a_log, ssd_dt_bias, ssd_d, ssd_norm_g, rwkv_mu, rwkv_w0, rwkv_w_up, rwkv_a0, rwkv_a_up, rwkv_g_up, rwkv_k_k, rwkv_k_a, rwkv_r_k, rwkv_ln_g, rwkv_ln_b, gla_gate_up, gla_gate_b, gla_norm_g, mlstm_conv_w, mlstm_conv_b, mlstm_i_b, mlstm_f_b, mlstm_norm_g, router_w, router_b, exp_w_gu, exp_b_gu, exp_w_down, exp_b_down, ple_gate_w, ple_proj, ln2_g, ln2_b):
    bsz, seq, d = x.shape
    t = bsz * seq
    depth = w_in.shape[0]
    blk = EXPERT_BLK
    n_blocks = -(-(t * TOP_K + N_EXPERTS * (blk - 1)) // blk)
    n_rows = n_blocks * blk
    nb_pad = -(-n_blocks // 128) * 128

    xf = x.reshape(t, d)
    xb = xf.astype(BF16)
    p2 = p.reshape(depth, t, PLE_DIM)
    b_gu4 = exp_b_gu[:, :, None, :]
    b_down4 = exp_b_down[:, :, None, :]
    w_packed = _pack_w_in(w_in)
    for i in range(depth):
        u = _in_proj(xb, w_packed, i)
        y_ssd = _ssd_mixer(u, bsz, seq, ssd_conv_w[i], ssd_conv_b[i], ssd_a_log[i], ssd_dt_bias[i],
                           ssd_d[i], ssd_norm_g[i])
        y_rwkv = _rwkv_mixer(u, bsz, seq, rwkv_mu[i], rwkv_w0[i], rwkv_w_up[i], rwkv_a0[i],
                             rwkv_a_up[i], rwkv_g_up[i], rwkv_k_k[i], rwkv_k_a[i], rwkv_r_k[i],
                             rwkv_ln_g[i], rwkv_ln_b[i])
        y_gla = _gla_mixer(u, bsz, seq, gla_gate_up[i], gla_gate_b[i], gla_norm_g[i])
        y_ml = _mlstm_mixer(u, bsz, seq, mlstm_conv_w[i], mlstm_conv_b[i], mlstm_i_b[i],
                            mlstm_f_b[i], mlstm_norm_g[i])
        x1f, _ = _out_proj_ln(xf, (y_ssd, y_rwkv, y_gla, y_ml), w_out[i].astype(BF16),
                              ln1_g[i], ln1_b[i])
        idx_t, gate_t, rank_t, counts = _router(x1f, router_w[i].T, router_b[i])
        dest_t, be, nv, pad_start, pad_len = _route_plan(idx_t, rank_t, counts, blk, nb_pad)
        nv = nv.reshape(128)
        xs = _dispatch(pad_start[:, 0], pad_len[:, 0], nv, dest_t, x1f, n_rows, blk)
        ys = _experts(be.reshape(nb_pad), nv, xs, exp_w_gu, b_gu4, exp_w_down, b_down4, i, blk)
        xf, xb = _combine(dest_t, ys, x1f, gate_t.T, p2, ple_gate_w[i].astype(BF16),
                          ple_proj[i].astype(BF16), ln2_g[i], ln2_b[i], i)
    return xf.reshape(bsz, seq, d)
```

```python
import functools

import jax
import jax.numpy as jnp
import numpy as np
from jax import lax
from jax.experimental import pallas as pl
from jax.experimental.pallas import tpu as pltpu

F32 = jnp.float32
BF16 = jnp.bfloat16
HI = lax.Precision.HIGHEST

D_MODEL = 1024
DEPTH = 4
GROUP_W = 256
N_HEADS = 4
HEAD_DIM = 64
SSD_STATE = 128
SSD_CHUNK = 128
GLA_DK = 128
GLA_HEAD_K = 32
GLA_TAU = 16.0
CHUNK = 64
RWKV_P_SCORE = 1
RWKV_P_INV = 1
RWKV_P_RHS = 1
RWKV_P_OUT = 1
RWKV_P_STATE = 1
RWKV_GN_EPS = 64e-5
NORM_EPS = 1e-5
LN_EPS = 1e-5
N_EXPERTS = 32
TOP_K = 4
SWIGLU_LIMIT = 7.0
SWIGLU_ALPHA = 1.702
PLE_DIM = 256
DEEPNORM_ALPHA = (2 * DEPTH) ** 0.25

U_WIDTH = 3584
COL_SSD_Z, COL_SSD_X, COL_SSD_BC = 0, 256, 512
COL_RWKV = 768
COL_ML_QK = 1536
COL_ML_VO = 2048
COL_GLA_QK, COL_GLA_V, COL_GLA_OG = 2560, 2816, 3072
COL_LORA = 3328
COL_MISC = 3456
MISC_DT, MISC_GLA, MISC_I, MISC_F = 0, 4, 20, 24

SSD_TB = GLA_TB = RWKV_TB = 512
MLSTM_TB = 256
RWKV_GROUP = 4
MIXER_NB = 1
EXPERT_BLK = 512
VMEM_LIMIT = 56 * 1024 * 1024


def _cparams(sem):
    return pltpu.CompilerParams(dimension_semantics=sem, vmem_limit_bytes=VMEM_LIMIT)


def _dot(a, b, prec=None):
    return lax.dot_general(a, b, (((1,), (0,)), ((), ())), precision=prec,
                           preferred_element_type=F32)


def _dot_nt(a, b, prec=None):
    return lax.dot_general(a, b, (((1,), (1,)), ((), ())), precision=prec,
                           preferred_element_type=F32)


def _dot_tn(a, b, prec=None):
    return lax.dot_general(a, b, (((0,), (0,)), ((), ())), precision=prec,
                           preferred_element_type=F32)


def _bf(x):
    return x.astype(BF16)


_DIMS = {"nn": (((1,), (0,)), ((), ())), "nt": (((1,), (1,)), ((), ())),
         "tn": (((0,), (0,)), ((), ()))}


def _split_bf(x):
    hi = x.astype(BF16)
    return hi, (x - hi.astype(F32)).astype(BF16)


def _pdot(a, b, passes, kind="nn"):
    dn = _DIMS[kind]
    if passes == 6:
        return lax.dot_general(a, b, dn, precision=HI, preferred_element_type=F32)
    dg = lambda x, y: lax.dot_general(x, y, dn, preferred_element_type=F32)
    if passes == 1:
        return dg(_bf(a), _bf(b))
    ah, al = _split_bf(a)
    bh, bl = _split_bf(b)
    return dg(ah, bh) + (dg(ah, bl) + dg(al, bh))


def _pieces(x, n):
    out, rem = [], x
    for _ in range(n):
        part = rem.astype(BF16)
        out.append(part)
        rem = rem - part.astype(F32)
    return out


def _dot_lx(a_exact, b, n):
    ab = _bf(a_exact)
    acc = None
    for part in _pieces(b, n):
        term = lax.dot_general(ab, part, _DIMS["nn"], preferred_element_type=F32)
        acc = term if acc is None else acc + term
    return acc


def _dot_rx(a, b_exact, n):
    bb = _bf(b_exact)
    acc = None
    for part in _pieces(a, n):
        term = lax.dot_general(part, bb, _DIMS["nn"], preferred_element_type=F32)
        acc = term if acc is None else acc + term
    return acc


def _sigmoid(x):
    return 1.0 / (1.0 + jnp.exp(-x))


def _silu(x):
    return x * _sigmoid(x)


def _softplus(x):
    return jnp.maximum(x, 0.0) + jnp.log1p(jnp.exp(-jnp.abs(x)))


def _log_sigmoid(x):
    return jnp.minimum(x, 0.0) - jnp.log1p(jnp.exp(-jnp.abs(x)))


def _iota(shape, dim):
    return lax.broadcasted_iota(jnp.int32, shape, dim)


def _tri_incl(n):
    return (_iota((n, n), 0) >= _iota((n, n), 1)).astype(F32)


def _seg_matrix(n, seg, value):
    sh = int(np.log2(seg))
    same = (lax.shift_right_logical(_iota((n, n), 0), sh)
            == lax.shift_right_logical(_iota((n, n), 1), sh))
    return jnp.where(same, value, 0.0).astype(F32)


def _head_mask(width, seg, h):
    lane = _iota((1, width), 1)
    return ((lane >= h * seg) & (lane < (h + 1) * seg)).astype(F32)


def _expand(cols, seg):
    n = len(cols)
    rows = cols[0].shape[0]
    width = n * seg
    lane = _iota((rows, width), 1)
    out = jnp.broadcast_to(cols[n - 1], (rows, width))
    for h in range(n - 2, -1, -1):
        out = jnp.where(lane < (h + 1) * seg, jnp.broadcast_to(cols[h], (rows, width)), out)
    return out


def _col_to_row(col):
    n = col.shape[0]
    eye = _iota((n, n), 0) == _iota((n, n), 1)
    return jnp.sum(jnp.where(eye, col, 0.0), axis=0, keepdims=True)


def _layer_norm(x, g, b):
    mu = jnp.mean(x, axis=-1, keepdims=True)
    xc = x - mu
    var = jnp.mean(xc * xc, axis=-1, keepdims=True)
    return xc * lax.rsqrt(var + LN_EPS) * g + b


def _causal_conv_silu(buf, w_ref, b_ref, rows):
    acc = b_ref[...]
    for k in range(4):
        acc = acc + buf[pl.ds(5 + k, rows), :] * w_ref[k:k + 1, :]
    return _silu(acc)


_W_IN_COLS = 3484
_W_RWKV, _W_GLA, _W_ML = 772, 772 + 896, 772 + 896 + 784
_PACK_SEGMENTS = (
    (COL_SSD_Z, 0, 768),
    (COL_RWKV, _W_RWKV, 768),
    (COL_ML_QK, _W_ML, 768),
    (COL_ML_VO + 256, _W_ML + 776, 256),
    (COL_GLA_QK, _W_GLA, 512),
    (COL_GLA_OG, _W_GLA + 528, 256),
    (COL_LORA, _W_RWKV + 768, 128),
)
_PACK_MISC = ((768, MISC_DT, 4), (_W_GLA + 512, MISC_GLA, 16), (_W_ML + 768, MISC_I, 8))


def _pack_body(w_ref, o_ref):
    def cols(src, n):
        sh = src % 128
        a0 = src - sh
        if sh == 0:
            return w_ref[:, a0:a0 + n]
        wd = -(-(sh + n) // 128) * 128
        return pltpu.roll(w_ref[:, a0:a0 + wd], wd - sh, axis=1)[:, 0:n]

    for dst, src, n in _PACK_SEGMENTS:
        o_ref[:, dst:dst + n] = cols(src, n).astype(BF16)
    lane = _iota((w_ref.shape[0], 128), 1)
    misc = jnp.zeros((w_ref.shape[0], 128), F32)
    for src, lane0, n in _PACK_MISC:
        assert src % 128 == lane0
        a0 = src - lane0
        misc = jnp.where((lane >= lane0) & (lane < lane0 + n), w_ref[:, a0:a0 + 128], misc)
    o_ref[:, COL_MISC:COL_MISC + 128] = misc.astype(BF16)


def _pack_w_in(w_in, rb=256):
    depth, d, n = w_in.shape
    assert n == _W_IN_COLS
    return pl.pallas_call(
        _pack_body,
        grid=(depth, d // rb),
        in_specs=[pl.BlockSpec((None, rb, U_WIDTH), lambda l, r: (l, r, 0))],
        out_specs=pl.BlockSpec((None, rb, U_WIDTH), lambda l, r: (l, r, 0)),
        out_shape=jax.ShapeDtypeStruct((depth, d, U_WIDTH), BF16),
        compiler_params=_cparams(("arbitrary", "arbitrary")),
        name="pack_w_in",
    )(w_in)


def _in_proj_body(x_ref, w_ref, o_ref):
    o_ref[...] = jnp.dot(x_ref[...], w_ref[...], preferred_element_type=F32)


def _in_proj(x_bf, w_all, layer, tm=512, tn=1792):
    t, k = x_bf.shape
    n = w_all.shape[2]
    return pl.pallas_call(
        _in_proj_body,
        grid=(n // tn, t // tm),
        in_specs=[pl.BlockSpec((tm, k), lambda j, i: (i, 0)),
                  pl.BlockSpec((None, k, tn), lambda j, i: (layer, 0, j))],
        out_specs=pl.BlockSpec((tm, tn), lambda j, i: (i, j)),
        out_shape=jax.ShapeDtypeStruct((t, n), F32),
        compiler_params=_cparams(("arbitrary", "arbitrary")),
        name="in_proj",
    )(x_bf, w_all)


def _ssd_body(z_ref, x_ref, bc_ref, misc_ref, cwx_ref, cwbc_ref, cbx_ref, cbbc_ref,
              alog_ref, dtb_ref, dskip_ref, ng_ref, o_ref, xbuf, bcbuf, st_ref, *, tb):
    xbuf[8:8 + tb, :] = x_ref[...]
    bcbuf[8:8 + tb, :] = bc_ref[...]
    xs_all = _causal_conv_silu(xbuf, cwx_ref, cbx_ref, tb)
    bc_all = _causal_conv_silu(bcbuf, cwbc_ref, cbbc_ref, tb)
    xbuf[0:8, :] = xbuf[tb:tb + 8, :]
    bcbuf[0:8, :] = bcbuf[tb:tb + 8, :]

    dt_all = _softplus(misc_ref[...] + dtb_ref[...])
    adt_all = dt_all * (-jnp.exp(alog_ref[...]))
    z_all = z_ref[...]
    L = SSD_CHUNK
    tri = _tri_incl(L)
    causal = _iota((L, L), 0) >= _iota((L, L), 1)
    masks = [_head_mask(GROUP_W, HEAD_DIM, h) for h in range(N_HEADS)]

    def chunk(c):
        rows = slice(c * L, (c + 1) * L)
        xs = xs_all[rows]
        bm = _bf(bc_all[rows, 0:SSD_STATE])
        cm = _bf(bc_all[rows, SSD_STATE:2 * SSD_STATE])
        dt = dt_all[rows]
        acum = _dot_lx(tri, adt_all[rows], 3)
        cols = [acum[:, h:h + 1] for h in range(N_HEADS)]
        lasts = [acum[L - 1:L, h:h + 1] for h in range(N_HEADS)]
        xc = xs * _expand([dt[:, h:h + 1] for h in range(N_HEADS)], HEAD_DIM)
        g = _dot_nt(cm, bm)
        y = jnp.zeros((L, GROUP_W), F32)
        for h in range(N_HEADS):
            lmat = jnp.exp(jnp.where(causal, cols[h] - _col_to_row(cols[h]), -jnp.inf))
            y = y + _dot(_bf(g * lmat), _bf(xc * masks[h]))
        dec_states = _expand([jnp.exp(lasts[h] - cols[h]) for h in range(N_HEADS)], HEAD_DIM)
        st_prev = st_ref[...]
        y = y + _dot(cm, _bf(st_prev)) * _expand([jnp.exp(cols[h]) for h in range(N_HEADS)], HEAD_DIM)
        st_new = _dot_tn(bm, _bf(xc * dec_states))
        st_ref[...] = st_prev * _expand([jnp.exp(lasts[h]) for h in range(N_HEADS)], HEAD_DIM) + st_new
        y = y + xs * dskip_ref[...]
        y = y * _silu(z_all[rows])
        ms = jnp.mean(y * y, axis=-1, keepdims=True)
        o_ref[rows, :] = (y * lax.rsqrt(ms + NORM_EPS) * ng_ref[...]).astype(o_ref.dtype)

    return tb // L, chunk, lambda: None


def _launch_mixer(body, name, u, bsz, seq, tb, u_blocks, consts, scratch):
    nbb = MIXER_NB
    u3 = u.reshape(bsz, seq, u.shape[-1])
    n_in, n_c = len(u_blocks), len(consts)

    n_s = len(scratch)

    def batched(*refs):
        ins, cs = refs[:n_in], refs[n_in:n_in + n_c]
        out, scr = refs[n_in + n_c], refs[n_in + n_c + 1:]

        @pl.when(pl.program_id(1) == 0)
        def _():
            for r in scr:
                r[...] = jnp.zeros(r.shape, F32)

        seqs = [body(*[r.at[bb] for r in ins], *cs, out.at[bb], *scr[bb * n_s:(bb + 1) * n_s], tb=tb)
                for bb in range(nbb)]
        for c in range(seqs[0][0]):
            for _, chunk, _ in seqs:
                chunk(c)
        for _, _, finish in seqs:
            finish()

    in_specs = [pl.BlockSpec((nbb, tb, w), functools.partial(lambda g, j, cb: (g, j, cb), cb=c // w))
                for w, c in u_blocks]
    in_specs += [pl.BlockSpec(a.shape, functools.partial(lambda g, j, nd: (0,) * nd, nd=a.ndim))
                 for a in consts]
    out = pl.pallas_call(
        batched,
        grid=(bsz // nbb, seq // tb),
        in_specs=in_specs,
        out_specs=pl.BlockSpec((nbb, tb, GROUP_W), lambda g, j: (g, j, 0)),
        out_shape=jax.ShapeDtypeStruct((bsz, seq, GROUP_W), BF16),
        scratch_shapes=[pltpu.VMEM(s, F32) for _ in range(nbb) for s in scratch],
        compiler_params=_cparams(("arbitrary", "arbitrary")),
        name=name,
    )(*([u3] * n_in), *consts)
    return out.reshape(bsz * seq, GROUP_W)


def _ssd_mixer(u, bsz, seq, conv_w, conv_b, a_log, dt_bias, d_skip, norm_g):
    tb = SSD_TB
    pad4 = lambda v: jnp.zeros((1, 128), F32).at[0, :N_HEADS].set(v)
    consts = (conv_w[:, :GROUP_W], conv_w[:, GROUP_W:],
              conv_b[None, :GROUP_W], conv_b[None, GROUP_W:],
              pad4(a_log), pad4(dt_bias),
              jnp.repeat(d_skip, HEAD_DIM)[None, :], norm_g[None, :])
    return _launch_mixer(
        _ssd_body, "ssd_mixer", u, bsz, seq, tb,
        [(GROUP_W, COL_SSD_Z), (GROUP_W, COL_SSD_X), (GROUP_W, COL_SSD_BC), (128, COL_MISC)],
        consts, [(tb + 8, GROUP_W), (tb + 8, GROUP_W), (SSD_STATE, GROUP_W)])


def _gla_body(qk_ref, v_ref, og_ref, misc_ref, gup_ref, gb_ref, ng_ref, o_ref, st_ref, obuf,
              *, tb):
    L = CHUNK
    q_all = qk_ref[:, 0:GLA_DK] * (GLA_HEAD_K ** -0.5)
    k_all = qk_ref[:, GLA_DK:2 * GLA_DK]
    v_all = v_ref[...]
    og_all = og_ref[...]
    gpre = _dot(_bf(misc_ref[...]), gup_ref[...]) + gb_ref[...]
    loga_all = _log_sigmoid(gpre) / GLA_TAU
    tri = _tri_incl(L)
    causal = _iota((L, L), 0) >= _iota((L, L), 1)
    kmasks = [_head_mask(GLA_DK, GLA_HEAD_K, h) for h in range(N_HEADS)]
    vmasks = [_head_mask(GROUP_W, HEAD_DIM, h) for h in range(N_HEADS)]
    bd = (lax.shift_right_logical(_iota((GROUP_W, GLA_DK), 0), 6)
          == lax.shift_right_logical(_iota((GROUP_W, GLA_DK), 1), 5))
    segmean = _seg_matrix(GROUP_W, HEAD_DIM, 1.0 / HEAD_DIM)

    def chunk(c):
        rows = slice(c * L, (c + 1) * L)
        bcum = _dot_lx(tri, loga_all[rows], 3)
        b_last = bcum[L - 1:L, :]
        q_dec = q_all[rows] * jnp.exp(bcum)
        k_inv = _bf(k_all[rows] * jnp.exp(-bcum))
        k_dec = _bf(k_all[rows] * jnp.exp(b_last - bcum))
        v = v_all[rows]
        st_prev = st_ref[...]
        o = _dot_nt(_bf(q_dec), _bf(st_prev))
        for h in range(N_HEADS):
            attn = jnp.where(causal, _dot_nt(_bf(q_dec * kmasks[h]), k_inv), 0.0)
            o = o + _dot(_bf(attn), _bf(v * vmasks[h]))
        contrib = jnp.where(bd, _dot_tn(_bf(v), k_dec), 0.0)
        st_ref[...] = st_prev * jnp.exp(b_last) + contrib
        obuf[rows, :] = o

    def finish():
        o = obuf[...]
        ms = _dot_rx(o * o, segmean, 2)
        out = o * lax.rsqrt(ms + NORM_EPS) * ng_ref[...] * _silu(og_all)
        o_ref[...] = out.astype(o_ref.dtype)

    return tb // L, chunk, finish


def _gla_mixer(u, bsz, seq, gate_up, gate_b, norm_g):
    gup = jnp.zeros((128, GLA_DK), F32).at[MISC_GLA:MISC_GLA + 16].set(gate_up).astype(BF16)
    return _launch_mixer(
        _gla_body, "gla_mixer", u, bsz, seq, GLA_TB,
        [(GROUP_W, COL_GLA_QK), (GROUP_W, COL_GLA_V), (GROUP_W, COL_GLA_OG), (128, COL_MISC)],
        (gup, gate_b[None, :], norm_g[None, :]),
        [(GROUP_W, GLA_DK), (GLA_TB, GROUP_W)])


def _mlstm_body(qk_ref, vo_ref, misc_ref, cw_ref, cb_ref, ib_ref, fb_ref, ng_ref, o_ref,
                qkbuf, c_ref, n_ref, m_ref, fin, *, tb):
    W = GROUP_W
    qkbuf[8:8 + tb, :] = qk_ref[...]
    qk = _causal_conv_silu(qkbuf, cw_ref, cb_ref, tb)
    qkbuf[0:8, :] = qkbuf[tb:tb + 8, :]
    q_all = qk[:, 0:W] * (HEAD_DIM ** -0.5)
    k_all = qk[:, W:2 * W]
    v_all = vo_ref[:, 0:W]
    og_all = vo_ref[:, W:2 * W]
    misc = misc_ref[...]
    ipre_all = misc + ib_ref[...]
    lf_all = _log_sigmoid(misc + fb_ref[...])

    L = CHUNK
    tri = _tri_incl(L)
    causal = _iota((L, L), 0) >= _iota((L, L), 1)
    masks = [_head_mask(W, HEAD_DIM, h) for h in range(N_HEADS)]
    bd = _seg_matrix(W, HEAD_DIM, 1.0)
    segmean = _seg_matrix(W, HEAD_DIM, 1.0 / HEAD_DIM)

    def prepare(c):
        rows = slice(c * L, (c + 1) * L)
        q, k, v = q_all[rows], k_all[rows], v_all[rows]
        kb = _bf(k)
        bcs = _dot_lx(tri, lf_all[rows], 3)
        ipre = ipre_all[rows]
        num = jnp.zeros((L, W), F32)
        heads, w_st = [], []
        for h in range(N_HEADS):
            b_col = bcs[:, MISC_F + h:MISC_F + h + 1]
            i_col = ipre[:, MISC_I + h:MISC_I + h + 1]
            b_last = bcs[L - 1:L, MISC_F + h:MISC_F + h + 1]
            dmat = jnp.where(causal, b_col - _col_to_row(b_col) + _col_to_row(i_col), -jnp.inf)
            m_dmat = jnp.max(dmat, axis=-1, keepdims=True)
            a_st = b_last - b_col + i_col
            m_loc = jnp.max(a_st, axis=0, keepdims=True)
            w_st.append(jnp.exp(a_st - m_loc))
            scores = _dot_nt(_bf(q * masks[h]), kb) * jnp.exp(dmat - m_dmat)
            num = num + _dot(_bf(scores), _bf(v * masks[h]))
            heads.append(dict(b_col=b_col, b_last=b_last, m_loc=m_loc, m_dmat=m_dmat,
                              rowsum=jnp.sum(scores, axis=-1, keepdims=True)))
        wst = _expand(w_st, HEAD_DIM)
        return dict(rows=rows, q=q, num=num, heads=heads,
                    c_new=bd * _dot_tn(_bf(v * wst), kb),
                    n_new=jnp.sum(k * wst, axis=0, keepdims=True))

    def chunk(_):
        for p in [prepare(c) for c in range(tb // L)]:
            rows, q = p["rows"], p["q"]
            m_state = m_ref[...]
            w_inter, corr, rowsum, eneg, s_old, s_new, m_new = [], [], [], [], [], [], []
            for h, hd in enumerate(p["heads"]):
                m_prev = m_state[:, h:h + 1]
                mn = jnp.maximum(hd["b_last"] + m_prev, hd["m_loc"])
                m_new.append(mn)
                s_old.append(jnp.exp(hd["b_last"] + m_prev - mn))
                s_new.append(jnp.exp(hd["m_loc"] - mn))
                m_inter = hd["b_col"] + m_prev
                m_row = jnp.maximum(m_inter, hd["m_dmat"])
                w_inter.append(jnp.exp(m_inter - m_row))
                cr = jnp.exp(hd["m_dmat"] - m_row)
                corr.append(cr)
                rowsum.append(cr * hd["rowsum"])
                eneg.append(jnp.exp(-m_row))
            c_prev = c_ref[...]
            n_prev = n_ref[...]
            wi = _expand(w_inter, HEAD_DIM)
            fin[0, rows, :] = _expand(corr, HEAD_DIM) * p["num"] + wi * _dot_nt(_bf(q), _bf(c_prev))
            fin[1, rows, :] = wi
            fin[2, rows, :] = q * n_prev
            fin[3, rows, :] = _expand(rowsum, HEAD_DIM)
            fin[4, rows, :] = _expand(eneg, HEAD_DIM)
            so = _expand(s_old, HEAD_DIM)
            sn = _expand(s_new, HEAD_DIM)
            c_ref[...] = so * c_prev + sn * p["c_new"]
            n_ref[...] = so * n_prev + sn * p["n_new"]
            lane = _iota((1, 128), 1)
            m_vec = jnp.zeros((1, 128), F32)
            for h in range(N_HEADS):
                m_vec = jnp.where(lane == h, m_new[h], m_vec)
            m_ref[...] = m_vec

    def finish():
        den = fin[1] * _dot_rx(fin[2], bd, 3) + fin[3]
        hval = fin[0] / jnp.maximum(jnp.abs(den), fin[4])
        hv = hval * _sigmoid(og_all)
        mean = _dot_rx(hv, segmean, 2)
        xc = hv - mean
        var = _dot_rx(xc * xc, segmean, 2)
        o_ref[...] = (xc * lax.rsqrt(var + NORM_EPS) * ng_ref[...]).astype(o_ref.dtype)

    return 1, chunk, finish


def _mlstm_mixer(u, bsz, seq, conv_w, conv_b, i_b, f_b, norm_g):
    tb = MLSTM_TB
    ib = jnp.zeros((1, 128), F32).at[0, MISC_I:MISC_I + N_HEADS].set(i_b)
    fb = jnp.zeros((1, 128), F32).at[0, MISC_F:MISC_F + N_HEADS].set(f_b)
    return _launch_mixer(
        _mlstm_body, "mlstm_mixer", u, bsz, seq, tb,
        [(2 * GROUP_W, COL_ML_QK), (2 * GROUP_W, COL_ML_VO), (128, COL_MISC)],
        (conv_w, conv_b[None, :], ib, fb, norm_g[None, :]),
        [(tb + 8, 2 * GROUP_W), (GROUP_W, GROUP_W), (1, GROUP_W), (1, 128), (5, tb, GROUP_W)])


def _stack_heads(x, masks):
    return jnp.concatenate([x * m for m in masks], axis=0)


def _rwkv_body(rkv_ref, lora_ref, mu_rkv_ref, mu_lora_ref, w0_ref, wup_ref, a0_ref, aup_ref,
               gup_ref, kk_ref, ka_ref, rk_ref, lng_ref, lnb_ref, o_ref,
               rbuf, lbuf, s_ref, obuf, *, tb):
    W = GROUP_W
    rbuf[8:8 + tb, :] = rkv_ref[...]
    lbuf[8:8 + tb, :] = lora_ref[...]
    rkv = rkv_ref[...]
    lora = lora_ref[...]
    rkv = rkv + (rbuf[pl.ds(7, tb), :] - rkv) * mu_rkv_ref[...]
    lora = lora + (lbuf[pl.ds(7, tb), :] - lora) * mu_lora_ref[...]
    rbuf[0:8, :] = rbuf[tb:tb + 8, :]
    lbuf[0:8, :] = lbuf[tb:tb + 8, :]

    r_all, k_all, v_all = rkv[:, 0:W], rkv[:, W:2 * W], rkv[:, 2 * W:3 * W]
    wpre = w0_ref[...] + _pdot(jnp.tanh(lora), wup_ref[...], 3)
    lw_all = -jnp.exp(-_softplus(-wpre) - 0.5)
    a_all = _sigmoid(a0_ref[...] + _pdot(lora, aup_ref[...], 3))
    g_all = _pdot(_sigmoid(lora), gup_ref[...], 3)
    segsum = _seg_matrix(W, HEAD_DIM, 1.0)
    segmean = _seg_matrix(W, HEAD_DIM, 1.0 / HEAD_DIM)
    kk = k_all * kk_ref[...]
    kk = kk / jnp.maximum(jnp.sqrt(_dot_rx(kk * kk, segsum, 2)), 1e-12)
    k2_all = k_all * (1.0 + (a_all - 1.0) * ka_ref[...])
    av_all = -kk
    bv_all = kk * a_all

    L = CHUNK
    HL = N_HEADS * L
    tri = _tri_incl(L)
    masks = [_head_mask(W, HEAD_DIM, h) for h in range(N_HEADS)]
    t_idx = _iota((L, HL), 0)
    s_idx = jnp.bitwise_and(_iota((L, HL), 1), L - 1)
    strict = s_idx < t_idx
    incl = s_idx <= t_idx
    eye = (_iota((HL, HL), 0) == _iota((HL, HL), 1)).astype(F32)

    def prepare(c):
        rows = slice(c * L, (c + 1) * L)
        lw = lw_all[rows]
        cum = _dot_lx(tri, lw, 3)
        cum_last = cum[L - 1:L, :]
        w_inv = jnp.exp(-cum)
        w_dec = jnp.exp(cum_last - cum)
        k2, v = k2_all[rows], v_all[rows]
        rt = r_all[rows] * jnp.exp(cum)
        at = av_all[rows] * jnp.exp(cum - lw)
        bt_bd = _stack_heads(bv_all[rows] * w_inv, masks)
        kt_bd = _stack_heads(k2 * w_inv, masks)
        v_bd = _stack_heads(v, masks)
        sc = _pdot(jnp.concatenate([at, rt], axis=0), jnp.concatenate([bt_bd, kt_bd], axis=0),
                   RWKV_P_SCORE, "nt")
        ab = jnp.where(strict, sc[0:L, 0:HL], 0.0)
        ak = jnp.where(strict, sc[0:L, HL:2 * HL], 0.0)
        rb = jnp.where(incl, sc[L:2 * L, 0:HL], 0.0)
        rk = jnp.where(incl, sc[L:2 * L, HL:2 * HL], 0.0)
        on_v = _pdot(jnp.concatenate([_stack_heads(ak, masks), rk], axis=0), v_bd, RWKV_P_RHS)
        dec_bd = jnp.concatenate([_stack_heads(bv_all[rows] * w_dec, masks),
                                  _stack_heads(k2 * w_dec, masks)], axis=0)
        return dict(rows=rows, p=_stack_heads(ab, masks), rb=rb, on_v=on_v, v_bd=v_bd,
                    dec_bd=dec_bd, s_decay=jnp.exp(cum_last),
                    lhs_s=jnp.concatenate([_stack_heads(at, masks), rt], axis=0))

    def inverse_steps(group):
        ps = [c["p"] for c in group]
        minvs = [eye + p for p in ps]
        for _ in range(5):
            ps = [_pdot(p, p, RWKV_P_INV) for p in ps]
            minvs = [m + _pdot(m, p, RWKV_P_INV) for m, p in zip(minvs, ps)]
            yield minvs

    def carry_state(c, minv):
        s_prev = s_ref[...]
        on_s = _pdot(c["lhs_s"], s_prev, RWKV_P_RHS, "nt")
        on = on_s + c["on_v"]
        sa_bd = _pdot(minv, on[0:HL], RWKV_P_RHS)
        obuf[c["rows"], :] = on[HL:HL + L] + _pdot(c["rb"], sa_bd, RWKV_P_OUT)
        s_ref[...] = s_prev * c["s_decay"] + _pdot(
            jnp.concatenate([sa_bd, c["v_bd"]], axis=0), c["dec_bd"], RWKV_P_STATE, "tn")

    def chunk(_):
        n, gsz = tb // L, RWKV_GROUP
        groups = [[prepare(c) for c in range(g, g + gsz)] for g in range(0, n, gsz)]
        minvs = list(inverse_steps(groups[0]))[-1]
        for gi, group in enumerate(groups):
            nxt = inverse_steps(groups[gi + 1]) if gi + 1 < len(groups) else iter(())
            nxt_minvs = None
            for c, minv in zip(group, minvs):
                carry_state(c, minv)
                nxt_minvs = next(nxt, nxt_minvs)
            for nxt_minvs in nxt:
                pass
            minvs = nxt_minvs

    def finish():
        o = obuf[...]
        mean = _dot_rx(o, segmean, 2)
        oc = o - mean
        var = _dot_rx(oc * oc, segmean, 2)
        on = oc * lax.rsqrt(var + RWKV_GN_EPS) * lng_ref[...] + lnb_ref[...]
        bonus = _dot_rx(r_all * k2_all * rk_ref[...], segsum, 2) * v_all
        o_ref[...] = ((on + bonus) * g_all).astype(o_ref.dtype)

    return 1, chunk, finish


def _rwkv_mixer(u, bsz, seq, mu, w0, w_up, a0, a_up, g_up, k_k, k_a, r_k, ln_g, ln_b):
    tb = RWKV_TB
    W = GROUP_W
    wup = jnp.zeros((128, W), F32).at[0:32].set(w_up)
    aup = jnp.zeros((128, W), F32).at[32:64].set(a_up)
    gup = jnp.zeros((128, W), F32).at[64:128].set(g_up)
    consts = (mu[None, :3 * W], mu[None, 3 * W:], w0[None, :], wup, a0[None, :], aup, gup,
              k_k[None, :], k_a[None, :], r_k.reshape(1, W), ln_g[None, :], ln_b[None, :])
    return _launch_mixer(
        _rwkv_body, "rwkv_mixer", u, bsz, seq, tb, [(3 * W, COL_RWKV), (128, COL_LORA)], consts,
        [(tb + 8, 3 * W), (tb + 8, 128), (W, W), (tb, W)])


def _outproj_body(x_ref, y0_ref, y1_ref, y2_ref, y3_ref, w_ref, g_ref, b_ref, of_ref, ob_ref):
    W = GROUP_W
    mix = _dot(y0_ref[...], w_ref[0:W, :])
    mix = mix + _dot(y1_ref[...], w_ref[W:2 * W, :])
    mix = mix + _dot(y2_ref[...], w_ref[2 * W:3 * W, :])
    mix = mix + _dot(y3_ref[...], w_ref[3 * W:4 * W, :])
    out = _layer_norm(DEEPNORM_ALPHA * x_ref[...] + mix, g_ref[...], b_ref[...])
    of_ref[...] = out
    ob_ref[...] = out.astype(BF16)


def _out_proj_ln(x, ys, w_bf, g, b, tm=512):
    t, d = x.shape
    row = lambda i: (i, 0)
    const = lambda i: (0, 0)
    return pl.pallas_call(
        _outproj_body,
        grid=(t // tm,),
        in_specs=[pl.BlockSpec((tm, d), row)] + [pl.BlockSpec((tm, GROUP_W), row)] * 4
        + [pl.BlockSpec((d, d), const), pl.BlockSpec((1, d), const), pl.BlockSpec((1, d), const)],
        out_specs=[pl.BlockSpec((tm, d), row), pl.BlockSpec((tm, d), row)],
        out_shape=[jax.ShapeDtypeStruct((t, d), F32), jax.ShapeDtypeStruct((t, d), BF16)],
        compiler_params=_cparams(("arbitrary",)),
        name="out_proj_ln",
    )(x, *ys, w_bf, g[None, :], b[None, :])


def _router_body(x_ref, wt_ref, b_ref, idx_ref, gate_ref, rank_ref, cnt_ref, carry_ref, *, tr):
    i = pl.program_id(0)

    @pl.when(i == 0)
    def _():
        carry_ref[...] = jnp.zeros(carry_ref.shape, F32)

    logits = _pdot(wt_ref[...], x_ref[...], 3, "nt") + b_ref[...][:, 0:1]
    e_iota = _iota((N_EXPERTS, tr), 0)
    work = logits
    onehot = jnp.zeros((N_EXPERTS, tr), F32)
    sels, vals, idxs = [], [], []
    for _ in range(TOP_K):
        m = jnp.max(work, axis=0, keepdims=True)
        idx = jnp.min(jnp.where(work == m, e_iota, N_EXPERTS), axis=0, keepdims=True)
        sel = e_iota == idx
        work = jnp.where(sel, -jnp.inf, work)
        onehot = onehot + sel.astype(F32)
        sels.append(sel)
        vals.append(m)
        idxs.append(idx)
    exps = [jnp.exp(v - vals[0]) for v in vals]
    tot = exps[0] + exps[1] + exps[2] + exps[3]
    upper = (_iota((tr, tr), 0) < _iota((tr, tr), 1)).astype(BF16)
    carry = carry_ref[...][:, 0:1]
    before = _dot(_bf(onehot), upper) + carry
    ranks = [jnp.sum(jnp.where(s, before, 0.0), axis=0, keepdims=True) for s in sels]
    idx_ref[...] = jnp.concatenate(idxs, axis=0)
    gate_ref[...] = jnp.concatenate([e / tot for e in exps], axis=0)
    rank_ref[...] = jnp.concatenate(ranks, axis=0).astype(jnp.int32)
    new_carry = carry + jnp.sum(onehot, axis=1, keepdims=True)
    carry_ref[...] = jnp.broadcast_to(new_carry, carry_ref.shape)
    cnt_ref[...] = jnp.broadcast_to(new_carry, cnt_ref.shape)


def _router(x, w_t, b, tr=512):
    t, d = x.shape
    return pl.pallas_call(
        functools.partial(_router_body, tr=tr),
        grid=(t // tr,),
        in_specs=[pl.BlockSpec((tr, d), lambda i: (i, 0)),
                  pl.BlockSpec((N_EXPERTS, d), lambda i: (0, 0)),
                  pl.BlockSpec((N_EXPERTS, 128), lambda i: (0, 0))],
        out_specs=[pl.BlockSpec((TOP_K, tr), lambda i: (0, i)),
                   pl.BlockSpec((TOP_K, tr), lambda i: (0, i)),
                   pl.BlockSpec((TOP_K, tr), lambda i: (0, i)),
                   pl.BlockSpec((N_EXPERTS, 128), lambda i: (0, 0))],
        out_shape=[jax.ShapeDtypeStruct((TOP_K, t), jnp.int32),
                   jax.ShapeDtypeStruct((TOP_K, t), F32),
                   jax.ShapeDtypeStruct((TOP_K, t), jnp.int32),
                   jax.ShapeDtypeStruct((N_EXPERTS, 128), F32)],
        scratch_shapes=[pltpu.VMEM((N_EXPERTS, 128), F32)],
        compiler_params=_cparams(("arbitrary",)),
        name="router",
    )(x, w_t, jnp.broadcast_to(b[:, None], (N_EXPERTS, 128)))


def _route_plan_body(idx_ref, rank_ref, cnt_ref, dest_ref, be_ref, nv_ref, ps_ref, pl_ref,
                     *, blk, nb_pad):
    E = N_EXPERTS
    cnt = cnt_ref[...][:, 0:1]
    padded = jnp.floor((cnt + (blk - 1)) / blk) * blk
    lower = (_iota((E, E), 1) < _iota((E, E), 0)).astype(F32)
    pstart = jnp.sum(lower * _col_to_row(padded), axis=1, keepdims=True)
    pend = pstart + padded
    ps_ref[...] = jnp.broadcast_to(pstart + cnt, ps_ref.shape).astype(jnp.int32)
    pl_ref[...] = jnp.broadcast_to(padded - cnt, pl_ref.shape).astype(jnp.int32)
    idx = idx_ref[...]
    dest = rank_ref[...].astype(F32)
    for e in range(E):
        dest = dest + jnp.where(idx == e, pstart[e:e + 1, 0:1], 0.0)
    dest_ref[...] = dest.astype(jnp.int32)
    blk_start = (_iota((1, nb_pad), 1) * blk).astype(F32)
    be = jnp.sum((pend <= blk_start).astype(F32), axis=0, keepdims=True)
    be_ref[...] = jnp.minimum(be, E - 1).astype(jnp.int32)
    nv_ref[...] = jnp.broadcast_to(pend[E - 1:E, 0:1] / blk, nv_ref.shape).astype(jnp.int32)


def _route_plan(idx_t, rank_t, counts, blk, nb_pad):
    t = idx_t.shape[1]
    return pl.pallas_call(
        functools.partial(_route_plan_body, blk=blk, nb_pad=nb_pad),
        out_shape=[jax.ShapeDtypeStruct((TOP_K, t), jnp.int32),
                   jax.ShapeDtypeStruct((1, nb_pad), jnp.int32),
                   jax.ShapeDtypeStruct((1, 128), jnp.int32),
                   jax.ShapeDtypeStruct((N_EXPERTS, 128), jnp.int32),
                   jax.ShapeDtypeStruct((N_EXPERTS, 128), jnp.int32)],
        compiler_params=pltpu.CompilerParams(vmem_limit_bytes=VMEM_LIMIT),
        name="route_plan",
    )(idx_t, rank_t, counts)


def _dispatch_body(ps_ref, pl_ref, nv_ref, dest_ref, x_ref, xs_hbm, zbuf, sem, fsem,
                   *, td, blk, n_blocks):
    sizes = [blk >> (s + 1) for s in range(int(np.log2(blk)) - 3)]

    def pad_piece(e, s):
        n = pl_ref[e]
        first = ps_ref[e] + jnp.bitwise_and(n, 7)
        off = pl.multiple_of(first + jnp.bitwise_and(n, -2 * s), 8)
        return (jnp.bitwise_and(n, s) != 0,
                pltpu.make_async_copy(zbuf.at[pl.ds(0, s)], xs_hbm.at[pl.ds(off, s)], fsem))

    def pad_row(e, r):
        return (r < jnp.bitwise_and(pl_ref[e], 7),
                pltpu.make_async_copy(zbuf.at[pl.ds(0, 1)], xs_hbm.at[pl.ds(ps_ref[e] + r, 1)], fsem))

    def tail_copy(b):
        off = pl.multiple_of(b * blk, blk)
        return pltpu.make_async_copy(zbuf, xs_hbm.at[pl.ds(off, blk)], fsem)

    @pl.when(pl.program_id(0) == 0)
    def _():
        zbuf[...] = jnp.zeros(zbuf.shape, zbuf.dtype)

        def pads(start):
            def body(e, carry):
                for cond, cp in [pad_piece(e, s) for s in sizes] + [pad_row(e, r) for r in range(7)]:
                    @pl.when(cond)
                    def _():
                        cp.start() if start else cp.wait()
                return carry
            lax.fori_loop(0, N_EXPERTS, body, 0)

        def tails(start):
            def body(b, carry):
                tail_copy(b).start() if start else tail_copy(b).wait()
                return carry
            lax.fori_loop(nv_ref[0], n_blocks, body, 0)

        pads(True)
        tails(True)
        pads(False)
        tails(False)

    def issue(g, carry):
        for s in range(8):
            for k in range(TOP_K):
                row = dest_ref[g * (8 * TOP_K) + (s * TOP_K + k)]
                pltpu.make_async_copy(x_ref.at[g, pl.ds(s, 1)],
                                      xs_hbm.at[pl.ds(row, 1)], sem).start(priority=k % 2)
        return carry

    lax.fori_loop(0, td // 8, issue, 0)
    for _ in range(TOP_K):
        pltpu.make_async_copy(xs_hbm.at[pl.ds(0, td)], xs_hbm.at[pl.ds(0, td)], sem).wait()


def _dispatch(pad_start, pad_len, nv, dest_flat, x, n_rows, blk, td=512):
    t, d = x.shape
    grid_spec = pltpu.PrefetchScalarGridSpec(
        num_scalar_prefetch=3,
        grid=(t // td,),
        in_specs=[pl.BlockSpec((td * TOP_K,), lambda i, *_: (i,), memory_space=pltpu.SMEM),
                  pl.BlockSpec((td // 8, 8, d), lambda i, *_: (i, 0, 0))],
        out_specs=pl.BlockSpec(memory_space=pl.ANY),
        scratch_shapes=[pltpu.VMEM((blk, d), x.dtype), pltpu.SemaphoreType.DMA(()),
                        pltpu.SemaphoreType.DMA(())],
    )
    return pl.pallas_call(
        functools.partial(_dispatch_body, td=td, blk=blk, n_blocks=n_rows // blk),
        grid_spec=grid_spec,
        out_shape=jax.ShapeDtypeStruct((n_rows, d), x.dtype),
        compiler_params=_cparams(("arbitrary",)),
        name="dispatch",
    )(pad_start, pad_len, nv, dest_flat, x.reshape(t // 8, 8, d))


def _expert_body(be_ref, nv_ref, xs_ref, wgu_ref, bgu_ref, wd_ref, bd_ref, ys_ref, wgu_bf, wd_bf):
    b = pl.program_id(0)
    d = D_MODEL

    @pl.when(b < nv_ref[0])
    def _():
        prev = be_ref[jnp.maximum(b - 1, 0)]

        @pl.when((b == 0) | (be_ref[b] != prev))
        def _():
            wgu_bf[...] = wgu_ref[...].astype(BF16)
            wd_bf[...] = wd_ref[...].astype(BF16)

        h = _dot(_bf(xs_ref[...]), wgu_bf[...]) + bgu_ref[...]
        hg = jnp.minimum(h[:, 0:d], SWIGLU_LIMIT)
        hl = jnp.clip(h[:, d:2 * d], -SWIGLU_LIMIT, SWIGLU_LIMIT)
        act = (hl + 1.0) * (hg * _sigmoid(hg * SWIGLU_ALPHA))
        ys_ref[...] = _dot(_bf(act), wd_bf[...]) + bd_ref[...]

    @pl.when(b >= nv_ref[0])
    def _():
        ys_ref[...] = jnp.zeros(ys_ref.shape, F32)


def _experts(be, nv, xs, w_gu, b_gu, w_down, b_down, layer, blk):
    n_rows, d = xs.shape
    nb = n_rows // blk

    def row(b, be_r, nv_r):
        return (jnp.minimum(b, nv_r[0] - 1), 0)

    def wsel(b, be_r, nv_r):
        return (layer, be_r[jnp.minimum(b, nv_r[0] - 1)], 0, 0)

    grid_spec = pltpu.PrefetchScalarGridSpec(
        num_scalar_prefetch=2,
        grid=(nb,),
        in_specs=[pl.BlockSpec((blk, d), row),
                  pl.BlockSpec((None, None, d, 2 * d), wsel),
                  pl.BlockSpec((None, None, 1, 2 * d), wsel),
                  pl.BlockSpec((None, None, d, d), wsel),
                  pl.BlockSpec((None, None, 1, d), wsel)],
        out_specs=pl.BlockSpec((blk, d), lambda b, be_r, nv_r: (b, 0)),
        scratch_shapes=[pltpu.VMEM((d, 2 * d), BF16), pltpu.VMEM((d, d), BF16)],
    )
    return pl.pallas_call(
        _expert_body,
        grid_spec=grid_spec,
        out_shape=jax.ShapeDtypeStruct((n_rows, d), F32),
        compiler_params=_cparams(("arbitrary",)),
        name="experts",
    )(be, nv, xs, w_gu, b_gu, w_down, b_down)


def _combine_body(dcur_ref, dnext_ref, ys_hbm, x_ref, gate_ref, p_ref, wg_ref, wp_ref, g_ref, b_ref,
                  of_ref, ob_ref, ybuf, sem, *, tc):
    i = pl.program_id(0)
    n = pl.num_programs(0)
    slot = lax.rem(i, 2)

    def issue(d_ref, s):
        def body(g, carry):
            for r in range(8):
                for k in range(TOP_K):
                    row = d_ref[g * (8 * TOP_K) + (r * TOP_K + k)]
                    pltpu.make_async_copy(ys_hbm.at[pl.ds(row, 1)], ybuf.at[s, k, g, pl.ds(r, 1)],
                                          sem.at[s]).start(priority=k % 2)
            return carry
        lax.fori_loop(0, tc // 8, body, 0)

    @pl.when(i == 0)
    def _():
        issue(dcur_ref, 0)

    for s in range(2):
        @pl.when((i + 1 < n) & (slot == 1 - s))
        def _():
            issue(dnext_ref, s)

    for k in range(TOP_K):
        pltpu.make_async_copy(ys_hbm.at[pl.ds(0, tc)], ys_hbm.at[pl.ds(0, tc)], sem.at[slot]).wait()

    gate = gate_ref[...]
    d = x_ref.shape[-1]
    ffn = ybuf[slot, 0].reshape(tc, d) * gate[:, 0:1]
    for k in range(1, TOP_K):
        ffn = ffn + ybuf[slot, k].reshape(tc, d) * gate[:, k:k + 1]
    h = DEEPNORM_ALPHA * x_ref[...] + ffn
    ple = _dot(_bf(p_ref[...]), wp_ref[...])
    h = h + _sigmoid(_dot(_bf(h), wg_ref[...])) * ple
    out = _layer_norm(h, g_ref[...], b_ref[...])
    of_ref[...] = out
    ob_ref[...] = out.astype(BF16)


def _combine(dest_flat, ys, x, gate, p, wg_bf, wp_bf, g, b, layer, tc=256):
    t, d = x.shape
    n = t // tc
    row = lambda i: (i, 0)
    const = lambda i: (0, 0)
    return pl.pallas_call(
        functools.partial(_combine_body, tc=tc),
        grid=(n,),
        in_specs=[pl.BlockSpec((tc * TOP_K,), lambda i: (i,), memory_space=pltpu.SMEM),
                  pl.BlockSpec((tc * TOP_K,), lambda i: (jnp.minimum(i + 1, n - 1),),
                               memory_space=pltpu.SMEM),
                  pl.BlockSpec(memory_space=pl.ANY),
                  pl.BlockSpec((tc, d), row),
                  pl.BlockSpec((tc, TOP_K), row),
                  pl.BlockSpec((None, tc, PLE_DIM), lambda i: (layer, i, 0)),
                  pl.BlockSpec((d, d), const),
                  pl.BlockSpec((PLE_DIM, d), const),
                  pl.BlockSpec((1, d), const),
                  pl.BlockSpec((1, d), const)],
        out_specs=[pl.BlockSpec((tc, d), row), pl.BlockSpec((tc, d), row)],
        out_shape=[jax.ShapeDtypeStruct((t, d), F32), jax.ShapeDtypeStruct((t, d), BF16)],
        scratch_shapes=[pltpu.VMEM((2, TOP_K, tc // 8, 8, d), F32), pltpu.SemaphoreType.DMA((2,))],
        compiler_params=_cparams(("arbitrary",)),
        name="combine",
    )(dest_flat, dest_flat, ys, x, gate, p, wg_bf, wp_bf, g[None, :], b[None, :])


def kernel(x, p, w_in, w_out, ln1_g, ln1_b, ssd_conv_w, ssd_conv_b, ssd_a_log, ssd_dt_bias, ssd_d, ssd_norm_g, rwkv_mu, rwkv_w0, rwkv_w_up, rwkv_a0, rwkv_a_up, rwkv_g_up, rwkv_k_k, rwkv_k_a, rwkv_r_k, rwkv_ln_g, rwkv_ln_b, gla_gate_up, gla_gate_b, gla_norm_g, mlstm_conv_w, mlstm_conv_b, mlstm_i_b, mlstm_f_b, mlstm_norm_g, router_w, router_b, exp_w_gu, exp_b_gu, exp_w_down, exp_b_down, ple_gate_w, ple_proj, ln2_g, ln2_b):
    bsz, seq, d = x.shape
    t = bsz * seq
    depth = w_in.shape[0]
    blk = EXPERT_BLK
    n_blocks = -(-(t * TOP_K + N_EXPERTS * (blk - 1)) // blk)
    n_rows = n_blocks * blk
    nb_pad = -(-n_blocks // 128) * 128

    xf = x.reshape(t, d)
    xb = xf.astype(BF16)
    p2 = p.reshape(depth, t, PLE_DIM)
    b_gu4 = exp_b_gu[:, :, None, :]
    b_down4 = exp_b_down[:, :, None, :]
    w_packed = _pack_w_in(w_in)
    for i in range(depth):
        u = _in_proj(xb, w_packed, i)
        y_ssd = _ssd_mixer(u, bsz, seq, ssd_conv_w[i], ssd_conv_b[i], ssd_a_log[i], ssd_dt_bias[i],
                           ssd_d[i], ssd_norm_g[i])
        y_rwkv = _rwkv_mixer(u, bsz, seq, rwkv_mu[i], rwkv_w0[i], rwkv_w_up[i], rwkv_a0[i],
                             rwkv_a_up[i], rwkv_g_up[i], rwkv_k_k[i], rwkv_k_a[i], rwkv_r_k[i],
                             rwkv_ln_g[i], rwkv_ln_b[i])
        y_gla = _gla_mixer(u, bsz, seq, gla_gate_up[i], gla_gate_b[i], gla_norm_g[i])
        y_ml = _mlstm_mixer(u, bsz, seq, mlstm_conv_w[i], mlstm_conv_b[i], mlstm_i_b[i],
                            mlstm_f_b[i], mlstm_norm_g[i])
        x1f, _ = _out_proj_ln(xf, (y_ssd, y_rwkv, y_gla, y_ml), w_out[i].astype(BF16),
                              ln1_g[i], ln1_b[i])
        idx_t, gate_t, rank_t, counts = _router(x1f, router_w[i].T, router_b[i])
        dest_t, be, nv, pad_start, pad_len = _route_plan(idx_t, rank_t, counts, blk, nb_pad)
        nv = nv.reshape(128)
        dest_flat = dest_t.T.reshape(t * TOP_K)
        xs = _dispatch(pad_start[:, 0], pad_len[:, 0], nv, dest_flat, x1f, n_rows, blk)
        ys = _experts(be.reshape(nb_pad), nv, xs, exp_w_gu, b_gu4, exp_w_down, b_down4, i, blk)
        xf, xb = _combine(dest_flat, ys, x1f, gate_t.T, p2, ple_gate_w[i].astype(BF16),
                          ple_proj[i].astype(BF16), ln2_g[i], ln2_b[i], i)
    return xf.reshape(bsz, seq, d)
```

```python
import functools

import jax
import jax.numpy as jnp
import numpy as np
from jax import lax
from jax.experimental import pallas as pl
from jax.experimental.pallas import tpu as pltpu

F32 = jnp.float32
BF16 = jnp.bfloat16
HI = lax.Precision.HIGHEST

D_MODEL = 1024
DEPTH = 4
GROUP_W = 256
N_HEADS = 4
HEAD_DIM = 64
SSD_STATE = 128
SSD_CHUNK = 128
GLA_DK = 128
GLA_HEAD_K = 32
GLA_TAU = 16.0
CHUNK = 64
RWKV_P_SCORE = 1
RWKV_P_INV = 1
RWKV_P_RHS = 1
RWKV_P_OUT = 1
RWKV_P_STATE = 1
RWKV_GN_EPS = 64e-5
NORM_EPS = 1e-5
LN_EPS = 1e-5
N_EXPERTS = 32
TOP_K = 4
SWIGLU_LIMIT = 7.0
SWIGLU_ALPHA = 1.702
PLE_DIM = 256
DEEPNORM_ALPHA = (2 * DEPTH) ** 0.25

U_WIDTH = 3584
COL_SSD_Z, COL_SSD_X, COL_SSD_BC = 0, 256, 512
COL_RWKV = 768
COL_ML_QK = 1536
COL_ML_VO = 2048
COL_GLA_QK, COL_GLA_V, COL_GLA_OG = 2560, 2816, 3072
COL_LORA = 3328
COL_MISC = 3456
MISC_DT, MISC_GLA, MISC_I, MISC_F = 0, 4, 20, 24

SSD_TB = GLA_TB = RWKV_TB = 512
MLSTM_TB = 256
RWKV_GROUP = 4
MIXER_NB = 1
EXPERT_BLK = 512
VMEM_LIMIT = 56 * 1024 * 1024


def _cparams(sem):
    return pltpu.CompilerParams(dimension_semantics=sem, vmem_limit_bytes=VMEM_LIMIT)


def _dot(a, b, prec=None):
    return lax.dot_general(a, b, (((1,), (0,)), ((), ())), precision=prec,
                           preferred_element_type=F32)


def _dot_nt(a, b, prec=None):
    return lax.dot_general(a, b, (((1,), (1,)), ((), ())), precision=prec,
                           preferred_element_type=F32)


def _dot_tn(a, b, prec=None):
    return lax.dot_general(a, b, (((0,), (0,)), ((), ())), precision=prec,
                           preferred_element_type=F32)


def _bf(x):
    return x.astype(BF16)


_DIMS = {"nn": (((1,), (0,)), ((), ())), "nt": (((1,), (1,)), ((), ())),
         "tn": (((0,), (0,)), ((), ()))}


def _split_bf(x):
    hi = x.astype(BF16)
    return hi, (x - hi.astype(F32)).astype(BF16)


def _pdot(a, b, passes, kind="nn"):
    dn = _DIMS[kind]
    if passes == 6:
        return lax.dot_general(a, b, dn, precision=HI, preferred_element_type=F32)
    dg = lambda x, y: lax.dot_general(x, y, dn, preferred_element_type=F32)
    if passes == 1:
        return dg(_bf(a), _bf(b))
    ah, al = _split_bf(a)
    bh, bl = _split_bf(b)
    return dg(ah, bh) + (dg(ah, bl) + dg(al, bh))


def _pieces(x, n):
    out, rem = [], x
    for _ in range(n):
        part = rem.astype(BF16)
        out.append(part)
        rem = rem - part.astype(F32)
    return out


def _dot_lx(a_exact, b, n):
    ab = _bf(a_exact)
    acc = None
    for part in _pieces(b, n):
        term = lax.dot_general(ab, part, _DIMS["nn"], preferred_element_type=F32)
        acc = term if acc is None else acc + term
    return acc


def _dot_rx(a, b_exact, n):
    bb = _bf(b_exact)
    acc = None
    for part in _pieces(a, n):
        term = lax.dot_general(part, bb, _DIMS["nn"], preferred_element_type=F32)
        acc = term if acc is None else acc + term
    return acc


def _sigmoid(x):
    return 1.0 / (1.0 + jnp.exp(-x))


def _silu(x):
    return x * _sigmoid(x)


def _softplus(x):
    return jnp.maximum(x, 0.0) + jnp.log1p(jnp.exp(-jnp.abs(x)))


def _log_sigmoid(x):
    return jnp.minimum(x, 0.0) - jnp.log1p(jnp.exp(-jnp.abs(x)))


def _iota(shape, dim):
    return lax.broadcasted_iota(jnp.int32, shape, dim)


def _tri_incl(n):
    return (_iota((n, n), 0) >= _iota((n, n), 1)).astype(F32)


def _seg_matrix(n, seg, value):
    sh = int(np.log2(seg))
    same = (lax.shift_right_logical(_iota((n, n), 0), sh)
            == lax.shift_right_logical(_iota((n, n), 1), sh))
    return jnp.where(same, value, 0.0).astype(F32)


def _head_mask(width, seg, h):
    lane = _iota((1, width), 1)
    return ((lane >= h * seg) & (lane < (h + 1) * seg)).astype(F32)


def _expand(cols, seg):
    n = len(cols)
    rows = cols[0].shape[0]
    width = n * seg
    lane = _iota((rows, width), 1)
    out = jnp.broadcast_to(cols[n - 1], (rows, width))
    for h in range(n - 2, -1, -1):
        out = jnp.where(lane < (h + 1) * seg, jnp.broadcast_to(cols[h], (rows, width)), out)
    return out


def _col_to_row(col):
    n = col.shape[0]
    eye = _iota((n, n), 0) == _iota((n, n), 1)
    return jnp.sum(jnp.where(eye, col, 0.0), axis=0, keepdims=True)


def _layer_norm(x, g, b):
    mu = jnp.mean(x, axis=-1, keepdims=True)
    xc = x - mu
    var = jnp.mean(xc * xc, axis=-1, keepdims=True)
    return xc * lax.rsqrt(var + LN_EPS) * g + b


def _causal_conv_silu(buf, w_ref, b_ref, rows):
    acc = b_ref[...]
    for k in range(4):
        acc = acc + buf[pl.ds(5 + k, rows), :] * w_ref[k:k + 1, :]
    return _silu(acc)


_W_IN_COLS = 3484
_W_RWKV, _W_GLA, _W_ML = 772, 772 + 896, 772 + 896 + 784
_PACK_SEGMENTS = (
    (COL_SSD_Z, 0, 768),
    (COL_RWKV, _W_RWKV, 768),
    (COL_ML_QK, _W_ML, 768),
    (COL_ML_VO + 256, _W_ML + 776, 256),
    (COL_GLA_QK, _W_GLA, 512),
    (COL_GLA_OG, _W_GLA + 528, 256),
    (COL_LORA, _W_RWKV + 768, 128),
)
_PACK_MISC = ((768, MISC_DT, 4), (_W_GLA + 512, MISC_GLA, 16), (_W_ML + 768, MISC_I, 8))


def _pack_body(w_ref, o_ref):
    def cols(src, n):
        sh = src % 128
        a0 = src - sh
        if sh == 0:
            return w_ref[:, a0:a0 + n]
        wd = -(-(sh + n) // 128) * 128
        return pltpu.roll(w_ref[:, a0:a0 + wd], wd - sh, axis=1)[:, 0:n]

    for dst, src, n in _PACK_SEGMENTS:
        o_ref[:, dst:dst + n] = cols(src, n).astype(BF16)
    lane = _iota((w_ref.shape[0], 128), 1)
    misc = jnp.zeros((w_ref.shape[0], 128), F32)
    for src, lane0, n in _PACK_MISC:
        assert src % 128 == lane0
        a0 = src - lane0
        misc = jnp.where((lane >= lane0) & (lane < lane0 + n), w_ref[:, a0:a0 + 128], misc)
    o_ref[:, COL_MISC:COL_MISC + 128] = misc.astype(BF16)


def _pack_w_in(w_in, rb=256):
    depth, d, n = w_in.shape
    assert n == _W_IN_COLS
    return pl.pallas_call(
        _pack_body,
        grid=(depth, d // rb),
        in_specs=[pl.BlockSpec((None, rb, U_WIDTH), lambda l, r: (l, r, 0))],
        out_specs=pl.BlockSpec((None, rb, U_WIDTH), lambda l, r: (l, r, 0)),
        out_shape=jax.ShapeDtypeStruct((depth, d, U_WIDTH), BF16),
        compiler_params=_cparams(("arbitrary", "arbitrary")),
        name="pack_w_in",
    )(w_in)


def _in_proj_body(x_ref, w_ref, o_ref):
    o_ref[...] = jnp.dot(x_ref[...], w_ref[...], preferred_element_type=F32)


def _in_proj(x_bf, w_all, layer, tm=512, tn=1792):
    t, k = x_bf.shape
    n = w_all.shape[2]
    return pl.pallas_call(
        _in_proj_body,
        grid=(n // tn, t // tm),
        in_specs=[pl.BlockSpec((tm, k), lambda j, i: (i, 0)),
                  pl.BlockSpec((None, k, tn), lambda j, i: (layer, 0, j))],
        out_specs=pl.BlockSpec((tm, tn), lambda j, i: (i, j)),
        out_shape=jax.ShapeDtypeStruct((t, n), F32),
        compiler_params=_cparams(("arbitrary", "arbitrary")),
        name="in_proj",
    )(x_bf, w_all)


def _ssd_body(z_ref, x_ref, bc_ref, misc_ref, cwx_ref, cwbc_ref, cbx_ref, cbbc_ref,
              alog_ref, dtb_ref, dskip_ref, ng_ref, o_ref, xbuf, bcbuf, st_ref, *, tb):
    xbuf[8:8 + tb, :] = x_ref[...]
    bcbuf[8:8 + tb, :] = bc_ref[...]
    xs_all = _causal_conv_silu(xbuf, cwx_ref, cbx_ref, tb)
    bc_all = _causal_conv_silu(bcbuf, cwbc_ref, cbbc_ref, tb)
    xbuf[0:8, :] = xbuf[tb:tb + 8, :]
    bcbuf[0:8, :] = bcbuf[tb:tb + 8, :]

    dt_all = _softplus(misc_ref[...] + dtb_ref[...])
    adt_all = dt_all * (-jnp.exp(alog_ref[...]))
    z_all = z_ref[...]
    L = SSD_CHUNK
    tri = _tri_incl(L)
    causal = _iota((L, L), 0) >= _iota((L, L), 1)
    masks = [_head_mask(GROUP_W, HEAD_DIM, h) for h in range(N_HEADS)]

    def chunk(c):
        rows = slice(c * L, (c + 1) * L)
        xs = xs_all[rows]
        bm = _bf(bc_all[rows, 0:SSD_STATE])
        cm = _bf(bc_all[rows, SSD_STATE:2 * SSD_STATE])
        dt = dt_all[rows]
        acum = _dot_lx(tri, adt_all[rows], 3)
        cols = [acum[:, h:h + 1] for h in range(N_HEADS)]
        lasts = [acum[L - 1:L, h:h + 1] for h in range(N_HEADS)]
        xc = xs * _expand([dt[:, h:h + 1] for h in range(N_HEADS)], HEAD_DIM)
        g = _dot_nt(cm, bm)
        y = jnp.zeros((L, GROUP_W), F32)
        for h in range(N_HEADS):
            lmat = jnp.exp(jnp.where(causal, cols[h] - _col_to_row(cols[h]), -jnp.inf))
            y = y + _dot(_bf(g * lmat), _bf(xc * masks[h]))
        dec_states = _expand([jnp.exp(lasts[h] - cols[h]) for h in range(N_HEADS)], HEAD_DIM)
        st_prev = st_ref[...]
        y = y + _dot(cm, _bf(st_prev)) * _expand([jnp.exp(cols[h]) for h in range(N_HEADS)], HEAD_DIM)
        st_new = _dot_tn(bm, _bf(xc * dec_states))
        st_ref[...] = st_prev * _expand([jnp.exp(lasts[h]) for h in range(N_HEADS)], HEAD_DIM) + st_new
        y = y + xs * dskip_ref[...]
        y = y * _silu(z_all[rows])
        ms = jnp.mean(y * y, axis=-1, keepdims=True)
        o_ref[rows, :] = (y * lax.rsqrt(ms + NORM_EPS) * ng_ref[...]).astype(o_ref.dtype)

    return tb // L, chunk, lambda: None


def _launch_mixer(body, name, u, bsz, seq, tb, u_blocks, consts, scratch):
    nbb = MIXER_NB
    u3 = u.reshape(bsz, seq, u.shape[-1])
    n_in, n_c = len(u_blocks), len(consts)

    n_s = len(scratch)

    def batched(*refs):
        ins, cs = refs[:n_in], refs[n_in:n_in + n_c]
        out, scr = refs[n_in + n_c], refs[n_in + n_c + 1:]

        @pl.when(pl.program_id(1) == 0)
        def _():
            for r in scr:
                r[...] = jnp.zeros(r.shape, F32)

        seqs = [body(*[r.at[bb] for r in ins], *cs, out.at[bb], *scr[bb * n_s:(bb + 1) * n_s], tb=tb)
                for bb in range(nbb)]
        for c in range(seqs[0][0]):
            for _, chunk, _ in seqs:
                chunk(c)
        for _, _, finish in seqs:
            finish()

    in_specs = [pl.BlockSpec((nbb, tb, w), functools.partial(lambda g, j, cb: (g, j, cb), cb=c // w))
                for w, c in u_blocks]
    in_specs += [pl.BlockSpec(a.shape, functools.partial(lambda g, j, nd: (0,) * nd, nd=a.ndim))
                 for a in consts]
    out = pl.pallas_call(
        batched,
        grid=(bsz // nbb, seq // tb),
        in_specs=in_specs,
        out_specs=pl.BlockSpec((nbb, tb, GROUP_W), lambda g, j: (g, j, 0)),
        out_shape=jax.ShapeDtypeStruct((bsz, seq, GROUP_W), BF16),
        scratch_shapes=[pltpu.VMEM(s, F32) for _ in range(nbb) for s in scratch],
        compiler_params=_cparams(("arbitrary", "arbitrary")),
        name=name,
    )(*([u3] * n_in), *consts)
    return out.reshape(bsz * seq, GROUP_W)


def _ssd_mixer(u, bsz, seq, conv_w, conv_b, a_log, dt_bias, d_skip, norm_g):
    tb = SSD_TB
    pad4 = lambda v: jnp.zeros((1, 128), F32).at[0, :N_HEADS].set(v)
    consts = (conv_w[:, :GROUP_W], conv_w[:, GROUP_W:],
              conv_b[None, :GROUP_W], conv_b[None, GROUP_W:],
              pad4(a_log), pad4(dt_bias),
              jnp.repeat(d_skip, HEAD_DIM)[None, :], norm_g[None, :])
    return _launch_mixer(
        _ssd_body, "ssd_mixer", u, bsz, seq, tb,
        [(GROUP_W, COL_SSD_Z), (GROUP_W, COL_SSD_X), (GROUP_W, COL_SSD_BC), (128, COL_MISC)],
        consts, [(tb + 8, GROUP_W), (tb + 8, GROUP_W), (SSD_STATE, GROUP_W)])


def _gla_body(qk_ref, v_ref, og_ref, misc_ref, gup_ref, gb_ref, ng_ref, o_ref, st_ref, obuf,
              *, tb):
    L = CHUNK
    q_all = qk_ref[:, 0:GLA_DK] * (GLA_HEAD_K ** -0.5)
    k_all = qk_ref[:, GLA_DK:2 * GLA_DK]
    v_all = v_ref[...]
    og_all = og_ref[...]
    gpre = _dot(_bf(misc_ref[...]), gup_ref[...]) + gb_ref[...]
    loga_all = _log_sigmoid(gpre) / GLA_TAU
    tri = _tri_incl(L)
    causal = _iota((L, L), 0) >= _iota((L, L), 1)
    kmasks = [_head_mask(GLA_DK, GLA_HEAD_K, h) for h in range(N_HEADS)]
    vmasks = [_head_mask(GROUP_W, HEAD_DIM, h) for h in range(N_HEADS)]
    bd = (lax.shift_right_logical(_iota((GROUP_W, GLA_DK), 0), 6)
          == lax.shift_right_logical(_iota((GROUP_W, GLA_DK), 1), 5))
    segmean = _seg_matrix(GROUP_W, HEAD_DIM, 1.0 / HEAD_DIM)

    def chunk(c):
        rows = slice(c * L, (c + 1) * L)
        bcum = _dot_lx(tri, loga_all[rows], 3)
        b_last = bcum[L - 1:L, :]
        q_dec = q_all[rows] * jnp.exp(bcum)
        k_inv = _bf(k_all[rows] * jnp.exp(-bcum))
        k_dec = _bf(k_all[rows] * jnp.exp(b_last - bcum))
        v = v_all[rows]
        st_prev = st_ref[...]
        o = _dot_nt(_bf(q_dec), _bf(st_prev))
        for h in range(N_HEADS):
            attn = jnp.where(causal, _dot_nt(_bf(q_dec * kmasks[h]), k_inv), 0.0)
            o = o + _dot(_bf(attn), _bf(v * vmasks[h]))
        contrib = jnp.where(bd, _dot_tn(_bf(v), k_dec), 0.0)
        st_ref[...] = st_prev * jnp.exp(b_last) + contrib
        obuf[rows, :] = o

    def finish():
        o = obuf[...]
        ms = _dot_rx(o * o, segmean, 2)
        out = o * lax.rsqrt(ms + NORM_EPS) * ng_ref[...] * _silu(og_all)
        o_ref[...] = out.astype(o_ref.dtype)

    return tb // L, chunk, finish


def _gla_mixer(u, bsz, seq, gate_up, gate_b, norm_g):
    gup = jnp.zeros((128, GLA_DK), F32).at[MISC_GLA:MISC_GLA + 16].set(gate_up).astype(BF16)
    return _launch_mixer(
        _gla_body, "gla_mixer", u, bsz, seq, GLA_TB,
        [(GROUP_W, COL_GLA_QK), (GROUP_W, COL_GLA_V), (GROUP_W, COL_GLA_OG), (128, COL_MISC)],
        (gup, gate_b[None, :], norm_g[None, :]),
        [(GROUP_W, GLA_DK), (GLA_TB, GROUP_W)])


def _mlstm_body(qk_ref, vo_ref, misc_ref, cw_ref, cb_ref, ib_ref, fb_ref, ng_ref, o_ref,
                qkbuf, c_ref, n_ref, m_ref, fin, *, tb):
    W = GROUP_W
    qkbuf[8:8 + tb, :] = qk_ref[...]
    qk = _causal_conv_silu(qkbuf, cw_ref, cb_ref, tb)
    qkbuf[0:8, :] = qkbuf[tb:tb + 8, :]
    q_all = qk[:, 0:W] * (HEAD_DIM ** -0.5)
    k_all = qk[:, W:2 * W]
    v_all = vo_ref[:, 0:W]
    og_all = vo_ref[:, W:2 * W]
    misc = misc_ref[...]
    ipre_all = misc + ib_ref[...]
    lf_all = _log_sigmoid(misc + fb_ref[...])

    L = CHUNK
    tri = _tri_incl(L)
    causal = _iota((L, L), 0) >= _iota((L, L), 1)
    masks = [_head_mask(W, HEAD_DIM, h) for h in range(N_HEADS)]
    bd = _seg_matrix(W, HEAD_DIM, 1.0)
    segmean = _seg_matrix(W, HEAD_DIM, 1.0 / HEAD_DIM)

    def prepare(c):
        rows = slice(c * L, (c + 1) * L)
        q, k, v = q_all[rows], k_all[rows], v_all[rows]
        kb = _bf(k)
        bcs = _dot_lx(tri, lf_all[rows], 3)
        ipre = ipre_all[rows]
        num = jnp.zeros((L, W), F32)
        heads, w_st = [], []
        for h in range(N_HEADS):
            b_col = bcs[:, MISC_F + h:MISC_F + h + 1]
            i_col = ipre[:, MISC_I + h:MISC_I + h + 1]
            b_last = bcs[L - 1:L, MISC_F + h:MISC_F + h + 1]
            dmat = jnp.where(causal, b_col - _col_to_row(b_col) + _col_to_row(i_col), -jnp.inf)
            m_dmat = jnp.max(dmat, axis=-1, keepdims=True)
            a_st = b_last - b_col + i_col
            m_loc = jnp.max(a_st, axis=0, keepdims=True)
            w_st.append(jnp.exp(a_st - m_loc))
            scores = _dot_nt(_bf(q * masks[h]), kb) * jnp.exp(dmat - m_dmat)
            num = num + _dot(_bf(scores), _bf(v * masks[h]))
            heads.append(dict(b_col=b_col, b_last=b_last, m_loc=m_loc, m_dmat=m_dmat,
                              rowsum=jnp.sum(scores, axis=-1, keepdims=True)))
        wst = _expand(w_st, HEAD_DIM)
        return dict(rows=rows, q=q, num=num, heads=heads,
                    c_new=bd * _dot_tn(_bf(v * wst), kb),
                    n_new=jnp.sum(k * wst, axis=0, keepdims=True))

    def chunk(_):
        for p in [prepare(c) for c in range(tb // L)]:
            rows, q = p["rows"], p["q"]
            m_state = m_ref[...]
            w_inter, corr, rowsum, eneg, s_old, s_new, m_new = [], [], [], [], [], [], []
            for h, hd in enumerate(p["heads"]):
                m_prev = m_state[:, h:h + 1]
                mn = jnp.maximum(hd["b_last"] + m_prev, hd["m_loc"])
                m_new.append(mn)
                s_old.append(jnp.exp(hd["b_last"] + m_prev - mn))
                s_new.append(jnp.exp(hd["m_loc"] - mn))
                m_inter = hd["b_col"] + m_prev
                m_row = jnp.maximum(m_inter, hd["m_dmat"])
                w_inter.append(jnp.exp(m_inter - m_row))
                cr = jnp.exp(hd["m_dmat"] - m_row)
                corr.append(cr)
                rowsum.append(cr * hd["rowsum"])
                eneg.append(jnp.exp(-m_row))
            c_prev = c_ref[...]
            n_prev = n_ref[...]
            wi = _expand(w_inter, HEAD_DIM)
            fin[0, rows, :] = _expand(corr, HEAD_DIM) * p["num"] + wi * _dot_nt(_bf(q), _bf(c_prev))
            fin[1, rows, :] = wi
            fin[2, rows, :] = q * n_prev
            fin[3, rows, :] = _expand(rowsum, HEAD_DIM)
            fin[4, rows, :] = _expand(eneg, HEAD_DIM)
            so = _expand(s_old, HEAD_DIM)
            sn = _expand(s_new, HEAD_DIM)
            c_ref[...] = so * c_prev + sn * p["c_new"]
            n_ref[...] = so * n_prev + sn * p["n_new"]
            lane = _iota((1, 128), 1)
            m_vec = jnp.zeros((1, 128), F32)
            for h in range(N_HEADS):
                m_vec = jnp.where(lane == h, m_new[h], m_vec)
            m_ref[...] = m_vec

    def finish():
        den = fin[1] * _dot_rx(fin[2], bd, 3) + fin[3]
        hval = fin[0] / jnp.maximum(jnp.abs(den), fin[4])
        hv = hval * _sigmoid(og_all)
        mean = _dot_rx(hv, segmean, 2)
        xc = hv - mean
        var = _dot_rx(xc * xc, segmean, 2)
        o_ref[...] = (xc * lax.rsqrt(var + NORM_EPS) * ng_ref[...]).astype(o_ref.dtype)

    return 1, chunk, finish


def _mlstm_mixer(u, bsz, seq, conv_w, conv_b, i_b, f_b, norm_g):
    tb = MLSTM_TB
    ib = jnp.zeros((1, 128), F32).at[0, MISC_I:MISC_I + N_HEADS].set(i_b)
    fb = jnp.zeros((1, 128), F32).at[0, MISC_F:MISC_F + N_HEADS].set(f_b)
    return _launch_mixer(
        _mlstm_body, "mlstm_mixer", u, bsz, seq, tb,
        [(2 * GROUP_W, COL_ML_QK), (2 * GROUP_W, COL_ML_VO), (128, COL_MISC)],
        (conv_w, conv_b[None, :], ib, fb, norm_g[None, :]),
        [(tb + 8, 2 * GROUP_W), (GROUP_W, GROUP_W), (1, GROUP_W), (1, 128), (5, tb, GROUP_W)])


def _stack_heads(x, masks):
    return jnp.concatenate([x * m for m in masks], axis=0)


def _rwkv_body(rkv_ref, lora_ref, mu_rkv_ref, mu_lora_ref, w0_ref, wup_ref, a0_ref, aup_ref,
               gup_ref, kk_ref, ka_ref, rk_ref, lng_ref, lnb_ref, o_ref,
               rbuf, lbuf, s_ref, obuf, *, tb):
    W = GROUP_W
    rbuf[8:8 + tb, :] = rkv_ref[...]
    lbuf[8:8 + tb, :] = lora_ref[...]
    rkv = rkv_ref[...]
    lora = lora_ref[...]
    rkv = rkv + (rbuf[pl.ds(7, tb), :] - rkv) * mu_rkv_ref[...]
    lora = lora + (lbuf[pl.ds(7, tb), :] - lora) * mu_lora_ref[...]
    rbuf[0:8, :] = rbuf[tb:tb + 8, :]
    lbuf[0:8, :] = lbuf[tb:tb + 8, :]

    r_all, k_all, v_all = rkv[:, 0:W], rkv[:, W:2 * W], rkv[:, 2 * W:3 * W]
    wpre = w0_ref[...] + _pdot(jnp.tanh(lora), wup_ref[...], 3)
    lw_all = -jnp.exp(-_softplus(-wpre) - 0.5)
    a_all = _sigmoid(a0_ref[...] + _pdot(lora, aup_ref[...], 3))
    g_all = _pdot(_sigmoid(lora), gup_ref[...], 3)
    segsum = _seg_matrix(W, HEAD_DIM, 1.0)
    segmean = _seg_matrix(W, HEAD_DIM, 1.0 / HEAD_DIM)
    kk = k_all * kk_ref[...]
    kk = kk / jnp.maximum(jnp.sqrt(_dot_rx(kk * kk, segsum, 2)), 1e-12)
    k2_all = k_all * (1.0 + (a_all - 1.0) * ka_ref[...])
    av_all = -kk
    bv_all = kk * a_all

    L = CHUNK
    HL = N_HEADS * L
    tri = _tri_incl(L)
    masks = [_head_mask(W, HEAD_DIM, h) for h in range(N_HEADS)]
    t_idx = _iota((L, HL), 0)
    s_idx = jnp.bitwise_and(_iota((L, HL), 1), L - 1)
    strict = s_idx < t_idx
    incl = s_idx <= t_idx
    eye = (_iota((HL, HL), 0) == _iota((HL, HL), 1)).astype(F32)

    def prepare(c):
        rows = slice(c * L, (c + 1) * L)
        lw = lw_all[rows]
        cum = _dot_lx(tri, lw, 3)
        cum_last = cum[L - 1:L, :]
        w_inv = jnp.exp(-cum)
        w_dec = jnp.exp(cum_last - cum)
        k2, v = k2_all[rows], v_all[rows]
        rt = r_all[rows] * jnp.exp(cum)
        at = av_all[rows] * jnp.exp(cum - lw)
        bt_bd = _stack_heads(bv_all[rows] * w_inv, masks)
        kt_bd = _stack_heads(k2 * w_inv, masks)
        v_bd = _stack_heads(v, masks)
        sc = _pdot(jnp.concatenate([at, rt], axis=0), jnp.concatenate([bt_bd, kt_bd], axis=0),
                   RWKV_P_SCORE, "nt")
        ab = jnp.where(strict, sc[0:L, 0:HL], 0.0)
        ak = jnp.where(strict, sc[0:L, HL:2 * HL], 0.0)
        rb = jnp.where(incl, sc[L:2 * L, 0:HL], 0.0)
        rk = jnp.where(incl, sc[L:2 * L, HL:2 * HL], 0.0)
        on_v = _pdot(jnp.concatenate([_stack_heads(ak, masks), rk], axis=0), v_bd, RWKV_P_RHS)
        dec_bd = jnp.concatenate([_stack_heads(bv_all[rows] * w_dec, masks),
                                  _stack_heads(k2 * w_dec, masks)], axis=0)
        return dict(rows=rows, p=_stack_heads(ab, masks), rb=rb, on_v=on_v, v_bd=v_bd,
                    dec_bd=dec_bd, s_decay=jnp.exp(cum_last),
                    lhs_s=jnp.concatenate([_stack_heads(at, masks), rt], axis=0))

    def inverse_steps(group):
        ps = [c["p"] for c in group]
        minvs = [eye + p for p in ps]
        for _ in range(5):
            ps = [_pdot(p, p, RWKV_P_INV) for p in ps]
            minvs = [m + _pdot(m, p, RWKV_P_INV) for m, p in zip(minvs, ps)]
            yield minvs

    def carry_state(c, minv):
        s_prev = s_ref[...]
        on_s = _pdot(c["lhs_s"], s_prev, RWKV_P_RHS, "nt")
        on = on_s + c["on_v"]
        sa_bd = _pdot(minv, on[0:HL], RWKV_P_RHS)
        obuf[c["rows"], :] = on[HL:HL + L] + _pdot(c["rb"], sa_bd, RWKV_P_OUT)
        s_ref[...] = s_prev * c["s_decay"] + _pdot(
            jnp.concatenate([sa_bd, c["v_bd"]], axis=0), c["dec_bd"], RWKV_P_STATE, "tn")

    def chunk(_):
        n, gsz = tb // L, RWKV_GROUP
        groups = [[prepare(c) for c in range(g, g + gsz)] for g in range(0, n, gsz)]
        minvs = list(inverse_steps(groups[0]))[-1]
        for gi, group in enumerate(groups):
            nxt = inverse_steps(groups[gi + 1]) if gi + 1 < len(groups) else iter(())
            nxt_minvs = None
            for c, minv in zip(group, minvs):
                carry_state(c, minv)
                nxt_minvs = next(nxt, nxt_minvs)
            for nxt_minvs in nxt:
                pass
            minvs = nxt_minvs

    def finish():
        o = obuf[...]
        mean = _dot_rx(o, segmean, 2)
        oc = o - mean
        var = _dot_rx(oc * oc, segmean, 2)
        on = oc * lax.rsqrt(var + RWKV_GN_EPS) * lng_ref[...] + lnb_ref[...]
        bonus = _dot_rx(r_all * k2_all * rk_ref[...], segsum, 2) * v_all
        o_ref[...] = ((on + bonus) * g_all).astype(o_ref.dtype)

    return 1, chunk, finish


def _rwkv_mixer(u, bsz, seq, mu, w0, w_up, a0, a_up, g_up, k_k, k_a, r_k, ln_g, ln_b):
    tb = RWKV_TB
    W = GROUP_W
    wup = jnp.zeros((128, W), F32).at[0:32].set(w_up)
    aup = jnp.zeros((128, W), F32).at[32:64].set(a_up)
    gup = jnp.zeros((128, W), F32).at[64:128].set(g_up)
    consts = (mu[None, :3 * W], mu[None, 3 * W:], w0[None, :], wup, a0[None, :], aup, gup,
              k_k[None, :], k_a[None, :], r_k.reshape(1, W), ln_g[None, :], ln_b[None, :])
    return _launch_mixer(
        _rwkv_body, "rwkv_mixer", u, bsz, seq, tb, [(3 * W, COL_RWKV), (128, COL_LORA)], consts,
        [(tb + 8, 3 * W), (tb + 8, 128), (W, W), (tb, W)])


def _outproj_body(x_ref, y0_ref, y1_ref, y2_ref, y3_ref, w_ref, g_ref, b_ref, rwt_ref, rb_ref,
                  of_ref, idx_ref, gate_ref, rank_ref, cnt_ref, carry_ref):
    W = GROUP_W
    mix = _dot(y0_ref[...], w_ref[0:W, :])
    mix = mix + _dot(y1_ref[...], w_ref[W:2 * W, :])
    mix = mix + _dot(y2_ref[...], w_ref[2 * W:3 * W, :])
    mix = mix + _dot(y3_ref[...], w_ref[3 * W:4 * W, :])
    out = _layer_norm(DEEPNORM_ALPHA * x_ref[...] + mix, g_ref[...], b_ref[...])
    of_ref[...] = out
    _route_tokens(out, rwt_ref, rb_ref, idx_ref, gate_ref, rank_ref, cnt_ref, carry_ref)


def _out_proj_ln_route(x, ys, w_bf, g, b, router_wt, router_b, tm=512):
    t, d = x.shape
    row = lambda i: (i, 0)
    const = lambda i: (0, 0)
    tok = lambda i: (0, i)
    return pl.pallas_call(
        _outproj_body,
        grid=(t // tm,),
        in_specs=[pl.BlockSpec((tm, d), row)] + [pl.BlockSpec((tm, GROUP_W), row)] * 4
        + [pl.BlockSpec((d, d), const), pl.BlockSpec((1, d), const), pl.BlockSpec((1, d), const),
           pl.BlockSpec((N_EXPERTS, d), const), pl.BlockSpec((N_EXPERTS, 128), const)],
        out_specs=[pl.BlockSpec((tm, d), row), pl.BlockSpec((TOP_K, tm), tok),
                   pl.BlockSpec((TOP_K, tm), tok), pl.BlockSpec((TOP_K, tm), tok),
                   pl.BlockSpec((N_EXPERTS, 128), const)],
        out_shape=[jax.ShapeDtypeStruct((t, d), F32),
                   jax.ShapeDtypeStruct((TOP_K, t), jnp.int32),
                   jax.ShapeDtypeStruct((TOP_K, t), F32),
                   jax.ShapeDtypeStruct((TOP_K, t), jnp.int32),
                   jax.ShapeDtypeStruct((N_EXPERTS, 128), F32)],
        scratch_shapes=[pltpu.VMEM((N_EXPERTS, 128), F32)],
        compiler_params=_cparams(("arbitrary",)),
        name="out_proj_ln_route",
    )(x, *ys, w_bf, g[None, :], b[None, :], router_wt,
      jnp.broadcast_to(router_b[:, None], (N_EXPERTS, 128)))


def _route_tokens(x, wt_ref, b_ref, idx_ref, gate_ref, rank_ref, cnt_ref, carry_ref):
    tr = x.shape[0]

    @pl.when(pl.program_id(0) == 0)
    def _():
        carry_ref[...] = jnp.zeros(carry_ref.shape, F32)

    logits = _pdot(wt_ref[...], x, 3, "nt") + b_ref[...][:, 0:1]
    e_iota = _iota((N_EXPERTS, tr), 0)
    work = logits
    onehot = jnp.zeros((N_EXPERTS, tr), F32)
    sels, vals, idxs = [], [], []
    for _ in range(TOP_K):
        m = jnp.max(work, axis=0, keepdims=True)
        idx = jnp.min(jnp.where(work == m, e_iota, N_EXPERTS), axis=0, keepdims=True)
        sel = e_iota == idx
        work = jnp.where(sel, -jnp.inf, work)
        onehot = onehot + sel.astype(F32)
        sels.append(sel)
        vals.append(m)
        idxs.append(idx)
    exps = [jnp.exp(v - vals[0]) for v in vals]
    tot = exps[0] + exps[1] + exps[2] + exps[3]
    upper = (_iota((tr, tr), 0) < _iota((tr, tr), 1)).astype(BF16)
    carry = carry_ref[...][:, 0:1]
    before = _dot(_bf(onehot), upper) + carry
    ranks = [jnp.sum(jnp.where(s, before, 0.0), axis=0, keepdims=True) for s in sels]
    idx_ref[...] = jnp.concatenate(idxs, axis=0)
    gate_ref[...] = jnp.concatenate([e / tot for e in exps], axis=0)
    rank_ref[...] = jnp.concatenate(ranks, axis=0).astype(jnp.int32)
    new_carry = carry + jnp.sum(onehot, axis=1, keepdims=True)
    carry_ref[...] = jnp.broadcast_to(new_carry, carry_ref.shape)
    cnt_ref[...] = jnp.broadcast_to(new_carry, cnt_ref.shape)


def _route_plan_body(idx_ref, rank_ref, cnt_ref, dest_ref, be_ref, nv_ref, ps_ref, pl_ref, vb_ref,
                     *, blk, nb_pad):
    E = N_EXPERTS
    cnt = cnt_ref[...][:, 0:1]
    padded = jnp.floor((cnt + (blk - 1)) / blk) * blk
    lower = (_iota((E, E), 1) < _iota((E, E), 0)).astype(F32)
    pstart = jnp.sum(lower * _col_to_row(padded), axis=1, keepdims=True)
    pend = pstart + padded
    ps_ref[...] = jnp.broadcast_to(pstart + cnt, ps_ref.shape).astype(jnp.int32)
    pl_ref[...] = jnp.broadcast_to(padded - cnt, pl_ref.shape).astype(jnp.int32)
    idx = idx_ref[...]
    dest = rank_ref[...].astype(F32)
    for e in range(E):
        dest = dest + jnp.where(idx == e, pstart[e:e + 1, 0:1], 0.0)
    dest_ref[...] = dest.astype(jnp.int32)
    blk_start = (_iota((1, nb_pad), 1) * blk).astype(F32)
    be = jnp.sum((pend <= blk_start).astype(F32), axis=0, keepdims=True)
    be_ref[...] = jnp.minimum(be, E - 1).astype(jnp.int32)
    nv_ref[...] = jnp.broadcast_to(pend[E - 1:E, 0:1] / blk, nv_ref.shape).astype(jnp.int32)
    owns = (pstart <= blk_start) & (blk_start < pend)
    rows = jnp.clip(pstart + cnt - blk_start, 0.0, float(blk))
    vb_ref[...] = jnp.sum(jnp.where(owns, rows, 0.0), axis=0, keepdims=True).astype(jnp.int32)


def _route_plan(idx_t, rank_t, counts, blk, nb_pad):
    t = idx_t.shape[1]
    return pl.pallas_call(
        functools.partial(_route_plan_body, blk=blk, nb_pad=nb_pad),
        out_shape=[jax.ShapeDtypeStruct((TOP_K, t), jnp.int32),
                   jax.ShapeDtypeStruct((1, nb_pad), jnp.int32),
                   jax.ShapeDtypeStruct((1, 128), jnp.int32),
                   jax.ShapeDtypeStruct((N_EXPERTS, 128), jnp.int32),
                   jax.ShapeDtypeStruct((N_EXPERTS, 128), jnp.int32),
                   jax.ShapeDtypeStruct((1, nb_pad), jnp.int32)],
        compiler_params=pltpu.CompilerParams(vmem_limit_bytes=VMEM_LIMIT),
        name="route_plan",
    )(idx_t, rank_t, counts)


def _dispatch_body(ps_ref, pl_ref, nv_ref, dest_ref, x_ref, xs_hbm, zbuf, sem, fsem,
                   *, td, blk, n_blocks):
    sizes = [blk >> (s + 1) for s in range(int(np.log2(blk)) - 3)]

    def pad_piece(e, s):
        n = pl_ref[e]
        first = ps_ref[e] + jnp.bitwise_and(n, 7)
        off = pl.multiple_of(first + jnp.bitwise_and(n, -2 * s), 8)
        return (jnp.bitwise_and(n, s) != 0,
                pltpu.make_async_copy(zbuf.at[pl.ds(0, s)], xs_hbm.at[pl.ds(off, s)], fsem))

    def pad_row(e, r):
        return (r < jnp.bitwise_and(pl_ref[e], 7),
                pltpu.make_async_copy(zbuf.at[pl.ds(0, 1)], xs_hbm.at[pl.ds(ps_ref[e] + r, 1)], fsem))

    def tail_copy(b):
        off = pl.multiple_of(b * blk, blk)
        return pltpu.make_async_copy(zbuf, xs_hbm.at[pl.ds(off, blk)], fsem)

    @pl.when(pl.program_id(0) == 0)
    def _():
        zbuf[...] = jnp.zeros(zbuf.shape, zbuf.dtype)

        def pads(start):
            def body(e, carry):
                for cond, cp in [pad_piece(e, s) for s in sizes] + [pad_row(e, r) for r in range(7)]:
                    @pl.when(cond)
                    def _():
                        cp.start() if start else cp.wait()
                return carry
            lax.fori_loop(0, N_EXPERTS, body, 0)

        def tails(start):
            def body(b, carry):
                tail_copy(b).start() if start else tail_copy(b).wait()
                return carry
            lax.fori_loop(nv_ref[0], n_blocks, body, 0)

        pads(True)
        tails(True)
        pads(False)
        tails(False)

    def issue(g, carry):
        for s in range(8):
            for k in range(TOP_K):
                row = dest_ref[g * (8 * TOP_K) + (s * TOP_K + k)]
                pltpu.make_async_copy(x_ref.at[g, pl.ds(s, 1)],
                                      xs_hbm.at[pl.ds(row, 1)], sem).start(priority=k % 2)
        return carry

    lax.fori_loop(0, td // 8, issue, 0)
    for _ in range(TOP_K):
        pltpu.make_async_copy(xs_hbm.at[pl.ds(0, td)], xs_hbm.at[pl.ds(0, td)], sem).wait()


def _dispatch(pad_start, pad_len, nv, dest_flat, x, n_rows, blk, td=512):
    t, d = x.shape
    grid_spec = pltpu.PrefetchScalarGridSpec(
        num_scalar_prefetch=3,
        grid=(t // td,),
        in_specs=[pl.BlockSpec((td * TOP_K,), lambda i, *_: (i,), memory_space=pltpu.SMEM),
                  pl.BlockSpec((td // 8, 8, d), lambda i, *_: (i, 0, 0))],
        out_specs=pl.BlockSpec(memory_space=pl.ANY),
        scratch_shapes=[pltpu.VMEM((blk, d), x.dtype), pltpu.SemaphoreType.DMA(()),
                        pltpu.SemaphoreType.DMA(())],
    )
    return pl.pallas_call(
        functools.partial(_dispatch_body, td=td, blk=blk, n_blocks=n_rows // blk),
        grid_spec=grid_spec,
        out_shape=jax.ShapeDtypeStruct((n_rows, d), x.dtype),
        compiler_params=_cparams(("arbitrary",)),
        name="dispatch",
    )(pad_start, pad_len, nv, dest_flat, x.reshape(t // 8, 8, d))


def _expert_body(be_ref, nv_ref, vb_ref, xs_ref, wgu_ref, bgu_ref, wd_ref, bd_ref, ys_ref,
                 wgu_bf, wd_bf):
    b = pl.program_id(0)
    d = D_MODEL
    blk = xs_ref.shape[0]
    half = blk // 2
    valid = b < nv_ref[0]

    @pl.when(valid)
    def _():
        prev = be_ref[jnp.maximum(b - 1, 0)]

        @pl.when((b == 0) | (be_ref[b] != prev))
        def _():
            wgu_bf[...] = wgu_ref[...].astype(BF16)
            wd_bf[...] = wd_ref[...].astype(BF16)

    def ffn(rows):
        h = _dot(_bf(xs_ref[rows, :]), wgu_bf[...]) + bgu_ref[...]
        hg = jnp.minimum(h[:, 0:d], SWIGLU_LIMIT)
        hl = jnp.clip(h[:, d:2 * d], -SWIGLU_LIMIT, SWIGLU_LIMIT)
        act = (hl + 1.0) * (hg * _sigmoid(hg * SWIGLU_ALPHA))
        ys_ref[rows, :] = _dot(_bf(act), wd_bf[...]) + bd_ref[...]

    @pl.when(valid & (vb_ref[b] > half))
    def _():
        ffn(slice(0, blk))

    @pl.when(valid & (vb_ref[b] <= half))
    def _():
        ffn(slice(0, half))
        ys_ref[half:blk, :] = jnp.zeros((blk - half, d), F32)

    @pl.when(jnp.logical_not(valid))
    def _():
        ys_ref[...] = jnp.zeros(ys_ref.shape, F32)


def _experts(be, nv, vb, xs, w_gu, b_gu, w_down, b_down, layer, blk):
    n_rows, d = xs.shape
    nb = n_rows // blk

    def row(b, be_r, nv_r, vb_r):
        return (jnp.minimum(b, nv_r[0] - 1), 0)

    def wsel(b, be_r, nv_r, vb_r):
        return (layer, be_r[jnp.minimum(b, nv_r[0] - 1)], 0, 0)

    grid_spec = pltpu.PrefetchScalarGridSpec(
        num_scalar_prefetch=3,
        grid=(nb,),
        in_specs=[pl.BlockSpec((blk, d), row),
                  pl.BlockSpec((None, None, d, 2 * d), wsel),
                  pl.BlockSpec((None, None, 1, 2 * d), wsel),
                  pl.BlockSpec((None, None, d, d), wsel),
                  pl.BlockSpec((None, None, 1, d), wsel)],
        out_specs=pl.BlockSpec((blk, d), lambda b, be_r, nv_r, vb_r: (b, 0)),
        scratch_shapes=[pltpu.VMEM((d, 2 * d), BF16), pltpu.VMEM((d, d), BF16)],
    )
    return pl.pallas_call(
        _expert_body,
        grid_spec=grid_spec,
        out_shape=jax.ShapeDtypeStruct((n_rows, d), F32),
        compiler_params=_cparams(("arbitrary",)),
        name="experts",
    )(be, nv, vb, xs, w_gu, b_gu, w_down, b_down)


def _combine_body(dcur_ref, dnext_ref, ys_hbm, x_ref, gate_ref, p_ref, wg_ref, wp_ref, g_ref, b_ref,
                  of_ref, ob_ref, ybuf, sem, *, tc):
    i = pl.program_id(0)
    n = pl.num_programs(0)
    slot = lax.rem(i, 2)

    def issue(d_ref, s):
        def body(g, carry):
            for r in range(8):
                for k in range(TOP_K):
                    row = d_ref[g * (8 * TOP_K) + (r * TOP_K + k)]
                    pltpu.make_async_copy(ys_hbm.at[pl.ds(row, 1)], ybuf.at[s, k, g, pl.ds(r, 1)],
                                          sem.at[s]).start(priority=k % 2)
            return carry
        lax.fori_loop(0, tc // 8, body, 0)

    @pl.when(i == 0)
    def _():
        issue(dcur_ref, 0)

    for s in range(2):
        @pl.when((i + 1 < n) & (slot == 1 - s))
        def _():
            issue(dnext_ref, s)

    for k in range(TOP_K):
        pltpu.make_async_copy(ys_hbm.at[pl.ds(0, tc)], ys_hbm.at[pl.ds(0, tc)], sem.at[slot]).wait()

    gate = gate_ref[...]
    d = x_ref.shape[-1]
    ffn = ybuf[slot, 0].reshape(tc, d) * gate[:, 0:1]
    for k in range(1, TOP_K):
        ffn = ffn + ybuf[slot, k].reshape(tc, d) * gate[:, k:k + 1]
    h = DEEPNORM_ALPHA * x_ref[...] + ffn
    ple = _dot(_bf(p_ref[...]), wp_ref[...])
    h = h + _sigmoid(_dot(_bf(h), wg_ref[...])) * ple
    out = _layer_norm(h, g_ref[...], b_ref[...])
    of_ref[...] = out
    ob_ref[...] = out.astype(BF16)


def _combine(dest_flat, ys, x, gate, p, wg_bf, wp_bf, g, b, layer, tc=256):
    t, d = x.shape
    n = t // tc
    row = lambda i: (i, 0)
    const = lambda i: (0, 0)
    return pl.pallas_call(
        functools.partial(_combine_body, tc=tc),
        grid=(n,),
        in_specs=[pl.BlockSpec((tc * TOP_K,), lambda i: (i,), memory_space=pltpu.SMEM),
                  pl.BlockSpec((tc * TOP_K,), lambda i: (jnp.minimum(i + 1, n - 1),),
                               memory_space=pltpu.SMEM),
                  pl.BlockSpec(memory_space=pl.ANY),
                  pl.BlockSpec((tc, d), row),
                  pl.BlockSpec((tc, TOP_K), row),
                  pl.BlockSpec((None, tc, PLE_DIM), lambda i: (layer, i, 0)),
                  pl.BlockSpec((d, d), const),
                  pl.BlockSpec((PLE_DIM, d), const),
                  pl.BlockSpec((1, d), const),
                  pl.BlockSpec((1, d), const)],
        out_specs=[pl.BlockSpec((tc, d), row), pl.BlockSpec((tc, d), row)],
        out_shape=[jax.ShapeDtypeStruct((t, d), F32), jax.ShapeDtypeStruct((t, d), BF16)],
        scratch_shapes=[pltpu.VMEM((2, TOP_K, tc // 8, 8, d), F32), pltpu.SemaphoreType.DMA((2,))],
        compiler_params=_cparams(("arbitrary",)),
        name="combine",
    )(dest_flat, dest_flat, ys, x, gate, p, wg_bf, wp_bf, g[None, :], b[None, :])


def kernel(x, p, w_in, w_out, ln1_g, ln1_b, ssd_conv_w, ssd_conv_b, ssd_a_log, ssd_dt_bias, ssd_d, ssd_norm_g, rwkv_mu, rwkv_w0, rwkv_w_up, rwkv_a0, rwkv_a_up, rwkv_g_up, rwkv_k_k, rwkv_k_a, rwkv_r_k, rwkv_ln_g, rwkv_ln_b, gla_gate_up, gla_gate_b, gla_norm_g, mlstm_conv_w, mlstm_conv_b, mlstm_i_b, mlstm_f_b, mlstm_norm_g, router_w, router_b, exp_w_gu, exp_b_gu, exp_w_down, exp_b_down, ple_gate_w, ple_proj, ln2_g, ln2_b):
    bsz, seq, d = x.shape
    t = bsz * seq
    depth = w_in.shape[0]
    blk = EXPERT_BLK
    n_blocks = -(-(t * TOP_K + N_EXPERTS * (blk - 1)) // blk)
    n_rows = n_blocks * blk
    nb_pad = -(-n_blocks // 128) * 128

    xf = x.reshape(t, d)
    xb = xf.astype(BF16)
    p2 = p.reshape(depth, t, PLE_DIM)
    b_gu4 = exp_b_gu[:, :, None, :]
    b_down4 = exp_b_down[:, :, None, :]
    w_packed = _pack_w_in(w_in)
    for i in range(depth):
        u = _in_proj(xb, w_packed, i)
        y_ssd = _ssd_mixer(u, bsz, seq, ssd_conv_w[i], ssd_conv_b[i], ssd_a_log[i], ssd_dt_bias[i],
                           ssd_d[i], ssd_norm_g[i])
        y_rwkv = _rwkv_mixer(u, bsz, seq, rwkv_mu[i], rwkv_w0[i], rwkv_w_up[i], rwkv_a0[i],
                             rwkv_a_up[i], rwkv_g_up[i], rwkv_k_k[i], rwkv_k_a[i], rwkv_r_k[i],
                             rwkv_ln_g[i], rwkv_ln_b[i])
        y_gla = _gla_mixer(u, bsz, seq, gla_gate_up[i], gla_gate_b[i], gla_norm_g[i])
        y_ml = _mlstm_mixer(u, bsz, seq, mlstm_conv_w[i], mlstm_conv_b[i], mlstm_i_b[i],
                            mlstm_f_b[i], mlstm_norm_g[i])
        x1f, idx_t, gate_t, rank_t, counts = _out_proj_ln_route(
            xf, (y_ssd, y_rwkv, y_gla, y_ml), w_out[i].astype(BF16), ln1_g[i], ln1_b[i],
            router_w[i].T, router_b[i])
        dest_t, be, nv, pad_start, pad_len, vb = _route_plan(idx_t, rank_t, counts, blk, nb_pad)
        nv = nv.reshape(128)
        dest_flat = dest_t.T.reshape(t * TOP_K)
        xs = _dispatch(pad_start[:, 0], pad_len[:, 0], nv, dest_flat, x1f, n_rows, blk)
        ys = _experts(be.reshape(nb_pad), nv, vb.reshape(nb_pad), xs, exp_w_gu, b_gu4, exp_w_down,
                      b_down4, i, blk)
        xf, xb = _combine(dest_flat, ys, x1f, gate_t.T, p2, ple_gate_w[i].astype(BF16),
                          ple_proj[i].astype(BF16), ln2_g[i], ln2_b[i], i)
    return xf.reshape(bsz, seq, d)
```

```python
import functools

import jax
import jax.numpy as jnp
import numpy as np
from jax import lax
from jax.experimental import pallas as pl
from jax.experimental.pallas import tpu as pltpu

F32 = jnp.float32
BF16 = jnp.bfloat16

D_MODEL = 1024
DEPTH = 4
GROUP_W = 256
N_HEADS = 4
HEAD_DIM = 64
SSD_STATE = 128
SSD_CHUNK = 128
GLA_DK = 128
GLA_HEAD_K = 32
GLA_TAU = 16.0
CHUNK = 64
RWKV_P_SCORE = 1
RWKV_P_INV = 1
RWKV_P_RHS = 1
RWKV_P_OUT = 1
RWKV_P_STATE = 1
RWKV_GN_EPS = 64e-5
NORM_EPS = 1e-5
LN_EPS = 1e-5
N_EXPERTS = 32
TOP_K = 4
SWIGLU_LIMIT = 7.0
SWIGLU_ALPHA = 1.702
PLE_DIM = 256
DEEPNORM_ALPHA = (2 * DEPTH) ** 0.25

U_WIDTH = 3584
COL_SSD_Z, COL_SSD_X, COL_SSD_BC = 0, 256, 512
COL_RWKV = 768
COL_ML_QK = 1536
COL_ML_VO = 2048
COL_GLA_QK, COL_GLA_V, COL_GLA_OG = 2560, 2816, 3072
COL_LORA = 3328
COL_MISC = 3456
MISC_DT, MISC_GLA, MISC_I, MISC_F = 0, 4, 20, 24

SSD_TB = GLA_TB = RWKV_TB = 512
MLSTM_TB = 256
RWKV_GROUP = 4
EXPERT_BLK = 512
V7X_VMEM_BYTES = 64 * 1024 * 1024
VMEM_LIMIT = V7X_VMEM_BYTES * 7 // 8


def _cparams(sem):
    return pltpu.CompilerParams(dimension_semantics=sem, vmem_limit_bytes=VMEM_LIMIT)


_DIMS = {"nn": (((1,), (0,)), ((), ())), "nt": (((1,), (1,)), ((), ())),
         "tn": (((0,), (0,)), ((), ()))}


def _dot(a, b):
    return lax.dot_general(a, b, _DIMS["nn"], preferred_element_type=F32)


def _dot_nt(a, b):
    return lax.dot_general(a, b, _DIMS["nt"], preferred_element_type=F32)


def _dot_tn(a, b):
    return lax.dot_general(a, b, _DIMS["tn"], preferred_element_type=F32)


def _bf(x):
    return x.astype(BF16)


def _split_bf(x):
    hi = x.astype(BF16)
    return hi, (x - hi.astype(F32)).astype(BF16)


def _pdot(a, b, passes, kind="nn"):
    assert passes in (1, 3)
    dg = lambda x, y: lax.dot_general(x, y, _DIMS[kind], preferred_element_type=F32)
    if passes == 1:
        return dg(_bf(a), _bf(b))
    ah, al = _split_bf(a)
    bh, bl = _split_bf(b)
    return dg(ah, bh) + (dg(ah, bl) + dg(al, bh))


def _pieces(x, n):
    out, rem = [], x
    for _ in range(n):
        part = rem.astype(BF16)
        out.append(part)
        rem = rem - part.astype(F32)
    return out


def _dot_lx(a_exact, b, n):
    ab = _bf(a_exact)
    acc = None
    for part in _pieces(b, n):
        term = lax.dot_general(ab, part, _DIMS["nn"], preferred_element_type=F32)
        acc = term if acc is None else acc + term
    return acc


def _dot_rx(a, b_exact, n):
    bb = _bf(b_exact)
    acc = None
    for part in _pieces(a, n):
        term = lax.dot_general(part, bb, _DIMS["nn"], preferred_element_type=F32)
        acc = term if acc is None else acc + term
    return acc


def _sigmoid(x):
    return 1.0 / (1.0 + jnp.exp(-x))


def _silu(x):
    return x * _sigmoid(x)


def _softplus(x):
    return jnp.maximum(x, 0.0) + jnp.log1p(jnp.exp(-jnp.abs(x)))


def _log_sigmoid(x):
    return jnp.minimum(x, 0.0) - jnp.log1p(jnp.exp(-jnp.abs(x)))


def _iota(shape, dim):
    return lax.broadcasted_iota(jnp.int32, shape, dim)


def _tri_incl(n):
    return (_iota((n, n), 0) >= _iota((n, n), 1)).astype(F32)


def _seg_matrix(n, seg, value):
    sh = int(np.log2(seg))
    same = (lax.shift_right_logical(_iota((n, n), 0), sh)
            == lax.shift_right_logical(_iota((n, n), 1), sh))
    return jnp.where(same, value, 0.0).astype(F32)


def _head_mask(width, seg, h):
    lane = _iota((1, width), 1)
    return ((lane >= h * seg) & (lane < (h + 1) * seg)).astype(F32)


def _expand(cols, seg):
    n = len(cols)
    rows = cols[0].shape[0]
    width = n * seg
    lane = _iota((rows, width), 1)
    out = jnp.broadcast_to(cols[n - 1], (rows, width))
    for h in range(n - 2, -1, -1):
        out = jnp.where(lane < (h + 1) * seg, jnp.broadcast_to(cols[h], (rows, width)), out)
    return out


def _col_to_row(col):
    n = col.shape[0]
    eye = _iota((n, n), 0) == _iota((n, n), 1)
    return jnp.sum(jnp.where(eye, col, 0.0), axis=0, keepdims=True)


def _layer_norm(x, g, b):
    mu = jnp.mean(x, axis=-1, keepdims=True)
    xc = x - mu
    var = jnp.mean(xc * xc, axis=-1, keepdims=True)
    return xc * lax.rsqrt(var + LN_EPS) * g + b


def _causal_conv_silu(buf, w_ref, b_ref, rows):
    acc = b_ref[...]
    for k in range(4):
        acc = acc + buf[pl.ds(5 + k, rows), :] * w_ref[k:k + 1, :]
    return _silu(acc)


_W_IN_COLS = 3484
_W_RWKV, _W_GLA, _W_ML = 772, 772 + 896, 772 + 896 + 784
_PACK_SEGMENTS = (
    (COL_SSD_Z, 0, 768),
    (COL_RWKV, _W_RWKV, 768),
    (COL_ML_QK, _W_ML, 768),
    (COL_ML_VO + 256, _W_ML + 776, 256),
    (COL_GLA_QK, _W_GLA, 512),
    (COL_GLA_OG, _W_GLA + 528, 256),
    (COL_LORA, _W_RWKV + 768, 128),
)
_PACK_MISC = ((768, MISC_DT, 4), (_W_GLA + 512, MISC_GLA, 16), (_W_ML + 768, MISC_I, 8))


def _pack_body(w_ref, o_ref):
    def cols(src, n):
        sh = src % 128
        a0 = src - sh
        if sh == 0:
            return w_ref[:, a0:a0 + n]
        wd = -(-(sh + n) // 128) * 128
        return pltpu.roll(w_ref[:, a0:a0 + wd], wd - sh, axis=1)[:, 0:n]

    for dst, src, n in _PACK_SEGMENTS:
        o_ref[:, dst:dst + n] = cols(src, n).astype(BF16)
    lane = _iota((w_ref.shape[0], 128), 1)
    misc = jnp.zeros((w_ref.shape[0], 128), F32)
    for src, lane0, n in _PACK_MISC:
        assert src % 128 == lane0
        a0 = src - lane0
        misc = jnp.where((lane >= lane0) & (lane < lane0 + n), w_ref[:, a0:a0 + 128], misc)
    o_ref[:, COL_MISC:COL_MISC + 128] = misc.astype(BF16)


def _pack_w_in(w_in, rb=256):
    depth, d, n = w_in.shape
    assert n == _W_IN_COLS
    return pl.pallas_call(
        _pack_body,
        grid=(depth, d // rb),
        in_specs=[pl.BlockSpec((None, rb, U_WIDTH), lambda l, r: (l, r, 0))],
        out_specs=pl.BlockSpec((None, rb, U_WIDTH), lambda l, r: (l, r, 0)),
        out_shape=jax.ShapeDtypeStruct((depth, d, U_WIDTH), BF16),
        compiler_params=_cparams(("arbitrary", "arbitrary")),
        name="pack_w_in",
    )(w_in)


def _in_proj_body(x_ref, w_ref, o_ref):
    o_ref[...] = jnp.dot(x_ref[...], w_ref[...], preferred_element_type=F32)


def _in_proj(x_bf, w_all, layer, tm=512, tn=1792):
    t, k = x_bf.shape
    n = w_all.shape[2]
    return pl.pallas_call(
        _in_proj_body,
        grid=(n // tn, t // tm),
        in_specs=[pl.BlockSpec((tm, k), lambda j, i: (i, 0)),
                  pl.BlockSpec((None, k, tn), lambda j, i: (layer, 0, j))],
        out_specs=pl.BlockSpec((tm, tn), lambda j, i: (i, j)),
        out_shape=jax.ShapeDtypeStruct((t, n), F32),
        compiler_params=_cparams(("arbitrary", "arbitrary")),
        name="in_proj",
    )(x_bf, w_all)


def _ssd_body(z_ref, x_ref, bc_ref, misc_ref, cwx_ref, cwbc_ref, cbx_ref, cbbc_ref,
              alog_ref, dtb_ref, dskip_ref, ng_ref, o_ref, xbuf, bcbuf, st_ref, *, tb):
    xbuf[8:8 + tb, :] = x_ref[...]
    bcbuf[8:8 + tb, :] = bc_ref[...]
    xs_all = _causal_conv_silu(xbuf, cwx_ref, cbx_ref, tb)
    bc_all = _causal_conv_silu(bcbuf, cwbc_ref, cbbc_ref, tb)
    xbuf[0:8, :] = xbuf[tb:tb + 8, :]
    bcbuf[0:8, :] = bcbuf[tb:tb + 8, :]

    dt_all = _softplus(misc_ref[...] + dtb_ref[...])
    adt_all = dt_all * (-jnp.exp(alog_ref[...]))
    z_all = z_ref[...]
    L = SSD_CHUNK
    tri = _tri_incl(L)
    causal = _iota((L, L), 0) >= _iota((L, L), 1)
    masks = [_head_mask(GROUP_W, HEAD_DIM, h) for h in range(N_HEADS)]

    def chunk(c):
        rows = slice(c * L, (c + 1) * L)
        xs = xs_all[rows]
        bm = _bf(bc_all[rows, 0:SSD_STATE])
        cm = _bf(bc_all[rows, SSD_STATE:2 * SSD_STATE])
        dt = dt_all[rows]
        acum = _dot_lx(tri, adt_all[rows], 3)
        cols = [acum[:, h:h + 1] for h in range(N_HEADS)]
        lasts = [acum[L - 1:L, h:h + 1] for h in range(N_HEADS)]
        xc = xs * _expand([dt[:, h:h + 1] for h in range(N_HEADS)], HEAD_DIM)
        g = _dot_nt(cm, bm)
        y = jnp.zeros((L, GROUP_W), F32)
        for h in range(N_HEADS):
            lmat = jnp.exp(jnp.where(causal, cols[h] - _col_to_row(cols[h]), -jnp.inf))
            y = y + _dot(_bf(g * lmat), _bf(xc * masks[h]))
        dec_states = _expand([jnp.exp(lasts[h] - cols[h]) for h in range(N_HEADS)], HEAD_DIM)
        st_prev = st_ref[...]
        y = y + _dot(cm, _bf(st_prev)) * _expand([jnp.exp(cols[h]) for h in range(N_HEADS)], HEAD_DIM)
        st_new = _dot_tn(bm, _bf(xc * dec_states))
        st_ref[...] = st_prev * _expand([jnp.exp(lasts[h]) for h in range(N_HEADS)], HEAD_DIM) + st_new
        y = y + xs * dskip_ref[...]
        y = y * _silu(z_all[rows])
        ms = jnp.mean(y * y, axis=-1, keepdims=True)
        o_ref[rows, :] = (y * lax.rsqrt(ms + NORM_EPS) * ng_ref[...]).astype(o_ref.dtype)

    return tb // L, chunk, lambda: None


def _launch_mixer(body, name, u, bsz, seq, tb, u_blocks, consts, scratch):
    u3 = u.reshape(bsz, seq, u.shape[-1])
    n_in, n_c = len(u_blocks), len(consts)

    def step(*refs):
        scr = refs[n_in + n_c + 1:]

        @pl.when(pl.program_id(1) == 0)
        def _():
            for r in scr:
                r[...] = jnp.zeros(r.shape, F32)

        n_chunks, chunk, finish = body(*refs, tb=tb)
        for c in range(n_chunks):
            chunk(c)
        finish()

    in_specs = [pl.BlockSpec((None, tb, w), functools.partial(lambda g, j, cb: (g, j, cb), cb=c // w))
                for w, c in u_blocks]
    in_specs += [pl.BlockSpec(a.shape, functools.partial(lambda g, j, nd: (0,) * nd, nd=a.ndim))
                 for a in consts]
    out = pl.pallas_call(
        step,
        grid=(bsz, seq // tb),
        in_specs=in_specs,
        out_specs=pl.BlockSpec((None, tb, GROUP_W), lambda g, j: (g, j, 0)),
        out_shape=jax.ShapeDtypeStruct((bsz, seq, GROUP_W), BF16),
        scratch_shapes=[pltpu.VMEM(s, F32) for s in scratch],
        compiler_params=_cparams(("arbitrary", "arbitrary")),
        name=name,
    )(*([u3] * n_in), *consts)
    return out.reshape(bsz * seq, GROUP_W)


def _ssd_mixer(u, bsz, seq, conv_w, conv_b, a_log, dt_bias, d_skip, norm_g):
    tb = SSD_TB
    pad4 = lambda v: jnp.zeros((1, 128), F32).at[0, :N_HEADS].set(v)
    consts = (conv_w[:, :GROUP_W], conv_w[:, GROUP_W:],
              conv_b[None, :GROUP_W], conv_b[None, GROUP_W:],
              pad4(a_log), pad4(dt_bias),
              jnp.repeat(d_skip, HEAD_DIM)[None, :], norm_g[None, :])
    return _launch_mixer(
        _ssd_body, "ssd_mixer", u, bsz, seq, tb,
        [(GROUP_W, COL_SSD_Z), (GROUP_W, COL_SSD_X), (GROUP_W, COL_SSD_BC), (128, COL_MISC)],
        consts, [(tb + 8, GROUP_W), (tb + 8, GROUP_W), (SSD_STATE, GROUP_W)])


def _gla_body(qk_ref, v_ref, og_ref, misc_ref, gup_ref, gb_ref, ng_ref, o_ref, st_ref, obuf,
              *, tb):
    L = CHUNK
    q_all = qk_ref[:, 0:GLA_DK] * (GLA_HEAD_K ** -0.5)
    k_all = qk_ref[:, GLA_DK:2 * GLA_DK]
    v_all = v_ref[...]
    og_all = og_ref[...]
    gpre = _dot(_bf(misc_ref[...]), gup_ref[...]) + gb_ref[...]
    loga_all = _log_sigmoid(gpre) / GLA_TAU
    tri = _tri_incl(L)
    causal = _iota((L, L), 0) >= _iota((L, L), 1)
    kmasks = [_head_mask(GLA_DK, GLA_HEAD_K, h) for h in range(N_HEADS)]
    vmasks = [_head_mask(GROUP_W, HEAD_DIM, h) for h in range(N_HEADS)]
    bd = (lax.shift_right_logical(_iota((GROUP_W, GLA_DK), 0), 6)
          == lax.shift_right_logical(_iota((GROUP_W, GLA_DK), 1), 5))
    segmean = _seg_matrix(GROUP_W, HEAD_DIM, 1.0 / HEAD_DIM)

    def chunk(c):
        rows = slice(c * L, (c + 1) * L)
        bcum = _dot_lx(tri, loga_all[rows], 3)
        b_last = bcum[L - 1:L, :]
        q_dec = q_all[rows] * jnp.exp(bcum)
        k_inv = _bf(k_all[rows] * jnp.exp(-bcum))
        k_dec = _bf(k_all[rows] * jnp.exp(b_last - bcum))
        v = v_all[rows]
        st_prev = st_ref[...]
        o = _dot_nt(_bf(q_dec), _bf(st_prev))
        for h in range(N_HEADS):
            attn = jnp.where(causal, _dot_nt(_bf(q_dec * kmasks[h]), k_inv), 0.0)
            o = o + _dot(_bf(attn), _bf(v * vmasks[h]))
        contrib = jnp.where(bd, _dot_tn(_bf(v), k_dec), 0.0)
        st_ref[...] = st_prev * jnp.exp(b_last) + contrib
        obuf[rows, :] = o

    def finish():
        o = obuf[...]
        ms = _dot_rx(o * o, segmean, 2)
        out = o * lax.rsqrt(ms + NORM_EPS) * ng_ref[...] * _silu(og_all)
        o_ref[...] = out.astype(o_ref.dtype)

    return tb // L, chunk, finish


def _gla_mixer(u, bsz, seq, gate_up, gate_b, norm_g):
    gup = jnp.zeros((128, GLA_DK), F32).at[MISC_GLA:MISC_GLA + 16].set(gate_up).astype(BF16)
    return _launch_mixer(
        _gla_body, "gla_mixer", u, bsz, seq, GLA_TB,
        [(GROUP_W, COL_GLA_QK), (GROUP_W, COL_GLA_V), (GROUP_W, COL_GLA_OG), (128, COL_MISC)],
        (gup, gate_b[None, :], norm_g[None, :]),
        [(GROUP_W, GLA_DK), (GLA_TB, GROUP_W)])


def _mlstm_body(qk_ref, vo_ref, misc_ref, cw_ref, cb_ref, ib_ref, fb_ref, ng_ref, o_ref,
                qkbuf, c_ref, n_ref, m_ref, fin, *, tb):
    W = GROUP_W
    qkbuf[8:8 + tb, :] = qk_ref[...]
    qk = _causal_conv_silu(qkbuf, cw_ref, cb_ref, tb)
    qkbuf[0:8, :] = qkbuf[tb:tb + 8, :]
    q_all = qk[:, 0:W] * (HEAD_DIM ** -0.5)
    k_all = qk[:, W:2 * W]
    v_all = vo_ref[:, 0:W]
    og_all = vo_ref[:, W:2 * W]
    misc = misc_ref[...]
    ipre_all = misc + ib_ref[...]
    lf_all = _log_sigmoid(misc + fb_ref[...])

    L = CHUNK
    tri = _tri_incl(L)
    causal = _iota((L, L), 0) >= _iota((L, L), 1)
    masks = [_head_mask(W, HEAD_DIM, h) for h in range(N_HEADS)]
    bd = _seg_matrix(W, HEAD_DIM, 1.0)
    segmean = _seg_matrix(W, HEAD_DIM, 1.0 / HEAD_DIM)

    def prepare(c):
        rows = slice(c * L, (c + 1) * L)
        q, k, v = q_all[rows], k_all[rows], v_all[rows]
        kb = _bf(k)
        bcs = _dot_lx(tri, lf_all[rows], 3)
        ipre = ipre_all[rows]
        num = jnp.zeros((L, W), F32)
        heads, w_st = [], []
        for h in range(N_HEADS):
            b_col = bcs[:, MISC_F + h:MISC_F + h + 1]
            i_col = ipre[:, MISC_I + h:MISC_I + h + 1]
            b_last = bcs[L - 1:L, MISC_F + h:MISC_F + h + 1]
            dmat = jnp.where(causal, b_col - _col_to_row(b_col) + _col_to_row(i_col), -jnp.inf)
            m_dmat = jnp.max(dmat, axis=-1, keepdims=True)
            a_st = b_last - b_col + i_col
            m_loc = jnp.max(a_st, axis=0, keepdims=True)
            w_st.append(jnp.exp(a_st - m_loc))
            scores = _dot_nt(_bf(q * masks[h]), kb) * jnp.exp(dmat - m_dmat)
            num = num + _dot(_bf(scores), _bf(v * masks[h]))
            heads.append(dict(b_col=b_col, b_last=b_last, m_loc=m_loc, m_dmat=m_dmat,
                              rowsum=jnp.sum(scores, axis=-1, keepdims=True)))
        wst = _expand(w_st, HEAD_DIM)
        return dict(rows=rows, q=q, num=num, heads=heads,
                    c_new=bd * _dot_tn(_bf(v * wst), kb),
                    n_new=jnp.sum(k * wst, axis=0, keepdims=True))

    def chunk(_):
        for p in [prepare(c) for c in range(tb // L)]:
            rows, q = p["rows"], p["q"]
            m_state = m_ref[...]
            w_inter, corr, rowsum, eneg, s_old, s_new, m_new = [], [], [], [], [], [], []
            for h, hd in enumerate(p["heads"]):
                m_prev = m_state[:, h:h + 1]
                mn = jnp.maximum(hd["b_last"] + m_prev, hd["m_loc"])
                m_new.append(mn)
                s_old.append(jnp.exp(hd["b_last"] + m_prev - mn))
                s_new.append(jnp.exp(hd["m_loc"] - mn))
                m_inter = hd["b_col"] + m_prev
                m_row = jnp.maximum(m_inter, hd["m_dmat"])
                w_inter.append(jnp.exp(m_inter - m_row))
                cr = jnp.exp(hd["m_dmat"] - m_row)
                corr.append(cr)
                rowsum.append(cr * hd["rowsum"])
                eneg.append(jnp.exp(-m_row))
            c_prev = c_ref[...]
            n_prev = n_ref[...]
            wi = _expand(w_inter, HEAD_DIM)
            fin[0, rows, :] = _expand(corr, HEAD_DIM) * p["num"] + wi * _dot_nt(_bf(q), _bf(c_prev))
            fin[1, rows, :] = wi
            fin[2, rows, :] = q * n_prev
            fin[3, rows, :] = _expand(rowsum, HEAD_DIM)
            fin[4, rows, :] = _expand(eneg, HEAD_DIM)
            so = _expand(s_old, HEAD_DIM)
            sn = _expand(s_new, HEAD_DIM)
            c_ref[...] = so * c_prev + sn * p["c_new"]
            n_ref[...] = so * n_prev + sn * p["n_new"]
            lane = _iota((1, 128), 1)
            m_vec = jnp.zeros((1, 128), F32)
            for h in range(N_HEADS):
                m_vec = jnp.where(lane == h, m_new[h], m_vec)
            m_ref[...] = m_vec

    def finish():
        den = fin[1] * _dot_rx(fin[2], bd, 3) + fin[3]
        hval = fin[0] / jnp.maximum(jnp.abs(den), fin[4])
        hv = hval * _sigmoid(og_all)
        mean = _dot_rx(hv, segmean, 2)
        xc = hv - mean
        var = _dot_rx(xc * xc, segmean, 2)
        o_ref[...] = (xc * lax.rsqrt(var + NORM_EPS) * ng_ref[...]).astype(o_ref.dtype)

    return 1, chunk, finish


def _mlstm_mixer(u, bsz, seq, conv_w, conv_b, i_b, f_b, norm_g):
    tb = MLSTM_TB
    ib = jnp.zeros((1, 128), F32).at[0, MISC_I:MISC_I + N_HEADS].set(i_b)
    fb = jnp.zeros((1, 128), F32).at[0, MISC_F:MISC_F + N_HEADS].set(f_b)
    return _launch_mixer(
        _mlstm_body, "mlstm_mixer", u, bsz, seq, tb,
        [(2 * GROUP_W, COL_ML_QK), (2 * GROUP_W, COL_ML_VO), (128, COL_MISC)],
        (conv_w, conv_b[None, :], ib, fb, norm_g[None, :]),
        [(tb + 8, 2 * GROUP_W), (GROUP_W, GROUP_W), (1, GROUP_W), (1, 128), (5, tb, GROUP_W)])


def _stack_heads(x, masks):
    return jnp.concatenate([x * m for m in masks], axis=0)


def _rwkv_body(rkv_ref, lora_ref, mu_rkv_ref, mu_lora_ref, w0_ref, wup_ref, a0_ref, aup_ref,
               gup_ref, kk_ref, ka_ref, rk_ref, lng_ref, lnb_ref, o_ref,
               rbuf, lbuf, s_ref, obuf, *, tb):
    W = GROUP_W
    rbuf[8:8 + tb, :] = rkv_ref[...]
    lbuf[8:8 + tb, :] = lora_ref[...]
    rkv = rkv_ref[...]
    lora = lora_ref[...]
    rkv = rkv + (rbuf[pl.ds(7, tb), :] - rkv) * mu_rkv_ref[...]
    lora = lora + (lbuf[pl.ds(7, tb), :] - lora) * mu_lora_ref[...]
    rbuf[0:8, :] = rbuf[tb:tb + 8, :]
    lbuf[0:8, :] = lbuf[tb:tb + 8, :]

    r_all, k_all, v_all = rkv[:, 0:W], rkv[:, W:2 * W], rkv[:, 2 * W:3 * W]
    wpre = w0_ref[...] + _pdot(jnp.tanh(lora), wup_ref[...], 3)
    lw_all = -jnp.exp(-_softplus(-wpre) - 0.5)
    a_all = _sigmoid(a0_ref[...] + _pdot(lora, aup_ref[...], 3))
    g_all = _pdot(_sigmoid(lora), gup_ref[...], 3)
    segsum = _seg_matrix(W, HEAD_DIM, 1.0)
    segmean = _seg_matrix(W, HEAD_DIM, 1.0 / HEAD_DIM)
    kk = k_all * kk_ref[...]
    kk = kk / jnp.maximum(jnp.sqrt(_dot_rx(kk * kk, segsum, 2)), 1e-12)
    k2_all = k_all * (1.0 + (a_all - 1.0) * ka_ref[...])
    av_all = -kk
    bv_all = kk * a_all

    L = CHUNK
    HL = N_HEADS * L
    tri = _tri_incl(L)
    masks = [_head_mask(W, HEAD_DIM, h) for h in range(N_HEADS)]
    t_idx = _iota((L, HL), 0)
    s_idx = jnp.bitwise_and(_iota((L, HL), 1), L - 1)
    strict = s_idx < t_idx
    incl = s_idx <= t_idx
    eye = (_iota((HL, HL), 0) == _iota((HL, HL), 1)).astype(F32)

    def prepare(c):
        rows = slice(c * L, (c + 1) * L)
        lw = lw_all[rows]
        cum = _dot_lx(tri, lw, 3)
        cum_last = cum[L - 1:L, :]
        w_inv = jnp.exp(-cum)
        w_dec = jnp.exp(cum_last - cum)
        k2, v = k2_all[rows], v_all[rows]
        rt = r_all[rows] * jnp.exp(cum)
        at = av_all[rows] * jnp.exp(cum - lw)
        bt_bd = _stack_heads(bv_all[rows] * w_inv, masks)
        kt_bd = _stack_heads(k2 * w_inv, masks)
        v_bd = _stack_heads(v, masks)
        sc = _pdot(jnp.concatenate([at, rt], axis=0), jnp.concatenate([bt_bd, kt_bd], axis=0),
                   RWKV_P_SCORE, "nt")
        ab = jnp.where(strict, sc[0:L, 0:HL], 0.0)
        ak = jnp.where(strict, sc[0:L, HL:2 * HL], 0.0)
        rb = jnp.where(incl, sc[L:2 * L, 0:HL], 0.0)
        rk = jnp.where(incl, sc[L:2 * L, HL:2 * HL], 0.0)
        on_v = _pdot(jnp.concatenate([_stack_heads(ak, masks), rk], axis=0), v_bd, RWKV_P_RHS)
        dec_bd = jnp.concatenate([_stack_heads(bv_all[rows] * w_dec, masks),
                                  _stack_heads(k2 * w_dec, masks)], axis=0)
        return dict(rows=rows, p=_stack_heads(ab, masks), rb=rb, on_v=on_v, v_bd=v_bd,
                    dec_bd=dec_bd, s_decay=jnp.exp(cum_last),
                    lhs_s=jnp.concatenate([_stack_heads(at, masks), rt], axis=0))

    def inverse_steps(group):
        ps = [c["p"] for c in group]
        minvs = [eye + p for p in ps]
        for _ in range(5):
            ps = [_pdot(p, p, RWKV_P_INV) for p in ps]
            minvs = [m + _pdot(m, p, RWKV_P_INV) for m, p in zip(minvs, ps)]
            yield minvs

    def carry_state(c, minv):
        s_prev = s_ref[...]
        on_s = _pdot(c["lhs_s"], s_prev, RWKV_P_RHS, "nt")
        on = on_s + c["on_v"]
        sa_bd = _pdot(minv, on[0:HL], RWKV_P_RHS)
        obuf[c["rows"], :] = on[HL:HL + L] + _pdot(c["rb"], sa_bd, RWKV_P_OUT)
        s_ref[...] = s_prev * c["s_decay"] + _pdot(
            jnp.concatenate([sa_bd, c["v_bd"]], axis=0), c["dec_bd"], RWKV_P_STATE, "tn")

    def chunk(_):
        n, gsz = tb // L, RWKV_GROUP
        groups = [[prepare(c) for c in range(g, g + gsz)] for g in range(0, n, gsz)]
        minvs = list(inverse_steps(groups[0]))[-1]
        for gi, group in enumerate(groups):
            nxt = inverse_steps(groups[gi + 1]) if gi + 1 < len(groups) else iter(())
            nxt_minvs = None
            for c, minv in zip(group, minvs):
                carry_state(c, minv)
                nxt_minvs = next(nxt, nxt_minvs)
            for nxt_minvs in nxt:
                pass
            minvs = nxt_minvs

    def finish():
        o = obuf[...]
        mean = _dot_rx(o, segmean, 2)
        oc = o - mean
        var = _dot_rx(oc * oc, segmean, 2)
        on = oc * lax.rsqrt(var + RWKV_GN_EPS) * lng_ref[...] + lnb_ref[...]
        bonus = _dot_rx(r_all * k2_all * rk_ref[...], segsum, 2) * v_all
        o_ref[...] = ((on + bonus) * g_all).astype(o_ref.dtype)

    return 1, chunk, finish


def _rwkv_mixer(u, bsz, seq, mu, w0, w_up, a0, a_up, g_up, k_k, k_a, r_k, ln_g, ln_b):
    tb = RWKV_TB
    W = GROUP_W
    wup = jnp.zeros((128, W), F32).at[0:32].set(w_up)
    aup = jnp.zeros((128, W), F32).at[32:64].set(a_up)
    gup = jnp.zeros((128, W), F32).at[64:128].set(g_up)
    consts = (mu[None, :3 * W], mu[None, 3 * W:], w0[None, :], wup, a0[None, :], aup, gup,
              k_k[None, :], k_a[None, :], r_k.reshape(1, W), ln_g[None, :], ln_b[None, :])
    return _launch_mixer(
        _rwkv_body, "rwkv_mixer", u, bsz, seq, tb, [(3 * W, COL_RWKV), (128, COL_LORA)], consts,
        [(tb + 8, 3 * W), (tb + 8, 128), (W, W), (tb, W)])


def _outproj_body(x_ref, y0_ref, y1_ref, y2_ref, y3_ref, w_ref, g_ref, b_ref, rwt_ref, rb_ref,
                  of_ref, idx_ref, gate_ref, rank_ref, cnt_ref, carry_ref):
    W = GROUP_W
    mix = _dot(y0_ref[...], w_ref[0:W, :])
    mix = mix + _dot(y1_ref[...], w_ref[W:2 * W, :])
    mix = mix + _dot(y2_ref[...], w_ref[2 * W:3 * W, :])
    mix = mix + _dot(y3_ref[...], w_ref[3 * W:4 * W, :])
    out = _layer_norm(DEEPNORM_ALPHA * x_ref[...] + mix, g_ref[...], b_ref[...])
    of_ref[...] = out
    _route_tokens(out, rwt_ref, rb_ref, idx_ref, gate_ref, rank_ref, cnt_ref, carry_ref)


def _out_proj_ln_route(x, ys, w_bf, g, b, router_wt, router_b, tm=512):
    t, d = x.shape
    row = lambda i: (i, 0)
    const = lambda i: (0, 0)
    tok = lambda i: (0, i)
    return pl.pallas_call(
        _outproj_body,
        grid=(t // tm,),
        in_specs=[pl.BlockSpec((tm, d), row)] + [pl.BlockSpec((tm, GROUP_W), row)] * 4
        + [pl.BlockSpec((d, d), const), pl.BlockSpec((1, d), const), pl.BlockSpec((1, d), const),
           pl.BlockSpec((N_EXPERTS, d), const), pl.BlockSpec((N_EXPERTS, 128), const)],
        out_specs=[pl.BlockSpec((tm, d), row), pl.BlockSpec((TOP_K, tm), tok),
                   pl.BlockSpec((TOP_K, tm), tok), pl.BlockSpec((TOP_K, tm), tok),
                   pl.BlockSpec((N_EXPERTS, 128), const)],
        out_shape=[jax.ShapeDtypeStruct((t, d), F32),
                   jax.ShapeDtypeStruct((TOP_K, t), jnp.int32),
                   jax.ShapeDtypeStruct((TOP_K, t), F32),
                   jax.ShapeDtypeStruct((TOP_K, t), jnp.int32),
                   jax.ShapeDtypeStruct((N_EXPERTS, 128), F32)],
        scratch_shapes=[pltpu.VMEM((N_EXPERTS, 128), F32)],
        compiler_params=_cparams(("arbitrary",)),
        name="out_proj_ln_route",
    )(x, *ys, w_bf, g[None, :], b[None, :], router_wt,
      jnp.broadcast_to(router_b[:, None], (N_EXPERTS, 128)))


def _route_tokens(x, wt_ref, b_ref, idx_ref, gate_ref, rank_ref, cnt_ref, carry_ref):
    tr = x.shape[0]

    @pl.when(pl.program_id(0) == 0)
    def _():
        carry_ref[...] = jnp.zeros(carry_ref.shape, F32)

    logits = _pdot(wt_ref[...], x, 3, "nt") + b_ref[...][:, 0:1]
    e_iota = _iota((N_EXPERTS, tr), 0)
    work = logits
    onehot = jnp.zeros((N_EXPERTS, tr), F32)
    sels, vals, idxs = [], [], []
    for _ in range(TOP_K):
        m = jnp.max(work, axis=0, keepdims=True)
        idx = jnp.min(jnp.where(work == m, e_iota, N_EXPERTS), axis=0, keepdims=True)
        sel = e_iota == idx
        work = jnp.where(sel, -jnp.inf, work)
        onehot = onehot + sel.astype(F32)
        sels.append(sel)
        vals.append(m)
        idxs.append(idx)
    exps = [jnp.exp(v - vals[0]) for v in vals]
    tot = exps[0] + exps[1] + exps[2] + exps[3]
    upper = (_iota((tr, tr), 0) < _iota((tr, tr), 1)).astype(BF16)
    carry = carry_ref[...][:, 0:1]
    before = _dot(_bf(onehot), upper) + carry
    ranks = [jnp.sum(jnp.where(s, before, 0.0), axis=0, keepdims=True) for s in sels]
    idx_ref[...] = jnp.concatenate(idxs, axis=0)
    gate_ref[...] = jnp.concatenate([e / tot for e in exps], axis=0)
    rank_ref[...] = jnp.concatenate(ranks, axis=0).astype(jnp.int32)
    new_carry = carry + jnp.sum(onehot, axis=1, keepdims=True)
    carry_ref[...] = jnp.broadcast_to(new_carry, carry_ref.shape)
    cnt_ref[...] = jnp.broadcast_to(new_carry, cnt_ref.shape)


def _route_plan_body(idx_ref, rank_ref, cnt_ref, dest_ref, be_ref, nv_ref, ps_ref, pl_ref, vb_ref,
                     *, blk, nb_pad):
    E = N_EXPERTS
    cnt = cnt_ref[...][:, 0:1]
    padded = jnp.floor((cnt + (blk - 1)) / blk) * blk
    lower = (_iota((E, E), 1) < _iota((E, E), 0)).astype(F32)
    pstart = jnp.sum(lower * _col_to_row(padded), axis=1, keepdims=True)
    pend = pstart + padded
    ps_ref[...] = jnp.broadcast_to(pstart + cnt, ps_ref.shape).astype(jnp.int32)
    pl_ref[...] = jnp.broadcast_to(padded - cnt, pl_ref.shape).astype(jnp.int32)
    idx = idx_ref[...]
    dest = rank_ref[...].astype(F32)
    for e in range(E):
        dest = dest + jnp.where(idx == e, pstart[e:e + 1, 0:1], 0.0)
    dest_ref[...] = dest.astype(jnp.int32)
    blk_start = (_iota((1, nb_pad), 1) * blk).astype(F32)
    be = jnp.sum((pend <= blk_start).astype(F32), axis=0, keepdims=True)
    be_ref[...] = jnp.minimum(be, E - 1).astype(jnp.int32)
    nv_ref[...] = jnp.broadcast_to(pend[E - 1:E, 0:1] / blk, nv_ref.shape).astype(jnp.int32)
    owns = (pstart <= blk_start) & (blk_start < pend)
    rows = jnp.clip(pstart + cnt - blk_start, 0.0, float(blk))
    vb_ref[...] = jnp.sum(jnp.where(owns, rows, 0.0), axis=0, keepdims=True).astype(jnp.int32)


def _route_plan(idx_t, rank_t, counts, blk, nb_pad):
    t = idx_t.shape[1]
    return pl.pallas_call(
        functools.partial(_route_plan_body, blk=blk, nb_pad=nb_pad),
        out_shape=[jax.ShapeDtypeStruct((TOP_K, t), jnp.int32),
                   jax.ShapeDtypeStruct((1, nb_pad), jnp.int32),
                   jax.ShapeDtypeStruct((1, 128), jnp.int32),
                   jax.ShapeDtypeStruct((N_EXPERTS, 128), jnp.int32),
                   jax.ShapeDtypeStruct((N_EXPERTS, 128), jnp.int32),
                   jax.ShapeDtypeStruct((1, nb_pad), jnp.int32)],
        compiler_params=pltpu.CompilerParams(vmem_limit_bytes=VMEM_LIMIT),
        name="route_plan",
    )(idx_t, rank_t, counts)


def _dispatch_body(ps_ref, pl_ref, nv_ref, dest_ref, x_ref, xs_hbm, zbuf, sem, fsem,
                   *, td, blk, n_blocks):
    sizes = [blk >> (s + 1) for s in range(int(np.log2(blk)) - 3)]

    def pad_piece(e, s):
        n = pl_ref[e]
        first = ps_ref[e] + jnp.bitwise_and(n, 7)
        off = pl.multiple_of(first + jnp.bitwise_and(n, -2 * s), 8)
        return (jnp.bitwise_and(n, s) != 0,
                pltpu.make_async_copy(zbuf.at[pl.ds(0, s)], xs_hbm.at[pl.ds(off, s)], fsem))

    def pad_row(e, r):
        return (r < jnp.bitwise_and(pl_ref[e], 7),
                pltpu.make_async_copy(zbuf.at[pl.ds(0, 1)], xs_hbm.at[pl.ds(ps_ref[e] + r, 1)], fsem))

    def tail_copy(b):
        off = pl.multiple_of(b * blk, blk)
        return pltpu.make_async_copy(zbuf, xs_hbm.at[pl.ds(off, blk)], fsem)

    @pl.when(pl.program_id(0) == 0)
    def _():
        zbuf[...] = jnp.zeros(zbuf.shape, zbuf.dtype)

        def pads(start):
            def body(e, carry):
                for cond, cp in [pad_piece(e, s) for s in sizes] + [pad_row(e, r) for r in range(7)]:
                    @pl.when(cond)
                    def _():
                        cp.start() if start else cp.wait()
                return carry
            lax.fori_loop(0, N_EXPERTS, body, 0)

        def tails(start):
            def body(b, carry):
                tail_copy(b).start() if start else tail_copy(b).wait()
                return carry
            lax.fori_loop(nv_ref[0], n_blocks, body, 0)

        pads(True)
        tails(True)
        pads(False)
        tails(False)

    def issue(g, carry):
        for s in range(8):
            for k in range(TOP_K):
                row = dest_ref[g * (8 * TOP_K) + (s * TOP_K + k)]
                pltpu.make_async_copy(x_ref.at[g, pl.ds(s, 1)],
                                      xs_hbm.at[pl.ds(row, 1)], sem).start(priority=k % 2)
        return carry

    lax.fori_loop(0, td // 8, issue, 0)
    for _ in range(TOP_K):
        pltpu.make_async_copy(xs_hbm.at[pl.ds(0, td)], xs_hbm.at[pl.ds(0, td)], sem).wait()


def _dispatch(pad_start, pad_len, nv, dest_flat, x, n_rows, blk, td=512):
    t, d = x.shape
    grid_spec = pltpu.PrefetchScalarGridSpec(
        num_scalar_prefetch=3,
        grid=(t // td,),
        in_specs=[pl.BlockSpec((td * TOP_K,), lambda i, *_: (i,), memory_space=pltpu.SMEM),
                  pl.BlockSpec((td // 8, 8, d), lambda i, *_: (i, 0, 0))],
        out_specs=pl.BlockSpec(memory_space=pl.ANY),
        scratch_shapes=[pltpu.VMEM((blk, d), x.dtype), pltpu.SemaphoreType.DMA(()),
                        pltpu.SemaphoreType.DMA(())],
    )
    return pl.pallas_call(
        functools.partial(_dispatch_body, td=td, blk=blk, n_blocks=n_rows // blk),
        grid_spec=grid_spec,
        out_shape=jax.ShapeDtypeStruct((n_rows, d), x.dtype),
        compiler_params=_cparams(("arbitrary",)),
        name="dispatch",
    )(pad_start, pad_len, nv, dest_flat, x.reshape(t // 8, 8, d))


def _expert_body(be_ref, nv_ref, vb_ref, xs_ref, wgu_ref, bgu_ref, wd_ref, bd_ref, ys_ref,
                 wgu_bf, wd_bf):
    b = pl.program_id(0)
    d = D_MODEL
    blk = xs_ref.shape[0]
    half = blk // 2
    valid = b < nv_ref[0]

    @pl.when(valid)
    def _():
        prev = be_ref[jnp.maximum(b - 1, 0)]

        @pl.when((b == 0) | (be_ref[b] != prev))
        def _():
            wgu_bf[...] = wgu_ref[...].astype(BF16)
            wd_bf[...] = wd_ref[...].astype(BF16)

    def ffn(rows):
        h = _dot(_bf(xs_ref[rows, :]), wgu_bf[...]) + bgu_ref[...]
        hg = jnp.minimum(h[:, 0:d], SWIGLU_LIMIT)
        hl = jnp.clip(h[:, d:2 * d], -SWIGLU_LIMIT, SWIGLU_LIMIT)
        act = (hl + 1.0) * (hg * _sigmoid(hg * SWIGLU_ALPHA))
        ys_ref[rows, :] = _dot(_bf(act), wd_bf[...]) + bd_ref[...]

    @pl.when(valid & (vb_ref[b] > half))
    def _():
        ffn(slice(0, blk))

    @pl.when(valid & (vb_ref[b] <= half))
    def _():
        ffn(slice(0, half))
        ys_ref[half:blk, :] = jnp.zeros((blk - half, d), F32)

    @pl.when(jnp.logical_not(valid))
    def _():
        ys_ref[...] = jnp.zeros(ys_ref.shape, F32)


def _experts(be, nv, vb, xs, w_gu, b_gu, w_down, b_down, layer, blk):
    n_rows, d = xs.shape
    nb = n_rows // blk

    def row(b, be_r, nv_r, vb_r):
        return (jnp.minimum(b, nv_r[0] - 1), 0)

    def wsel(b, be_r, nv_r, vb_r):
        return (layer, be_r[jnp.minimum(b, nv_r[0] - 1)], 0, 0)

    grid_spec = pltpu.PrefetchScalarGridSpec(
        num_scalar_prefetch=3,
        grid=(nb,),
        in_specs=[pl.BlockSpec((blk, d), row),
                  pl.BlockSpec((None, None, d, 2 * d), wsel),
                  pl.BlockSpec((None, None, 1, 2 * d), wsel),
                  pl.BlockSpec((None, None, d, d), wsel),
                  pl.BlockSpec((None, None, 1, d), wsel)],
        out_specs=pl.BlockSpec((blk, d), lambda b, be_r, nv_r, vb_r: (b, 0)),
        scratch_shapes=[pltpu.VMEM((d, 2 * d), BF16), pltpu.VMEM((d, d), BF16)],
    )
    return pl.pallas_call(
        _expert_body,
        grid_spec=grid_spec,
        out_shape=jax.ShapeDtypeStruct((n_rows, d), F32),
        compiler_params=_cparams(("arbitrary",)),
        name="experts",
    )(be, nv, vb, xs, w_gu, b_gu, w_down, b_down)


def _combine_body(dcur_ref, dnext_ref, ys_hbm, x_ref, gate_ref, p_ref, wg_ref, wp_ref, g_ref, b_ref,
                  of_ref, ob_ref, ybuf, sem, *, tc):
    i = pl.program_id(0)
    n = pl.num_programs(0)
    slot = lax.rem(i, 2)

    def issue(d_ref, s):
        def body(g, carry):
            for r in range(8):
                for k in range(TOP_K):
                    row = d_ref[g * (8 * TOP_K) + (r * TOP_K + k)]
                    pltpu.make_async_copy(ys_hbm.at[pl.ds(row, 1)], ybuf.at[s, k, g, pl.ds(r, 1)],
                                          sem.at[s]).start(priority=k % 2)
            return carry
        lax.fori_loop(0, tc // 8, body, 0)

    @pl.when(i == 0)
    def _():
        issue(dcur_ref, 0)

    for s in range(2):
        @pl.when((i + 1 < n) & (slot == 1 - s))
        def _():
            issue(dnext_ref, s)

    for k in range(TOP_K):
        pltpu.make_async_copy(ys_hbm.at[pl.ds(0, tc)], ys_hbm.at[pl.ds(0, tc)], sem.at[slot]).wait()

    gate = gate_ref[...]
    d = x_ref.shape[-1]
    ffn = ybuf[slot, 0].reshape(tc, d) * gate[:, 0:1]
    for k in range(1, TOP_K):
        ffn = ffn + ybuf[slot, k].reshape(tc, d) * gate[:, k:k + 1]
    h = DEEPNORM_ALPHA * x_ref[...] + ffn
    ple = _dot(_bf(p_ref[...]), wp_ref[...])
    h = h + _sigmoid(_dot(_bf(h), wg_ref[...])) * ple
    out = _layer_norm(h, g_ref[...], b_ref[...])
    of_ref[...] = out
    ob_ref[...] = out.astype(BF16)


def _combine(dest_flat, ys, x, gate, p, wg_bf, wp_bf, g, b, layer, tc=256):
    t, d = x.shape
    n = t // tc
    row = lambda i: (i, 0)
    const = lambda i: (0, 0)
    return pl.pallas_call(
        functools.partial(_combine_body, tc=tc),
        grid=(n,),
        in_specs=[pl.BlockSpec((tc * TOP_K,), lambda i: (i,), memory_space=pltpu.SMEM),
                  pl.BlockSpec((tc * TOP_K,), lambda i: (jnp.minimum(i + 1, n - 1),),
                               memory_space=pltpu.SMEM),
                  pl.BlockSpec(memory_space=pl.ANY),
                  pl.BlockSpec((tc, d), row),
                  pl.BlockSpec((tc, TOP_K), row),
                  pl.BlockSpec((None, tc, PLE_DIM), lambda i: (layer, i, 0)),
                  pl.BlockSpec((d, d), const),
                  pl.BlockSpec((PLE_DIM, d), const),
                  pl.BlockSpec((1, d), const),
                  pl.BlockSpec((1, d), const)],
        out_specs=[pl.BlockSpec((tc, d), row), pl.BlockSpec((tc, d), row)],
        out_shape=[jax.ShapeDtypeStruct((t, d), F32), jax.ShapeDtypeStruct((t, d), BF16)],
        scratch_shapes=[pltpu.VMEM((2, TOP_K, tc // 8, 8, d), F32), pltpu.SemaphoreType.DMA((2,))],
        compiler_params=_cparams(("arbitrary",)),
        name="combine",
    )(dest_flat, dest_flat, ys, x, gate, p, wg_bf, wp_bf, g[None, :], b[None, :])


def kernel(x, p, w_in, w_out, ln1_g, ln1_b, ssd_conv_w, ssd_conv_b, ssd_a_log, ssd_dt_bias, ssd_d, ssd_norm_g, rwkv_mu, rwkv_w0, rwkv_w_up, rwkv_a0, rwkv_a_up, rwkv_g_up, rwkv_k_k, rwkv_k_a, rwkv_r_k, rwkv_ln_g, rwkv_ln_b, gla_gate_up, gla_gate_b, gla_norm_g, mlstm_conv_w, mlstm_conv_b, mlstm_i_b, mlstm_f_b, mlstm_norm_g, router_w, router_b, exp_w_gu, exp_b_gu, exp_w_down, exp_b_down, ple_gate_w, ple_proj, ln2_g, ln2_b):
    bsz, seq, d = x.shape
    t = bsz * seq
    depth = w_in.shape[0]
    blk = EXPERT_BLK
    n_blocks = -(-(t * TOP_K + N_EXPERTS * (blk - 1)) // blk)
    n_rows = n_blocks * blk
    nb_pad = -(-n_blocks // 128) * 128

    xf = x.reshape(t, d)
    xb = xf.astype(BF16)
    p2 = p.reshape(depth, t, PLE_DIM)
    b_gu4 = exp_b_gu[:, :, None, :]
    b_down4 = exp_b_down[:, :, None, :]
    w_packed = _pack_w_in(w_in)
    for i in range(depth):
        u = _in_proj(xb, w_packed, i)
        y_ssd = _ssd_mixer(u, bsz, seq, ssd_conv_w[i], ssd_conv_b[i], ssd_a_log[i], ssd_dt_bias[i],
                           ssd_d[i], ssd_norm_g[i])
        y_rwkv = _rwkv_mixer(u, bsz, seq, rwkv_mu[i], rwkv_w0[i], rwkv_w_up[i], rwkv_a0[i],
                             rwkv_a_up[i], rwkv_g_up[i], rwkv_k_k[i], rwkv_k_a[i], rwkv_r_k[i],
                             rwkv_ln_g[i], rwkv_ln_b[i])
        y_gla = _gla_mixer(u, bsz, seq, gla_gate_up[i], gla_gate_b[i], gla_norm_g[i])
        y_ml = _mlstm_mixer(u, bsz, seq, mlstm_conv_w[i], mlstm_conv_b[i], mlstm_i_b[i],
                            mlstm_f_b[i], mlstm_norm_g[i])
        x1f, idx_t, gate_t, rank_t, counts = _out_proj_ln_route(
            xf, (y_ssd, y_rwkv, y_gla, y_ml), w_out[i].astype(BF16), ln1_g[i], ln1_b[i],
            router_w[i].T, router_b[i])
        dest_t, be, nv, pad_start, pad_len, vb = _route_plan(idx_t, rank_t, counts, blk, nb_pad)
        nv = nv.reshape(128)
        dest_flat = dest_t.T.reshape(t * TOP_K)
        xs = _dispatch(pad_start[:, 0], pad_len[:, 0], nv, dest_flat, x1f, n_rows, blk)
        ys = _experts(be.reshape(nb_pad), nv, vb.reshape(nb_pad), xs, exp_w_gu, b_gu4, exp_w_down,
                      b_down4, i, blk)
        xf, xb = _combine(dest_flat, ys, x1f, gate_t.T, p2, ple_gate_w[i].astype(BF16),
                          ple_proj[i].astype(BF16), ln2_g[i], ln2_b[i], i)
    return xf.reshape(bsz, seq, d)
```

```python
import functools

import jax
import jax.numpy as jnp
import numpy as np
from jax import lax
from jax.experimental import pallas as pl
from jax.experimental.pallas import tpu as pltpu

F32 = jnp.float32
BF16 = jnp.bfloat16

D_MODEL = 1024
DEPTH = 4
GROUP_W = 256
N_HEADS = 4
HEAD_DIM = 64
SSD_STATE = 128
SSD_CHUNK = 128
GLA_DK = 128
GLA_HEAD_K = 32
GLA_TAU = 16.0
CHUNK = 64
RWKV_P_SCORE = 1
RWKV_P_INV = 1
RWKV_P_RHS = 1
RWKV_P_OUT = 1
RWKV_P_STATE = 1
RWKV_GN_EPS = 64e-5
NORM_EPS = 1e-5
LN_EPS = 1e-5
N_EXPERTS = 32
TOP_K = 4
SWIGLU_LIMIT = 7.0
SWIGLU_ALPHA = 1.702
PLE_DIM = 256
DEEPNORM_ALPHA = (2 * DEPTH) ** 0.25

U_WIDTH = 3584
COL_SSD_Z, COL_SSD_X, COL_SSD_BC = 0, 256, 512
COL_RWKV = 768
COL_ML_QK = 1536
COL_ML_VO = 2048
COL_GLA_QK, COL_GLA_V, COL_GLA_OG = 2560, 2816, 3072
COL_LORA = 3328
COL_MISC = 3456
MISC_DT, MISC_GLA, MISC_I, MISC_F = 0, 4, 20, 24

SSD_TB = GLA_TB = RWKV_TB = 512
MLSTM_TB = 256
RWKV_GROUP = 4
EXPERT_BLK = 512
V7X_VMEM_BYTES = 64 * 1024 * 1024
VMEM_LIMIT = V7X_VMEM_BYTES * 7 // 8


def _cparams(sem):
    return pltpu.CompilerParams(dimension_semantics=sem, vmem_limit_bytes=VMEM_LIMIT)


_DIMS = {"nn": (((1,), (0,)), ((), ())), "nt": (((1,), (1,)), ((), ())),
         "tn": (((0,), (0,)), ((), ()))}


def _dot(a, b):
    return lax.dot_general(a, b, _DIMS["nn"], preferred_element_type=F32)


def _dot_nt(a, b):
    return lax.dot_general(a, b, _DIMS["nt"], preferred_element_type=F32)


def _dot_tn(a, b):
    return lax.dot_general(a, b, _DIMS["tn"], preferred_element_type=F32)


def _bf(x):
    return x.astype(BF16)


def _split_bf(x):
    hi = x.astype(BF16)
    return hi, (x - hi.astype(F32)).astype(BF16)


def _pdot(a, b, passes, kind="nn"):
    assert passes in (1, 3)
    dg = lambda x, y: lax.dot_general(x, y, _DIMS[kind], preferred_element_type=F32)
    if passes == 1:
        return dg(_bf(a), _bf(b))
    ah, al = _split_bf(a)
    bh, bl = _split_bf(b)
    return dg(ah, bh) + (dg(ah, bl) + dg(al, bh))


def _pieces(x, n):
    out, rem = [], x
    for _ in range(n):
        part = rem.astype(BF16)
        out.append(part)
        rem = rem - part.astype(F32)
    return out


def _dot_lx(a_exact, b, n):
    ab = _bf(a_exact)
    acc = None
    for part in _pieces(b, n):
        term = lax.dot_general(ab, part, _DIMS["nn"], preferred_element_type=F32)
        acc = term if acc is None else acc + term
    return acc


def _dot_rx(a, b_exact, n):
    bb = _bf(b_exact)
    acc = None
    for part in _pieces(a, n):
        term = lax.dot_general(part, bb, _DIMS["nn"], preferred_element_type=F32)
        acc = term if acc is None else acc + term
    return acc


def _sigmoid(x):
    return 1.0 / (1.0 + jnp.exp(-x))


def _silu(x):
    return x * _sigmoid(x)


def _softplus(x):
    return jnp.maximum(x, 0.0) + jnp.log1p(jnp.exp(-jnp.abs(x)))


def _log_sigmoid(x):
    return jnp.minimum(x, 0.0) - jnp.log1p(jnp.exp(-jnp.abs(x)))


def _iota(shape, dim):
    return lax.broadcasted_iota(jnp.int32, shape, dim)


def _tri_incl(n):
    return (_iota((n, n), 0) >= _iota((n, n), 1)).astype(F32)


def _seg_matrix(n, seg, value):
    sh = int(np.log2(seg))
    same = (lax.shift_right_logical(_iota((n, n), 0), sh)
            == lax.shift_right_logical(_iota((n, n), 1), sh))
    return jnp.where(same, value, 0.0).astype(F32)


def _head_mask(width, seg, h):
    lane = _iota((1, width), 1)
    return ((lane >= h * seg) & (lane < (h + 1) * seg)).astype(F32)


def _expand(cols, seg):
    n = len(cols)
    rows = cols[0].shape[0]
    width = n * seg
    lane = _iota((rows, width), 1)
    out = jnp.broadcast_to(cols[n - 1], (rows, width))
    for h in range(n - 2, -1, -1):
        out = jnp.where(lane < (h + 1) * seg, jnp.broadcast_to(cols[h], (rows, width)), out)
    return out


def _col_to_row(col):
    n = col.shape[0]
    eye = _iota((n, n), 0) == _iota((n, n), 1)
    return jnp.sum(jnp.where(eye, col, 0.0), axis=0, keepdims=True)


def _layer_norm(x, g, b):
    mu = jnp.mean(x, axis=-1, keepdims=True)
    xc = x - mu
    var = jnp.mean(xc * xc, axis=-1, keepdims=True)
    return xc * lax.rsqrt(var + LN_EPS) * g + b


def _causal_conv_silu(buf, w_ref, b_ref, rows):
    acc = b_ref[...]
    for k in range(4):
        acc = acc + buf[pl.ds(5 + k, rows), :] * w_ref[k:k + 1, :]
    return _silu(acc)


_W_IN_COLS = 3484
_W_RWKV, _W_GLA, _W_ML = 772, 772 + 896, 772 + 896 + 784
_PACK_SEGMENTS = (
    (COL_SSD_Z, 0, 768),
    (COL_RWKV, _W_RWKV, 768),
    (COL_ML_QK, _W_ML, 768),
    (COL_ML_VO + 256, _W_ML + 776, 256),
    (COL_GLA_QK, _W_GLA, 512),
    (COL_GLA_OG, _W_GLA + 528, 256),
    (COL_LORA, _W_RWKV + 768, 128),
)
_PACK_MISC = ((768, MISC_DT, 4), (_W_GLA + 512, MISC_GLA, 16), (_W_ML + 768, MISC_I, 8))


def _pack_body(w_ref, o_ref):
    def cols(src, n):
        sh = src % 128
        a0 = src - sh
        if sh == 0:
            return w_ref[:, a0:a0 + n]
        wd = -(-(sh + n) // 128) * 128
        return pltpu.roll(w_ref[:, a0:a0 + wd], wd - sh, axis=1)[:, 0:n]

    for dst, src, n in _PACK_SEGMENTS:
        o_ref[:, dst:dst + n] = cols(src, n).astype(BF16)
    lane = _iota((w_ref.shape[0], 128), 1)
    misc = jnp.zeros((w_ref.shape[0], 128), F32)
    for src, lane0, n in _PACK_MISC:
        assert src % 128 == lane0
        a0 = src - lane0
        misc = jnp.where((lane >= lane0) & (lane < lane0 + n), w_ref[:, a0:a0 + 128], misc)
    o_ref[:, COL_MISC:COL_MISC + 128] = misc.astype(BF16)


def _pack_w_in(w_in, rb=256):
    depth, d, n = w_in.shape
    assert n == _W_IN_COLS
    return pl.pallas_call(
        _pack_body,
        grid=(depth, d // rb),
        in_specs=[pl.BlockSpec((None, rb, U_WIDTH), lambda l, r: (l, r, 0))],
        out_specs=pl.BlockSpec((None, rb, U_WIDTH), lambda l, r: (l, r, 0)),
        out_shape=jax.ShapeDtypeStruct((depth, d, U_WIDTH), BF16),
        compiler_params=_cparams(("arbitrary", "arbitrary")),
        name="pack_w_in",
    )(w_in)


def _in_proj_body(x_ref, w_ref, o_ref):
    o_ref[...] = jnp.dot(x_ref[...], w_ref[...], preferred_element_type=F32)


def _in_proj(x_bf, w_all, layer, tm=512, tn=1792):
    t, k = x_bf.shape
    n = w_all.shape[2]
    return pl.pallas_call(
        _in_proj_body,
        grid=(n // tn, t // tm),
        in_specs=[pl.BlockSpec((tm, k), lambda j, i: (i, 0)),
                  pl.BlockSpec((None, k, tn), lambda j, i: (layer, 0, j))],
        out_specs=pl.BlockSpec((tm, tn), lambda j, i: (i, j)),
        out_shape=jax.ShapeDtypeStruct((t, n), F32),
        compiler_params=_cparams(("arbitrary", "arbitrary")),
        name="in_proj",
    )(x_bf, w_all)


def _ssd_body(z_ref, x_ref, bc_ref, misc_ref, cwx_ref, cwbc_ref, cbx_ref, cbbc_ref,
              alog_ref, dtb_ref, dskip_ref, ng_ref, o_ref, xbuf, bcbuf, st_ref, *, tb):
    xbuf[8:8 + tb, :] = x_ref[...]
    bcbuf[8:8 + tb, :] = bc_ref[...]
    xs_all = _causal_conv_silu(xbuf, cwx_ref, cbx_ref, tb)
    bc_all = _causal_conv_silu(bcbuf, cwbc_ref, cbbc_ref, tb)
    xbuf[0:8, :] = xbuf[tb:tb + 8, :]
    bcbuf[0:8, :] = bcbuf[tb:tb + 8, :]

    dt_all = _softplus(misc_ref[...] + dtb_ref[...])
    adt_all = dt_all * (-jnp.exp(alog_ref[...]))
    z_all = z_ref[...]
    L = SSD_CHUNK
    tri = _tri_incl(L)
    causal = _iota((L, L), 0) >= _iota((L, L), 1)
    masks = [_head_mask(GROUP_W, HEAD_DIM, h) for h in range(N_HEADS)]

    def chunk(c):
        rows = slice(c * L, (c + 1) * L)
        xs = xs_all[rows]
        bm = _bf(bc_all[rows, 0:SSD_STATE])
        cm = _bf(bc_all[rows, SSD_STATE:2 * SSD_STATE])
        dt = dt_all[rows]
        acum = _dot_lx(tri, adt_all[rows], 3)
        cols = [acum[:, h:h + 1] for h in range(N_HEADS)]
        lasts = [acum[L - 1:L, h:h + 1] for h in range(N_HEADS)]
        xc = xs * _expand([dt[:, h:h + 1] for h in range(N_HEADS)], HEAD_DIM)
        g = _dot_nt(cm, bm)
        y = jnp.zeros((L, GROUP_W), F32)
        for h in range(N_HEADS):
            lmat = jnp.exp(jnp.where(causal, cols[h] - _col_to_row(cols[h]), -jnp.inf))
            y = y + _dot(_bf(g * lmat), _bf(xc * masks[h]))
        dec_states = _expand([jnp.exp(lasts[h] - cols[h]) for h in range(N_HEADS)], HEAD_DIM)
        st_prev = st_ref[...]
        y = y + _dot(cm, _bf(st_prev)) * _expand([jnp.exp(cols[h]) for h in range(N_HEADS)], HEAD_DIM)
        st_new = _dot_tn(bm, _bf(xc * dec_states))
        st_ref[...] = st_prev * _expand([jnp.exp(lasts[h]) for h in range(N_HEADS)], HEAD_DIM) + st_new
        y = y + xs * dskip_ref[...]
        y = y * _silu(z_all[rows])
        ms = jnp.mean(y * y, axis=-1, keepdims=True)
        o_ref[rows, :] = (y * lax.rsqrt(ms + NORM_EPS) * ng_ref[...]).astype(o_ref.dtype)

    return tb // L, chunk, lambda: None


def _launch_mixer(body, name, u, bsz, seq, tb, u_blocks, consts, scratch):
    u3 = u.reshape(bsz, seq, u.shape[-1])
    n_in, n_c = len(u_blocks), len(consts)

    def step(*refs):
        scr = refs[n_in + n_c + 1:]

        @pl.when(pl.program_id(1) == 0)
        def _():
            for r in scr:
                r[...] = jnp.zeros(r.shape, F32)

        n_chunks, chunk, finish = body(*refs, tb=tb)
        for c in range(n_chunks):
            chunk(c)
        finish()

    in_specs = [pl.BlockSpec((None, tb, w), functools.partial(lambda g, j, cb: (g, j, cb), cb=c // w))
                for w, c in u_blocks]
    in_specs += [pl.BlockSpec(a.shape, functools.partial(lambda g, j, nd: (0,) * nd, nd=a.ndim))
                 for a in consts]
    out = pl.pallas_call(
        step,
        grid=(bsz, seq // tb),
        in_specs=in_specs,
        out_specs=pl.BlockSpec((None, tb, GROUP_W), lambda g, j: (g, j, 0)),
        out_shape=jax.ShapeDtypeStruct((bsz, seq, GROUP_W), BF16),
        scratch_shapes=[pltpu.VMEM(s, F32) for s in scratch],
        compiler_params=_cparams(("arbitrary", "arbitrary")),
        name=name,
    )(*([u3] * n_in), *consts)
    return out.reshape(bsz * seq, GROUP_W)


def _ssd_mixer(u, bsz, seq, conv_w, conv_b, a_log, dt_bias, d_skip, norm_g):
    tb = SSD_TB
    pad4 = lambda v: jnp.zeros((1, 128), F32).at[0, :N_HEADS].set(v)
    consts = (conv_w[:, :GROUP_W], conv_w[:, GROUP_W:],
              conv_b[None, :GROUP_W], conv_b[None, GROUP_W:],
              pad4(a_log), pad4(dt_bias),
              jnp.repeat(d_skip, HEAD_DIM)[None, :], norm_g[None, :])
    return _launch_mixer(
        _ssd_body, "ssd_mixer", u, bsz, seq, tb,
        [(GROUP_W, COL_SSD_Z), (GROUP_W, COL_SSD_X), (GROUP_W, COL_SSD_BC), (128, COL_MISC)],
        consts, [(tb + 8, GROUP_W), (tb + 8, GROUP_W), (SSD_STATE, GROUP_W)])


def _gla_body(qk_ref, v_ref, og_ref, misc_ref, gup_ref, gb_ref, ng_ref, o_ref, st_ref, obuf,
              *, tb):
    L = CHUNK
    q_all = qk_ref[:, 0:GLA_DK] * (GLA_HEAD_K ** -0.5)
    k_all = qk_ref[:, GLA_DK:2 * GLA_DK]
    v_all = v_ref[...]
    og_all = og_ref[...]
    gpre = _dot(_bf(misc_ref[...]), gup_ref[...]) + gb_ref[...]
    loga_all = _log_sigmoid(gpre) / GLA_TAU
    tri = _tri_incl(L)
    causal = _iota((L, L), 0) >= _iota((L, L), 1)
    kmasks = [_head_mask(GLA_DK, GLA_HEAD_K, h) for h in range(N_HEADS)]
    vmasks = [_head_mask(GROUP_W, HEAD_DIM, h) for h in range(N_HEADS)]
    bd = (lax.shift_right_logical(_iota((GROUP_W, GLA_DK), 0), 6)
          == lax.shift_right_logical(_iota((GROUP_W, GLA_DK), 1), 5))
    segmean = _seg_matrix(GROUP_W, HEAD_DIM, 1.0 / HEAD_DIM)

    def chunk(c):
        rows = slice(c * L, (c + 1) * L)
        bcum = _dot_lx(tri, loga_all[rows], 3)
        b_last = bcum[L - 1:L, :]
        q_dec = q_all[rows] * jnp.exp(bcum)
        k_inv = _bf(k_all[rows] * jnp.exp(-bcum))
        k_dec = _bf(k_all[rows] * jnp.exp(b_last - bcum))
        v = v_all[rows]
        st_prev = st_ref[...]
        o = _dot_nt(_bf(q_dec), _bf(st_prev))
        for h in range(N_HEADS):
            attn = jnp.where(causal, _dot_nt(_bf(q_dec * kmasks[h]), k_inv), 0.0)
            o = o + _dot(_bf(attn), _bf(v * vmasks[h]))
        contrib = jnp.where(bd, _dot_tn(_bf(v), k_dec), 0.0)
        st_ref[...] = st_prev * jnp.exp(b_last) + contrib
        obuf[rows, :] = o

    def finish():
        o = obuf[...]
        ms = _dot_rx(o * o, segmean, 2)
        out = o * lax.rsqrt(ms + NORM_EPS) * ng_ref[...] * _silu(og_all)
        o_ref[...] = out.astype(o_ref.dtype)

    return tb // L, chunk, finish


def _gla_mixer(u, bsz, seq, gate_up, gate_b, norm_g):
    gup = jnp.zeros((128, GLA_DK), F32).at[MISC_GLA:MISC_GLA + 16].set(gate_up).astype(BF16)
    return _launch_mixer(
        _gla_body, "gla_mixer", u, bsz, seq, GLA_TB,
        [(GROUP_W, COL_GLA_QK), (GROUP_W, COL_GLA_V), (GROUP_W, COL_GLA_OG), (128, COL_MISC)],
        (gup, gate_b[None, :], norm_g[None, :]),
        [(GROUP_W, GLA_DK), (GLA_TB, GROUP_W)])


def _mlstm_body(qk_ref, vo_ref, misc_ref, cw_ref, cb_ref, ib_ref, fb_ref, ng_ref, o_ref,
                qkbuf, c_ref, n_ref, m_ref, fin, *, tb):
    W = GROUP_W
    qkbuf[8:8 + tb, :] = qk_ref[...]
    qk = _causal_conv_silu(qkbuf, cw_ref, cb_ref, tb)
    qkbuf[0:8, :] = qkbuf[tb:tb + 8, :]
    q_all = qk[:, 0:W] * (HEAD_DIM ** -0.5)
    k_all = qk[:, W:2 * W]
    v_all = vo_ref[:, 0:W]
    og_all = vo_ref[:, W:2 * W]
    misc = misc_ref[...]
    ipre_all = misc + ib_ref[...]
    lf_all = _log_sigmoid(misc + fb_ref[...])

    L = CHUNK
    tri = _tri_incl(L)
    causal = _iota((L, L), 0) >= _iota((L, L), 1)
    masks = [_head_mask(W, HEAD_DIM, h) for h in range(N_HEADS)]
    bd = _seg_matrix(W, HEAD_DIM, 1.0)
    segmean = _seg_matrix(W, HEAD_DIM, 1.0 / HEAD_DIM)

    def prepare(c):
        rows = slice(c * L, (c + 1) * L)
        q, k, v = q_all[rows], k_all[rows], v_all[rows]
        kb = _bf(k)
        bcs = _dot_lx(tri, lf_all[rows], 3)
        ipre = ipre_all[rows]
        num = jnp.zeros((L, W), F32)
        heads, w_st = [], []
        for h in range(N_HEADS):
            b_col = bcs[:, MISC_F + h:MISC_F + h + 1]
            i_col = ipre[:, MISC_I + h:MISC_I + h + 1]
            b_last = bcs[L - 1:L, MISC_F + h:MISC_F + h + 1]
            dmat = jnp.where(causal, b_col - _col_to_row(b_col) + _col_to_row(i_col), -jnp.inf)
            m_dmat = jnp.max(dmat, axis=-1, keepdims=True)
            a_st = b_last - b_col + i_col
            m_loc = jnp.max(a_st, axis=0, keepdims=True)
            w_st.append(jnp.exp(a_st - m_loc))
            scores = _dot_nt(_bf(q * masks[h]), kb) * jnp.exp(dmat - m_dmat)
            num = num + _dot(_bf(scores), _bf(v * masks[h]))
            heads.append(dict(b_col=b_col, b_last=b_last, m_loc=m_loc, m_dmat=m_dmat,
                              rowsum=jnp.sum(scores, axis=-1, keepdims=True)))
        wst = _expand(w_st, HEAD_DIM)
        return dict(rows=rows, q=q, num=num, heads=heads,
                    c_new=bd * _dot_tn(_bf(v * wst), kb),
                    n_new=jnp.sum(k * wst, axis=0, keepdims=True))

    def chunk(_):
        for p in [prepare(c) for c in range(tb // L)]:
            rows, q = p["rows"], p["q"]
            m_state = m_ref[...]
            w_inter, corr, rowsum, eneg, s_old, s_new, m_new = [], [], [], [], [], [], []
            for h, hd in enumerate(p["heads"]):
                m_prev = m_state[:, h:h + 1]
                mn = jnp.maximum(hd["b_last"] + m_prev, hd["m_loc"])
                m_new.append(mn)
                s_old.append(jnp.exp(hd["b_last"] + m_prev - mn))
                s_new.append(jnp.exp(hd["m_loc"] - mn))
                m_inter = hd["b_col"] + m_prev
                m_row = jnp.maximum(m_inter, hd["m_dmat"])
                w_inter.append(jnp.exp(m_inter - m_row))
                cr = jnp.exp(hd["m_dmat"] - m_row)
                corr.append(cr)
                rowsum.append(cr * hd["rowsum"])
                eneg.append(jnp.exp(-m_row))
            c_prev = c_ref[...]
            n_prev = n_ref[...]
            wi = _expand(w_inter, HEAD_DIM)
            fin[0, rows, :] = _expand(corr, HEAD_DIM) * p["num"] + wi * _dot_nt(_bf(q), _bf(c_prev))
            fin[1, rows, :] = wi
            fin[2, rows, :] = q * n_prev
            fin[3, rows, :] = _expand(rowsum, HEAD_DIM)
            fin[4, rows, :] = _expand(eneg, HEAD_DIM)
            so = _expand(s_old, HEAD_DIM)
            sn = _expand(s_new, HEAD_DIM)
            c_ref[...] = so * c_prev + sn * p["c_new"]
            n_ref[...] = so * n_prev + sn * p["n_new"]
            lane = _iota((1, 128), 1)
            m_vec = jnp.zeros((1, 128), F32)
            for h in range(N_HEADS):
                m_vec = jnp.where(lane == h, m_new[h], m_vec)
            m_ref[...] = m_vec

    def finish():
        den = fin[1] * _dot_rx(fin[2], bd, 3) + fin[3]
        hval = fin[0] / jnp.maximum(jnp.abs(den), fin[4])
        hv = hval * _sigmoid(og_all)
        mean = _dot_rx(hv, segmean, 2)
        xc = hv - mean
        var = _dot_rx(xc * xc, segmean, 2)
        o_ref[...] = (xc * lax.rsqrt(var + NORM_EPS) * ng_ref[...]).astype(o_ref.dtype)

    return 1, chunk, finish


def _mlstm_mixer(u, bsz, seq, conv_w, conv_b, i_b, f_b, norm_g):
    tb = MLSTM_TB
    ib = jnp.zeros((1, 128), F32).at[0, MISC_I:MISC_I + N_HEADS].set(i_b)
    fb = jnp.zeros((1, 128), F32).at[0, MISC_F:MISC_F + N_HEADS].set(f_b)
    return _launch_mixer(
        _mlstm_body, "mlstm_mixer", u, bsz, seq, tb,
        [(2 * GROUP_W, COL_ML_QK), (2 * GROUP_W, COL_ML_VO), (128, COL_MISC)],
        (conv_w, conv_b[None, :], ib, fb, norm_g[None, :]),
        [(tb + 8, 2 * GROUP_W), (GROUP_W, GROUP_W), (1, GROUP_W), (1, 128), (5, tb, GROUP_W)])


def _stack_heads(x, masks):
    return jnp.concatenate([x * m for m in masks], axis=0)


def _rwkv_body(rkv_ref, lora_ref, mu_rkv_ref, mu_lora_ref, w0_ref, wup_ref, a0_ref, aup_ref,
               gup_ref, kk_ref, ka_ref, rk_ref, lng_ref, lnb_ref, o_ref,
               rbuf, lbuf, s_ref, obuf, *, tb):
    W = GROUP_W
    rbuf[8:8 + tb, :] = rkv_ref[...]
    lbuf[8:8 + tb, :] = lora_ref[...]
    rkv = rkv_ref[...]
    lora = lora_ref[...]
    rkv = rkv + (rbuf[pl.ds(7, tb), :] - rkv) * mu_rkv_ref[...]
    lora = lora + (lbuf[pl.ds(7, tb), :] - lora) * mu_lora_ref[...]
    rbuf[0:8, :] = rbuf[tb:tb + 8, :]
    lbuf[0:8, :] = lbuf[tb:tb + 8, :]

    r_all, k_all, v_all = rkv[:, 0:W], rkv[:, W:2 * W], rkv[:, 2 * W:3 * W]
    wpre = w0_ref[...] + _pdot(jnp.tanh(lora), wup_ref[...], 3)
    lw_all = -jnp.exp(-_softplus(-wpre) - 0.5)
    a_all = _sigmoid(a0_ref[...] + _pdot(lora, aup_ref[...], 3))
    g_all = _pdot(_sigmoid(lora), gup_ref[...], 3)
    segsum = _seg_matrix(W, HEAD_DIM, 1.0)
    segmean = _seg_matrix(W, HEAD_DIM, 1.0 / HEAD_DIM)
    kk = k_all * kk_ref[...]
    kk = kk / jnp.maximum(jnp.sqrt(_dot_rx(kk * kk, segsum, 2)), 1e-12)
    k2_all = k_all * (1.0 + (a_all - 1.0) * ka_ref[...])
    av_all = -kk
    bv_all = kk * a_all

    L = CHUNK
    HL = N_HEADS * L
    tri = _tri_incl(L)
    masks = [_head_mask(W, HEAD_DIM, h) for h in range(N_HEADS)]
    t_idx = _iota((L, HL), 0)
    s_idx = jnp.bitwise_and(_iota((L, HL), 1), L - 1)
    strict = s_idx < t_idx
    incl = s_idx <= t_idx
    eye = (_iota((HL, HL), 0) == _iota((HL, HL), 1)).astype(F32)

    def prepare(c):
        rows = slice(c * L, (c + 1) * L)
        lw = lw_all[rows]
        cum = _dot_lx(tri, lw, 3)
        cum_last = cum[L - 1:L, :]
        w_inv = jnp.exp(-cum)
        w_dec = jnp.exp(cum_last - cum)
        k2, v = k2_all[rows], v_all[rows]
        rt = r_all[rows] * jnp.exp(cum)
        at = av_all[rows] * jnp.exp(cum - lw)
        bt_bd = _stack_heads(bv_all[rows] * w_inv, masks)
        kt_bd = _stack_heads(k2 * w_inv, masks)
        v_bd = _stack_heads(v, masks)
        sc = _pdot(jnp.concatenate([at, rt], axis=0), jnp.concatenate([bt_bd, kt_bd], axis=0),
                   RWKV_P_SCORE, "nt")
        ab = jnp.where(strict, sc[0:L, 0:HL], 0.0)
        ak = jnp.where(strict, sc[0:L, HL:2 * HL], 0.0)
        rb = jnp.where(incl, sc[L:2 * L, 0:HL], 0.0)
        rk = jnp.where(incl, sc[L:2 * L, HL:2 * HL], 0.0)
        on_v = _pdot(jnp.concatenate([_stack_heads(ak, masks), rk], axis=0), v_bd, RWKV_P_RHS)
        dec_bd = jnp.concatenate([_stack_heads(bv_all[rows] * w_dec, masks),
                                  _stack_heads(k2 * w_dec, masks)], axis=0)
        return dict(rows=rows, p=_stack_heads(ab, masks), rb=rb, on_v=on_v, v_bd=v_bd,
                    dec_bd=dec_bd, s_decay=jnp.exp(cum_last),
                    lhs_s=jnp.concatenate([_stack_heads(at, masks), rt], axis=0))

    def inverse_steps(group):
        ps = [c["p"] for c in group]
        minvs = [eye + p for p in ps]
        for _ in range(5):
            ps = [_pdot(p, p, RWKV_P_INV) for p in ps]
            minvs = [m + _pdot(m, p, RWKV_P_INV) for m, p in zip(minvs, ps)]
            yield minvs

    def carry_state(c, minv):
        s_prev = s_ref[...]
        on_s = _pdot(c["lhs_s"], s_prev, RWKV_P_RHS, "nt")
        on = on_s + c["on_v"]
        sa_bd = _pdot(minv, on[0:HL], RWKV_P_RHS)
        obuf[c["rows"], :] = on[HL:HL + L] + _pdot(c["rb"], sa_bd, RWKV_P_OUT)
        s_ref[...] = s_prev * c["s_decay"] + _pdot(
            jnp.concatenate([sa_bd, c["v_bd"]], axis=0), c["dec_bd"], RWKV_P_STATE, "tn")

    def chunk(_):
        n, gsz = tb // L, RWKV_GROUP
        groups = [[prepare(c) for c in range(g, g + gsz)] for g in range(0, n, gsz)]
        minvs = list(inverse_steps(groups[0]))[-1]
        for gi, group in enumerate(groups):
            nxt = inverse_steps(groups[gi + 1]) if gi + 1 < len(groups) else iter(())
            nxt_minvs = None
            for c, minv in zip(group, minvs):
                carry_state(c, minv)
                nxt_minvs = next(nxt, nxt_minvs)
            for nxt_minvs in nxt:
                pass
            minvs = nxt_minvs

    def finish():
        o = obuf[...]
        mean = _dot_rx(o, segmean, 2)
        oc = o - mean
        var = _dot_rx(oc * oc, segmean, 2)
        on = oc * lax.rsqrt(var + RWKV_GN_EPS) * lng_ref[...] + lnb_ref[...]
        bonus = _dot_rx(r_all * k2_all * rk_ref[...], segsum, 2) * v_all
        o_ref[...] = ((on + bonus) * g_all).astype(o_ref.dtype)

    return 1, chunk, finish


def _rwkv_mixer(u, bsz, seq, mu, w0, w_up, a0, a_up, g_up, k_k, k_a, r_k, ln_g, ln_b):
    tb = RWKV_TB
    W = GROUP_W
    wup = jnp.zeros((128, W), F32).at[0:32].set(w_up)
    aup = jnp.zeros((128, W), F32).at[32:64].set(a_up)
    gup = jnp.zeros((128, W), F32).at[64:128].set(g_up)
    consts = (mu[None, :3 * W], mu[None, 3 * W:], w0[None, :], wup, a0[None, :], aup, gup,
              k_k[None, :], k_a[None, :], r_k.reshape(1, W), ln_g[None, :], ln_b[None, :])
    return _launch_mixer(
        _rwkv_body, "rwkv_mixer", u, bsz, seq, tb, [(3 * W, COL_RWKV), (128, COL_LORA)], consts,
        [(tb + 8, 3 * W), (tb + 8, 128), (W, W), (tb, W)])


def _outproj_body(x_ref, y0_ref, y1_ref, y2_ref, y3_ref, w_ref, g_ref, b_ref, rwt_ref, rb_ref,
                  of_ref, idx_ref, gate_ref, rank_ref, cnt_ref, carry_ref):
    W = GROUP_W
    mix = _dot(y0_ref[...], w_ref[0:W, :])
    mix = mix + _dot(y1_ref[...], w_ref[W:2 * W, :])
    mix = mix + _dot(y2_ref[...], w_ref[2 * W:3 * W, :])
    mix = mix + _dot(y3_ref[...], w_ref[3 * W:4 * W, :])
    out = _layer_norm(DEEPNORM_ALPHA * x_ref[...] + mix, g_ref[...], b_ref[...])
    of_ref[...] = out
    _route_tokens(out, rwt_ref, rb_ref, idx_ref, gate_ref, rank_ref, cnt_ref, carry_ref)


def _out_proj_ln_route(x, ys, w_bf, g, b, router_wt, router_b, tm=512):
    t, d = x.shape
    row = lambda i: (i, 0)
    const = lambda i: (0, 0)
    tok = lambda i: (0, i)
    return pl.pallas_call(
        _outproj_body,
        grid=(t // tm,),
        in_specs=[pl.BlockSpec((tm, d), row)] + [pl.BlockSpec((tm, GROUP_W), row)] * 4
        + [pl.BlockSpec((d, d), const), pl.BlockSpec((1, d), const), pl.BlockSpec((1, d), const),
           pl.BlockSpec((N_EXPERTS, d), const), pl.BlockSpec((N_EXPERTS, 128), const)],
        out_specs=[pl.BlockSpec((tm, d), row), pl.BlockSpec((TOP_K, tm), tok),
                   pl.BlockSpec((TOP_K, tm), tok), pl.BlockSpec((TOP_K, tm), tok),
                   pl.BlockSpec((N_EXPERTS, 128), const)],
        out_shape=[jax.ShapeDtypeStruct((t, d), F32),
                   jax.ShapeDtypeStruct((TOP_K, t), jnp.int32),
                   jax.ShapeDtypeStruct((TOP_K, t), F32),
                   jax.ShapeDtypeStruct((TOP_K, t), jnp.int32),
                   jax.ShapeDtypeStruct((N_EXPERTS, 128), F32)],
        scratch_shapes=[pltpu.VMEM((N_EXPERTS, 128), F32)],
        compiler_params=_cparams(("arbitrary",)),
        name="out_proj_ln_route",
    )(x, *ys, w_bf, g[None, :], b[None, :], router_wt,
      jnp.broadcast_to(router_b[:, None], (N_EXPERTS, 128)))


def _route_tokens(x, wt_ref, b_ref, idx_ref, gate_ref, rank_ref, cnt_ref, carry_ref):
    tr = x.shape[0]

    @pl.when(pl.program_id(0) == 0)
    def _():
        carry_ref[...] = jnp.zeros(carry_ref.shape, F32)

    logits = _pdot(wt_ref[...], x, 3, "nt") + b_ref[...][:, 0:1]
    e_iota = _iota((N_EXPERTS, tr), 0)
    work = logits
    onehot = jnp.zeros((N_EXPERTS, tr), F32)
    sels, vals, idxs = [], [], []
    for _ in range(TOP_K):
        m = jnp.max(work, axis=0, keepdims=True)
        idx = jnp.min(jnp.where(work == m, e_iota, N_EXPERTS), axis=0, keepdims=True)
        sel = e_iota == idx
        work = jnp.where(sel, -jnp.inf, work)
        onehot = onehot + sel.astype(F32)
        sels.append(sel)
        vals.append(m)
        idxs.append(idx)
    exps = [jnp.exp(v - vals[0]) for v in vals]
    tot = exps[0] + exps[1] + exps[2] + exps[3]
    upper = (_iota((tr, tr), 0) < _iota((tr, tr), 1)).astype(BF16)
    carry = carry_ref[...][:, 0:1]
    before = _dot(_bf(onehot), upper) + carry
    ranks = [jnp.sum(jnp.where(s, before, 0.0), axis=0, keepdims=True) for s in sels]
    idx_ref[...] = jnp.concatenate(idxs, axis=0)
    gate_ref[...] = jnp.concatenate([e / tot for e in exps], axis=0)
    rank_ref[...] = jnp.concatenate(ranks, axis=0).astype(jnp.int32)
    new_carry = carry + jnp.sum(onehot, axis=1, keepdims=True)
    carry_ref[...] = jnp.broadcast_to(new_carry, carry_ref.shape)
    cnt_ref[...] = jnp.broadcast_to(new_carry, cnt_ref.shape)


def _route_plan_body(idx_ref, rank_ref, cnt_ref, dest_ref, be_ref, nv_ref, ps_ref, pl_ref, vb_ref,
                     *, blk, nb_pad):
    E = N_EXPERTS
    cnt = cnt_ref[...][:, 0:1]
    padded = jnp.floor((cnt + (blk - 1)) / blk) * blk
    lower = (_iota((E, E), 1) < _iota((E, E), 0)).astype(F32)
    pstart = jnp.sum(lower * _col_to_row(padded), axis=1, keepdims=True)
    pend = pstart + padded
    ps_ref[...] = jnp.broadcast_to(pstart + cnt, ps_ref.shape).astype(jnp.int32)
    pl_ref[...] = jnp.broadcast_to(padded - cnt, pl_ref.shape).astype(jnp.int32)
    idx = idx_ref[...]
    dest = rank_ref[...].astype(F32)
    for e in range(E):
        dest = dest + jnp.where(idx == e, pstart[e:e + 1, 0:1], 0.0)
    dest_ref[...] = dest.astype(jnp.int32)
    blk_start = (_iota((1, nb_pad), 1) * blk).astype(F32)
    be = jnp.sum((pend <= blk_start).astype(F32), axis=0, keepdims=True)
    be_ref[...] = jnp.minimum(be, E - 1).astype(jnp.int32)
    nv_ref[...] = jnp.broadcast_to(pend[E - 1:E, 0:1] / blk, nv_ref.shape).astype(jnp.int32)
    owns = (pstart <= blk_start) & (blk_start < pend)
    rows = jnp.clip(pstart + cnt - blk_start, 0.0, float(blk))
    vb_ref[...] = jnp.sum(jnp.where(owns, rows, 0.0), axis=0, keepdims=True).astype(jnp.int32)


def _route_plan(idx_t, rank_t, counts, blk, nb_pad):
    t = idx_t.shape[1]
    return pl.pallas_call(
        functools.partial(_route_plan_body, blk=blk, nb_pad=nb_pad),
        out_shape=[jax.ShapeDtypeStruct((TOP_K, t), jnp.int32),
                   jax.ShapeDtypeStruct((1, nb_pad), jnp.int32),
                   jax.ShapeDtypeStruct((1, 128), jnp.int32),
                   jax.ShapeDtypeStruct((N_EXPERTS, 128), jnp.int32),
                   jax.ShapeDtypeStruct((N_EXPERTS, 128), jnp.int32),
                   jax.ShapeDtypeStruct((1, nb_pad), jnp.int32)],
        compiler_params=pltpu.CompilerParams(vmem_limit_bytes=VMEM_LIMIT),
        name="route_plan",
    )(idx_t, rank_t, counts)


def _dispatch_body(ps_ref, pl_ref, nv_ref, dest_ref, x_ref, xs_hbm, zbuf, sem, fsem,
                   *, td, blk, n_blocks):
    sizes = [blk >> (s + 1) for s in range(int(np.log2(blk)) - 3)]

    def pad_piece(e, s):
        n = pl_ref[e]
        first = ps_ref[e] + jnp.bitwise_and(n, 7)
        off = pl.multiple_of(first + jnp.bitwise_and(n, -2 * s), 8)
        return (jnp.bitwise_and(n, s) != 0,
                pltpu.make_async_copy(zbuf.at[pl.ds(0, s)], xs_hbm.at[pl.ds(off, s)], fsem))

    def pad_row(e, r):
        return (r < jnp.bitwise_and(pl_ref[e], 7),
                pltpu.make_async_copy(zbuf.at[pl.ds(0, 1)], xs_hbm.at[pl.ds(ps_ref[e] + r, 1)], fsem))

    def tail_copy(b):
        off = pl.multiple_of(b * blk, blk)
        return pltpu.make_async_copy(zbuf, xs_hbm.at[pl.ds(off, blk)], fsem)

    @pl.when(pl.program_id(0) == 0)
    def _():
        zbuf[...] = jnp.zeros(zbuf.shape, zbuf.dtype)

        def pads(start):
            def body(e, carry):
                for cond, cp in [pad_piece(e, s) for s in sizes] + [pad_row(e, r) for r in range(7)]:
                    @pl.when(cond)
                    def _():
                        cp.start() if start else cp.wait()
                return carry
            lax.fori_loop(0, N_EXPERTS, body, 0)

        def tails(start):
            def body(b, carry):
                tail_copy(b).start() if start else tail_copy(b).wait()
                return carry
            lax.fori_loop(nv_ref[0], n_blocks, body, 0)

        pads(True)
        tails(True)
        pads(False)
        tails(False)

    def issue(g, carry):
        for s in range(8):
            for k in range(TOP_K):
                row = dest_ref[g * (8 * TOP_K) + (s * TOP_K + k)]
                pltpu.make_async_copy(x_ref.at[g, pl.ds(s, 1)],
                                      xs_hbm.at[pl.ds(row, 1)], sem).start(priority=k % 2)
        return carry

    lax.fori_loop(0, td // 8, issue, 0)
    for _ in range(TOP_K):
        pltpu.make_async_copy(xs_hbm.at[pl.ds(0, td)], xs_hbm.at[pl.ds(0, td)], sem).wait()


def _dispatch(pad_start, pad_len, nv, dest_flat, x, n_rows, blk, td=1024):
    t, d = x.shape
    grid_spec = pltpu.PrefetchScalarGridSpec(
        num_scalar_prefetch=3,
        grid=(t // td,),
        in_specs=[pl.BlockSpec((td * TOP_K,), lambda i, *_: (i,), memory_space=pltpu.SMEM),
                  pl.BlockSpec((td // 8, 8, d), lambda i, *_: (i, 0, 0))],
        out_specs=pl.BlockSpec(memory_space=pl.ANY),
        scratch_shapes=[pltpu.VMEM((blk, d), x.dtype), pltpu.SemaphoreType.DMA(()),
                        pltpu.SemaphoreType.DMA(())],
    )
    return pl.pallas_call(
        functools.partial(_dispatch_body, td=td, blk=blk, n_blocks=n_rows // blk),
        grid_spec=grid_spec,
        out_shape=jax.ShapeDtypeStruct((n_rows, d), x.dtype),
        compiler_params=_cparams(("arbitrary",)),
        name="dispatch",
    )(pad_start, pad_len, nv, dest_flat, x.reshape(t // 8, 8, d))


def _expert_body(be_ref, nv_ref, vb_ref, xs_ref, wgu_ref, bgu_ref, wd_ref, bd_ref, ys_ref,
                 wgu_bf, wd_bf):
    b = pl.program_id(0)
    d = D_MODEL
    blk = xs_ref.shape[0]
    half = blk // 2
    valid = b < nv_ref[0]

    @pl.when(valid)
    def _():
        prev = be_ref[jnp.maximum(b - 1, 0)]

        @pl.when((b == 0) | (be_ref[b] != prev))
        def _():
            wgu_bf[...] = wgu_ref[...].astype(BF16)
            wd_bf[...] = wd_ref[...].astype(BF16)

    def ffn(rows):
        h = _dot(_bf(xs_ref[rows, :]), wgu_bf[...]) + bgu_ref[...]
        hg = jnp.minimum(h[:, 0:d], SWIGLU_LIMIT)
        hl = jnp.clip(h[:, d:2 * d], -SWIGLU_LIMIT, SWIGLU_LIMIT)
        act = (hl + 1.0) * (hg * _sigmoid(hg * SWIGLU_ALPHA))
        ys_ref[rows, :] = _dot(_bf(act), wd_bf[...]) + bd_ref[...]

    @pl.when(valid & (vb_ref[b] > half))
    def _():
        ffn(slice(0, blk))

    @pl.when(valid & (vb_ref[b] <= half))
    def _():
        ffn(slice(0, half))
        ys_ref[half:blk, :] = jnp.zeros((blk - half, d), F32)

    @pl.when(jnp.logical_not(valid))
    def _():
        ys_ref[...] = jnp.zeros(ys_ref.shape, F32)


def _experts(be, nv, vb, xs, w_gu, b_gu, w_down, b_down, layer, blk):
    n_rows, d = xs.shape
    nb = n_rows // blk

    def row(b, be_r, nv_r, vb_r):
        return (jnp.minimum(b, nv_r[0] - 1), 0)

    def wsel(b, be_r, nv_r, vb_r):
        return (layer, be_r[jnp.minimum(b, nv_r[0] - 1)], 0, 0)

    grid_spec = pltpu.PrefetchScalarGridSpec(
        num_scalar_prefetch=3,
        grid=(nb,),
        in_specs=[pl.BlockSpec((blk, d), row),
                  pl.BlockSpec((None, None, d, 2 * d), wsel),
                  pl.BlockSpec((None, None, 1, 2 * d), wsel),
                  pl.BlockSpec((None, None, d, d), wsel),
                  pl.BlockSpec((None, None, 1, d), wsel)],
        out_specs=pl.BlockSpec((blk, d), lambda b, be_r, nv_r, vb_r: (b, 0)),
        scratch_shapes=[pltpu.VMEM((d, 2 * d), BF16), pltpu.VMEM((d, d), BF16)],
    )
    return pl.pallas_call(
        _expert_body,
        grid_spec=grid_spec,
        out_shape=jax.ShapeDtypeStruct((n_rows, d), F32),
        compiler_params=_cparams(("arbitrary",)),
        name="experts",
    )(be, nv, vb, xs, w_gu, b_gu, w_down, b_down)


def _combine_body(dcur_ref, dnext_ref, ys_hbm, x_ref, gate_ref, p_ref, wg_ref, wp_ref, g_ref, b_ref,
                  of_ref, ob_ref, ybuf, sem, *, tc):
    i = pl.program_id(0)
    n = pl.num_programs(0)
    slot = lax.rem(i, 2)

    def issue(d_ref, s):
        def body(g, carry):
            for r in range(8):
                for k in range(TOP_K):
                    row = d_ref[g * (8 * TOP_K) + (r * TOP_K + k)]
                    pltpu.make_async_copy(ys_hbm.at[pl.ds(row, 1)], ybuf.at[s, k, g, pl.ds(r, 1)],
                                          sem.at[s]).start(priority=k % 2)
            return carry
        lax.fori_loop(0, tc // 8, body, 0)

    @pl.when(i == 0)
    def _():
        issue(dcur_ref, 0)

    for s in range(2):
        @pl.when((i + 1 < n) & (slot == 1 - s))
        def _():
            issue(dnext_ref, s)

    for k in range(TOP_K):
        pltpu.make_async_copy(ys_hbm.at[pl.ds(0, tc)], ys_hbm.at[pl.ds(0, tc)], sem.at[slot]).wait()

    gate = gate_ref[...]
    d = x_ref.shape[-1]
    ffn = ybuf[slot, 0].reshape(tc, d) * gate[:, 0:1]
    for k in range(1, TOP_K):
        ffn = ffn + ybuf[slot, k].reshape(tc, d) * gate[:, k:k + 1]
    h = DEEPNORM_ALPHA * x_ref[...] + ffn
    ple = _dot(_bf(p_ref[...]), wp_ref[...])
    h = h + _sigmoid(_dot(_bf(h), wg_ref[...])) * ple
    out = _layer_norm(h, g_ref[...], b_ref[...])
    of_ref[...] = out
    ob_ref[...] = out.astype(BF16)


def _combine(dest_flat, ys, x, gate, p, wg_bf, wp_bf, g, b, layer, tc=512):
    t, d = x.shape
    n = t // tc
    row = lambda i: (i, 0)
    const = lambda i: (0, 0)
    return pl.pallas_call(
        functools.partial(_combine_body, tc=tc),
        grid=(n,),
        in_specs=[pl.BlockSpec((tc * TOP_K,), lambda i: (i,), memory_space=pltpu.SMEM),
                  pl.BlockSpec((tc * TOP_K,), lambda i: (jnp.minimum(i + 1, n - 1),),
                               memory_space=pltpu.SMEM),
                  pl.BlockSpec(memory_space=pl.ANY),
                  pl.BlockSpec((tc, d), row),
                  pl.BlockSpec((tc, TOP_K), row),
                  pl.BlockSpec((None, tc, PLE_DIM), lambda i: (layer, i, 0)),
                  pl.BlockSpec((d, d), const),
                  pl.BlockSpec((PLE_DIM, d), const),
                  pl.BlockSpec((1, d), const),
                  pl.BlockSpec((1, d), const)],
        out_specs=[pl.BlockSpec((tc, d), row), pl.BlockSpec((tc, d), row)],
        out_shape=[jax.ShapeDtypeStruct((t, d), F32), jax.ShapeDtypeStruct((t, d), BF16)],
        scratch_shapes=[pltpu.VMEM((2, TOP_K, tc // 8, 8, d), F32), pltpu.SemaphoreType.DMA((2,))],
        compiler_params=_cparams(("arbitrary",)),
        name="combine",
    )(dest_flat, dest_flat, ys, x, gate, p, wg_bf, wp_bf, g[None, :], b[None, :])


def kernel(x, p, w_in, w_out, ln1_g, ln1_b, ssd_conv_w, ssd_conv_b, ssd_a_log, ssd_dt_bias, ssd_d, ssd_norm_g, rwkv_mu, rwkv_w0, rwkv_w_up, rwkv_a0, rwkv_a_up, rwkv_g_up, rwkv_k_k, rwkv_k_a, rwkv_r_k, rwkv_ln_g, rwkv_ln_b, gla_gate_up, gla_gate_b, gla_norm_g, mlstm_conv_w, mlstm_conv_b, mlstm_i_b, mlstm_f_b, mlstm_norm_g, router_w, router_b, exp_w_gu, exp_b_gu, exp_w_down, exp_b_down, ple_gate_w, ple_proj, ln2_g, ln2_b):
    bsz, seq, d = x.shape
    t = bsz * seq
    depth = w_in.shape[0]
    blk = EXPERT_BLK
    n_blocks = -(-(t * TOP_K + N_EXPERTS * (blk - 1)) // blk)
    n_rows = n_blocks * blk
    nb_pad = -(-n_blocks // 128) * 128

    xf = x.reshape(t, d)
    xb = xf.astype(BF16)
    p2 = p.reshape(depth, t, PLE_DIM)
    b_gu4 = exp_b_gu[:, :, None, :]
    b_down4 = exp_b_down[:, :, None, :]
    w_packed = _pack_w_in(w_in)
    for i in range(depth):
        u = _in_proj(xb, w_packed, i)
        y_ssd = _ssd_mixer(u, bsz, seq, ssd_conv_w[i], ssd_conv_b[i], ssd_a_log[i], ssd_dt_bias[i],
                           ssd_d[i], ssd_norm_g[i])
        y_rwkv = _rwkv_mixer(u, bsz, seq, rwkv_mu[i], rwkv_w0[i], rwkv_w_up[i], rwkv_a0[i],
                             rwkv_a_up[i], rwkv_g_up[i], rwkv_k_k[i], rwkv_k_a[i], rwkv_r_k[i],
                             rwkv_ln_g[i], rwkv_ln_b[i])
        y_gla = _gla_mixer(u, bsz, seq, gla_gate_up[i], gla_gate_b[i], gla_norm_g[i])
        y_ml = _mlstm_mixer(u, bsz, seq, mlstm_conv_w[i], mlstm_conv_b[i], mlstm_i_b[i],
                            mlstm_f_b[i], mlstm_norm_g[i])
        x1f, idx_t, gate_t, rank_t, counts = _out_proj_ln_route(
            xf, (y_ssd, y_rwkv, y_gla, y_ml), w_out[i].astype(BF16), ln1_g[i], ln1_b[i],
            router_w[i].T, router_b[i])
        dest_t, be, nv, pad_start, pad_len, vb = _route_plan(idx_t, rank_t, counts, blk, nb_pad)
        nv = nv.reshape(128)
        dest_flat = dest_t.T.reshape(t * TOP_K)
        xs = _dispatch(pad_start[:, 0], pad_len[:, 0], nv, dest_flat, x1f, n_rows, blk)
        ys = _experts(be.reshape(nb_pad), nv, vb.reshape(nb_pad), xs, exp_w_gu, b_gu4, exp_w_down,
                      b_down4, i, blk)
        xf, xb = _combine(dest_flat, ys, x1f, gate_t.T, p2, ple_gate_w[i].astype(BF16),
                          ple_proj[i].astype(BF16), ln2_g[i], ln2_b[i], i)
    return xf.reshape(bsz, seq, d)
```

```python
import functools

import jax
import jax.numpy as jnp
import numpy as np
from jax import lax
from jax.experimental import pallas as pl
from jax.experimental.pallas import tpu as pltpu

F32 = jnp.float32
BF16 = jnp.bfloat16

D_MODEL = 1024
DEPTH = 4
GROUP_W = 256
N_HEADS = 4
HEAD_DIM = 64
SSD_STATE = 128
SSD_CHUNK = 128
GLA_DK = 128
GLA_HEAD_K = 32
GLA_TAU = 16.0
CHUNK = 64
RWKV_P_SCORE = 1
RWKV_P_INV = 1
RWKV_P_RHS = 1
RWKV_P_OUT = 1
RWKV_P_STATE = 1
RWKV_GN_EPS = 64e-5
NORM_EPS = 1e-5
LN_EPS = 1e-5
N_EXPERTS = 32
TOP_K = 4
SWIGLU_LIMIT = 7.0
SWIGLU_ALPHA = 1.702
PLE_DIM = 256
DEEPNORM_ALPHA = (2 * DEPTH) ** 0.25

U_WIDTH = 3584
COL_SSD_Z, COL_SSD_X, COL_SSD_BC = 0, 256, 512
COL_RWKV = 768
COL_ML_QK = 1536
COL_ML_VO = 2048
COL_GLA_QK, COL_GLA_V, COL_GLA_OG = 2560, 2816, 3072
COL_LORA = 3328
COL_MISC = 3456
MISC_DT, MISC_GLA, MISC_I, MISC_F = 0, 4, 20, 24

SSD_TB = GLA_TB = RWKV_TB = 512
MLSTM_TB = 256
RWKV_GROUP = 4
EXPERT_BLK = 512
V7X_VMEM_BYTES = 64 * 1024 * 1024
VMEM_LIMIT = V7X_VMEM_BYTES * 7 // 8


def _cparams(sem):
    return pltpu.CompilerParams(dimension_semantics=sem, vmem_limit_bytes=VMEM_LIMIT)


_DIMS = {"nn": (((1,), (0,)), ((), ())), "nt": (((1,), (1,)), ((), ())),
         "tn": (((0,), (0,)), ((), ()))}


def _dot(a, b):
    return lax.dot_general(a, b, _DIMS["nn"], preferred_element_type=F32)


def _dot_nt(a, b):
    return lax.dot_general(a, b, _DIMS["nt"], preferred_element_type=F32)


def _dot_tn(a, b):
    return lax.dot_general(a, b, _DIMS["tn"], preferred_element_type=F32)


def _bf(x):
    return x.astype(BF16)


def _split_bf(x):
    hi = x.astype(BF16)
    return hi, (x - hi.astype(F32)).astype(BF16)


def _pdot(a, b, passes, kind="nn"):
    assert passes in (1, 3)
    dg = lambda x, y: lax.dot_general(x, y, _DIMS[kind], preferred_element_type=F32)
    if passes == 1:
        return dg(_bf(a), _bf(b))
    ah, al = _split_bf(a)
    bh, bl = _split_bf(b)
    return dg(ah, bh) + (dg(ah, bl) + dg(al, bh))


def _pieces(x, n):
    out, rem = [], x
    for _ in range(n):
        part = rem.astype(BF16)
        out.append(part)
        rem = rem - part.astype(F32)
    return out


def _dot_lx(a_exact, b, n):
    ab = _bf(a_exact)
    acc = None
    for part in _pieces(b, n):
        term = lax.dot_general(ab, part, _DIMS["nn"], preferred_element_type=F32)
        acc = term if acc is None else acc + term
    return acc


def _dot_rx(a, b_exact, n):
    bb = _bf(b_exact)
    acc = None
    for part in _pieces(a, n):
        term = lax.dot_general(part, bb, _DIMS["nn"], preferred_element_type=F32)
        acc = term if acc is None else acc + term
    return acc


def _sigmoid(x):
    return 1.0 / (1.0 + jnp.exp(-x))


def _silu(x):
    return x * _sigmoid(x)


def _softplus(x):
    return jnp.maximum(x, 0.0) + jnp.log1p(jnp.exp(-jnp.abs(x)))


def _log_sigmoid(x):
    return jnp.minimum(x, 0.0) - jnp.log1p(jnp.exp(-jnp.abs(x)))


def _iota(shape, dim):
    return lax.broadcasted_iota(jnp.int32, shape, dim)


def _tri_incl(n):
    return (_iota((n, n), 0) >= _iota((n, n), 1)).astype(F32)


def _seg_matrix(n, seg, value):
    sh = int(np.log2(seg))
    same = (lax.shift_right_logical(_iota((n, n), 0), sh)
            == lax.shift_right_logical(_iota((n, n), 1), sh))
    return jnp.where(same, value, 0.0).astype(F32)


def _head_mask(width, seg, h):
    lane = _iota((1, width), 1)
    return ((lane >= h * seg) & (lane < (h + 1) * seg)).astype(F32)


def _expand(cols, seg):
    n = len(cols)
    rows = cols[0].shape[0]
    width = n * seg
    lane = _iota((rows, width), 1)
    out = jnp.broadcast_to(cols[n - 1], (rows, width))
    for h in range(n - 2, -1, -1):
        out = jnp.where(lane < (h + 1) * seg, jnp.broadcast_to(cols[h], (rows, width)), out)
    return out


def _col_to_row(col):
    n = col.shape[0]
    eye = _iota((n, n), 0) == _iota((n, n), 1)
    return jnp.sum(jnp.where(eye, col, 0.0), axis=0, keepdims=True)


def _layer_norm(x, g, b):
    mu = jnp.mean(x, axis=-1, keepdims=True)
    xc = x - mu
    var = jnp.mean(xc * xc, axis=-1, keepdims=True)
    return xc * lax.rsqrt(var + LN_EPS) * g + b


def _causal_conv_silu(buf, w_ref, b_ref, rows):
    acc = b_ref[...]
    for k in range(4):
        acc = acc + buf[pl.ds(5 + k, rows), :] * w_ref[k:k + 1, :]
    return _silu(acc)


_W_IN_COLS = 3484
_W_RWKV, _W_GLA, _W_ML = 772, 772 + 896, 772 + 896 + 784
_PACK_SEGMENTS = (
    (COL_SSD_Z, 0, 768),
    (COL_RWKV, _W_RWKV, 768),
    (COL_ML_QK, _W_ML, 768),
    (COL_ML_VO + 256, _W_ML + 776, 256),
    (COL_GLA_QK, _W_GLA, 512),
    (COL_GLA_OG, _W_GLA + 528, 256),
    (COL_LORA, _W_RWKV + 768, 128),
)
_PACK_MISC = ((768, MISC_DT, 4), (_W_GLA + 512, MISC_GLA, 16), (_W_ML + 768, MISC_I, 8))


def _pack_body(w_ref, o_ref):
    def cols(src, n):
        sh = src % 128
        a0 = src - sh
        if sh == 0:
            return w_ref[:, a0:a0 + n]
        wd = -(-(sh + n) // 128) * 128
        return pltpu.roll(w_ref[:, a0:a0 + wd], wd - sh, axis=1)[:, 0:n]

    for dst, src, n in _PACK_SEGMENTS:
        o_ref[:, dst:dst + n] = cols(src, n).astype(BF16)
    lane = _iota((w_ref.shape[0], 128), 1)
    misc = jnp.zeros((w_ref.shape[0], 128), F32)
    for src, lane0, n in _PACK_MISC:
        assert src % 128 == lane0
        a0 = src - lane0
        misc = jnp.where((lane >= lane0) & (lane < lane0 + n), w_ref[:, a0:a0 + 128], misc)
    o_ref[:, COL_MISC:COL_MISC + 128] = misc.astype(BF16)


def _pack_w_in(w_in, rb=256):
    depth, d, n = w_in.shape
    assert n == _W_IN_COLS
    return pl.pallas_call(
        _pack_body,
        grid=(depth, d // rb),
        in_specs=[pl.BlockSpec((None, rb, U_WIDTH), lambda l, r: (l, r, 0))],
        out_specs=pl.BlockSpec((None, rb, U_WIDTH), lambda l, r: (l, r, 0)),
        out_shape=jax.ShapeDtypeStruct((depth, d, U_WIDTH), BF16),
        compiler_params=_cparams(("arbitrary", "arbitrary")),
        name="pack_w_in",
    )(w_in)


def _in_proj_body(x_ref, w_ref, o_ref):
    o_ref[...] = jnp.dot(x_ref[...], w_ref[...], preferred_element_type=F32)


def _in_proj(x_bf, w_all, layer, tm=512, tn=1792):
    t, k = x_bf.shape
    n = w_all.shape[2]
    return pl.pallas_call(
        _in_proj_body,
        grid=(n // tn, t // tm),
        in_specs=[pl.BlockSpec((tm, k), lambda j, i: (i, 0)),
                  pl.BlockSpec((None, k, tn), lambda j, i: (layer, 0, j))],
        out_specs=pl.BlockSpec((tm, tn), lambda j, i: (i, j)),
        out_shape=jax.ShapeDtypeStruct((t, n), F32),
        compiler_params=_cparams(("arbitrary", "arbitrary")),
        name="in_proj",
    )(x_bf, w_all)


def _ssd_body(z_ref, x_ref, bc_ref, misc_ref, cwx_ref, cwbc_ref, cbx_ref, cbbc_ref,
              alog_ref, dtb_ref, dskip_ref, ng_ref, o_ref, xbuf, bcbuf, st_ref, *, tb):
    xbuf[8:8 + tb, :] = x_ref[...]
    bcbuf[8:8 + tb, :] = bc_ref[...]
    xs_all = _causal_conv_silu(xbuf, cwx_ref, cbx_ref, tb)
    bc_all = _causal_conv_silu(bcbuf, cwbc_ref, cbbc_ref, tb)
    xbuf[0:8, :] = xbuf[tb:tb + 8, :]
    bcbuf[0:8, :] = bcbuf[tb:tb + 8, :]

    dt_all = _softplus(misc_ref[...] + dtb_ref[...])
    adt_all = dt_all * (-jnp.exp(alog_ref[...]))
    z_all = z_ref[...]
    L = SSD_CHUNK
    tri = _tri_incl(L)
    causal = _iota((L, L), 0) >= _iota((L, L), 1)
    masks = [_head_mask(GROUP_W, HEAD_DIM, h) for h in range(N_HEADS)]

    def chunk(c):
        rows = slice(c * L, (c + 1) * L)
        xs = xs_all[rows]
        bm = _bf(bc_all[rows, 0:SSD_STATE])
        cm = _bf(bc_all[rows, SSD_STATE:2 * SSD_STATE])
        dt = dt_all[rows]
        acum = _dot_lx(tri, adt_all[rows], 3)
        cols = [acum[:, h:h + 1] for h in range(N_HEADS)]
        lasts = [acum[L - 1:L, h:h + 1] for h in range(N_HEADS)]
        xc = xs * _expand([dt[:, h:h + 1] for h in range(N_HEADS)], HEAD_DIM)
        g = _dot_nt(cm, bm)
        y = jnp.zeros((L, GROUP_W), F32)
        for h in range(N_HEADS):
            lmat = jnp.exp(jnp.where(causal, cols[h] - _col_to_row(cols[h]), -jnp.inf))
            y = y + _dot(_bf(g * lmat), _bf(xc * masks[h]))
        dec_states = _expand([jnp.exp(lasts[h] - cols[h]) for h in range(N_HEADS)], HEAD_DIM)
        st_prev = st_ref[...]
        y = y + _dot(cm, _bf(st_prev)) * _expand([jnp.exp(cols[h]) for h in range(N_HEADS)], HEAD_DIM)
        st_new = _dot_tn(bm, _bf(xc * dec_states))
        st_ref[...] = st_prev * _expand([jnp.exp(lasts[h]) for h in range(N_HEADS)], HEAD_DIM) + st_new
        y = y + xs * dskip_ref[...]
        y = y * _silu(z_all[rows])
        ms = jnp.mean(y * y, axis=-1, keepdims=True)
        o_ref[rows, :] = (y * lax.rsqrt(ms + NORM_EPS) * ng_ref[...]).astype(o_ref.dtype)

    return tb // L, chunk, lambda: None


def _launch_mixer(body, name, u, bsz, seq, tb, u_blocks, consts, scratch):
    u3 = u.reshape(bsz, seq, u.shape[-1])
    n_in, n_c = len(u_blocks), len(consts)

    def step(*refs):
        scr = refs[n_in + n_c + 1:]

        @pl.when(pl.program_id(1) == 0)
        def _():
            for r in scr:
                r[...] = jnp.zeros(r.shape, F32)

        n_chunks, chunk, finish = body(*refs, tb=tb)
        for c in range(n_chunks):
            chunk(c)
        finish()

    in_specs = [pl.BlockSpec((None, tb, w), functools.partial(lambda g, j, cb: (g, j, cb), cb=c // w))
                for w, c in u_blocks]
    in_specs += [pl.BlockSpec(a.shape, functools.partial(lambda g, j, nd: (0,) * nd, nd=a.ndim))
                 for a in consts]
    out = pl.pallas_call(
        step,
        grid=(bsz, seq // tb),
        in_specs=in_specs,
        out_specs=pl.BlockSpec((None, tb, GROUP_W), lambda g, j: (g, j, 0)),
        out_shape=jax.ShapeDtypeStruct((bsz, seq, GROUP_W), BF16),
        scratch_shapes=[pltpu.VMEM(s, F32) for s in scratch],
        compiler_params=_cparams(("arbitrary", "arbitrary")),
        name=name,
    )(*([u3] * n_in), *consts)
    return out.reshape(bsz * seq, GROUP_W)


def _ssd_mixer(u, bsz, seq, conv_w, conv_b, a_log, dt_bias, d_skip, norm_g):
    tb = SSD_TB
    pad4 = lambda v: jnp.zeros((1, 128), F32).at[0, :N_HEADS].set(v)
    consts = (conv_w[:, :GROUP_W], conv_w[:, GROUP_W:],
              conv_b[None, :GROUP_W], conv_b[None, GROUP_W:],
              pad4(a_log), pad4(dt_bias),
              jnp.repeat(d_skip, HEAD_DIM)[None, :], norm_g[None, :])
    return _launch_mixer(
        _ssd_body, "ssd_mixer", u, bsz, seq, tb,
        [(GROUP_W, COL_SSD_Z), (GROUP_W, COL_SSD_X), (GROUP_W, COL_SSD_BC), (128, COL_MISC)],
        consts, [(tb + 8, GROUP_W), (tb + 8, GROUP_W), (SSD_STATE, GROUP_W)])


def _gla_body(qk_ref, v_ref, og_ref, misc_ref, gup_ref, gb_ref, ng_ref, o_ref, st_ref, obuf,
              *, tb):
    L = CHUNK
    q_all = qk_ref[:, 0:GLA_DK] * (GLA_HEAD_K ** -0.5)
    k_all = qk_ref[:, GLA_DK:2 * GLA_DK]
    v_all = v_ref[...]
    og_all = og_ref[...]
    gpre = _dot(_bf(misc_ref[...]), gup_ref[...]) + gb_ref[...]
    loga_all = _log_sigmoid(gpre) / GLA_TAU
    tri = _tri_incl(L)
    causal = _iota((L, L), 0) >= _iota((L, L), 1)
    kmasks = [_head_mask(GLA_DK, GLA_HEAD_K, h) for h in range(N_HEADS)]
    vmasks = [_head_mask(GROUP_W, HEAD_DIM, h) for h in range(N_HEADS)]
    bd = (lax.shift_right_logical(_iota((GROUP_W, GLA_DK), 0), 6)
          == lax.shift_right_logical(_iota((GROUP_W, GLA_DK), 1), 5))
    segmean = _seg_matrix(GROUP_W, HEAD_DIM, 1.0 / HEAD_DIM)

    def chunk(c):
        rows = slice(c * L, (c + 1) * L)
        bcum = _dot_lx(tri, loga_all[rows], 3)
        b_last = bcum[L - 1:L, :]
        q_dec = q_all[rows] * jnp.exp(bcum)
        k_inv = _bf(k_all[rows] * jnp.exp(-bcum))
        k_dec = _bf(k_all[rows] * jnp.exp(b_last - bcum))
        v = v_all[rows]
        st_prev = st_ref[...]
        o = _dot_nt(_bf(q_dec), _bf(st_prev))
        for h in range(N_HEADS):
            attn = jnp.where(causal, _dot_nt(_bf(q_dec * kmasks[h]), k_inv), 0.0)
            o = o + _dot(_bf(attn), _bf(v * vmasks[h]))
        contrib = jnp.where(bd, _dot_tn(_bf(v), k_dec), 0.0)
        st_ref[...] = st_prev * jnp.exp(b_last) + contrib
        obuf[rows, :] = o

    def finish():
        o = obuf[...]
        ms = _dot_rx(o * o, segmean, 2)
        out = o * lax.rsqrt(ms + NORM_EPS) * ng_ref[...] * _silu(og_all)
        o_ref[...] = out.astype(o_ref.dtype)

    return tb // L, chunk, finish


def _gla_mixer(u, bsz, seq, gate_up, gate_b, norm_g):
    gup = jnp.zeros((128, GLA_DK), F32).at[MISC_GLA:MISC_GLA + 16].set(gate_up).astype(BF16)
    return _launch_mixer(
        _gla_body, "gla_mixer", u, bsz, seq, GLA_TB,
        [(GROUP_W, COL_GLA_QK), (GROUP_W, COL_GLA_V), (GROUP_W, COL_GLA_OG), (128, COL_MISC)],
        (gup, gate_b[None, :], norm_g[None, :]),
        [(GROUP_W, GLA_DK), (GLA_TB, GROUP_W)])


def _mlstm_body(qk_ref, vo_ref, misc_ref, cw_ref, cb_ref, ib_ref, fb_ref, ng_ref, o_ref,
                qkbuf, c_ref, n_ref, m_ref, fin, *, tb):
    W = GROUP_W
    qkbuf[8:8 + tb, :] = qk_ref[...]
    qk = _causal_conv_silu(qkbuf, cw_ref, cb_ref, tb)
    qkbuf[0:8, :] = qkbuf[tb:tb + 8, :]
    q_all = qk[:, 0:W] * (HEAD_DIM ** -0.5)
    k_all = qk[:, W:2 * W]
    v_all = vo_ref[:, 0:W]
    og_all = vo_ref[:, W:2 * W]
    misc = misc_ref[...]
    ipre_all = misc + ib_ref[...]
    lf_all = _log_sigmoid(misc + fb_ref[...])

    L = CHUNK
    tri = _tri_incl(L)
    causal = _iota((L, L), 0) >= _iota((L, L), 1)
    masks = [_head_mask(W, HEAD_DIM, h) for h in range(N_HEADS)]
    bd = _seg_matrix(W, HEAD_DIM, 1.0)
    segmean = _seg_matrix(W, HEAD_DIM, 1.0 / HEAD_DIM)

    def prepare(c):
        rows = slice(c * L, (c + 1) * L)
        q, k, v = q_all[rows], k_all[rows], v_all[rows]
        kb = _bf(k)
        bcs = _dot_lx(tri, lf_all[rows], 3)
        ipre = ipre_all[rows]
        num = jnp.zeros((L, W), F32)
        heads, w_st = [], []
        for h in range(N_HEADS):
            b_col = bcs[:, MISC_F + h:MISC_F + h + 1]
            i_col = ipre[:, MISC_I + h:MISC_I + h + 1]
            b_last = bcs[L - 1:L, MISC_F + h:MISC_F + h + 1]
            dmat = jnp.where(causal, b_col - _col_to_row(b_col) + _col_to_row(i_col), -jnp.inf)
            m_dmat = jnp.max(dmat, axis=-1, keepdims=True)
            a_st = b_last - b_col + i_col
            m_loc = jnp.max(a_st, axis=0, keepdims=True)
            w_st.append(jnp.exp(a_st - m_loc))
            scores = _dot_nt(_bf(q * masks[h]), kb) * jnp.exp(dmat - m_dmat)
            num = num + _dot(_bf(scores), _bf(v * masks[h]))
            heads.append(dict(b_col=b_col, b_last=b_last, m_loc=m_loc, m_dmat=m_dmat,
                              rowsum=jnp.sum(scores, axis=-1, keepdims=True)))
        wst = _expand(w_st, HEAD_DIM)
        return dict(rows=rows, q=q, num=num, heads=heads,
                    c_new=bd * _dot_tn(_bf(v * wst), kb),
                    n_new=jnp.sum(k * wst, axis=0, keepdims=True))

    def chunk(_):
        for p in [prepare(c) for c in range(tb // L)]:
            rows, q = p["rows"], p["q"]
            m_state = m_ref[...]
            w_inter, corr, rowsum, eneg, s_old, s_new, m_new = [], [], [], [], [], [], []
            for h, hd in enumerate(p["heads"]):
                m_prev = m_state[:, h:h + 1]
                mn = jnp.maximum(hd["b_last"] + m_prev, hd["m_loc"])
                m_new.append(mn)
                s_old.append(jnp.exp(hd["b_last"] + m_prev - mn))
                s_new.append(jnp.exp(hd["m_loc"] - mn))
                m_inter = hd["b_col"] + m_prev
                m_row = jnp.maximum(m_inter, hd["m_dmat"])
                w_inter.append(jnp.exp(m_inter - m_row))
                cr = jnp.exp(hd["m_dmat"] - m_row)
                corr.append(cr)
                rowsum.append(cr * hd["rowsum"])
                eneg.append(jnp.exp(-m_row))
            c_prev = c_ref[...]
            n_prev = n_ref[...]
            wi = _expand(w_inter, HEAD_DIM)
            fin[0, rows, :] = _expand(corr, HEAD_DIM) * p["num"] + wi * _dot_nt(_bf(q), _bf(c_prev))
            fin[1, rows, :] = wi
            fin[2, rows, :] = q * n_prev
            fin[3, rows, :] = _expand(rowsum, HEAD_DIM)
            fin[4, rows, :] = _expand(eneg, HEAD_DIM)
            so = _expand(s_old, HEAD_DIM)
            sn = _expand(s_new, HEAD_DIM)
            c_ref[...] = so * c_prev + sn * p["c_new"]
            n_ref[...] = so * n_prev + sn * p["n_new"]
            lane = _iota((1, 128), 1)
            m_vec = jnp.zeros((1, 128), F32)
            for h in range(N_HEADS):
                m_vec = jnp.where(lane == h, m_new[h], m_vec)
            m_ref[...] = m_vec

    def finish():
        den = fin[1] * _dot_rx(fin[2], bd, 3) + fin[3]
        hval = fin[0] / jnp.maximum(jnp.abs(den), fin[4])
        hv = hval * _sigmoid(og_all)
        mean = _dot_rx(hv, segmean, 2)
        xc = hv - mean
        var = _dot_rx(xc * xc, segmean, 2)
        o_ref[...] = (xc * lax.rsqrt(var + NORM_EPS) * ng_ref[...]).astype(o_ref.dtype)

    return 1, chunk, finish


def _mlstm_mixer(u, bsz, seq, conv_w, conv_b, i_b, f_b, norm_g):
    tb = MLSTM_TB
    ib = jnp.zeros((1, 128), F32).at[0, MISC_I:MISC_I + N_HEADS].set(i_b)
    fb = jnp.zeros((1, 128), F32).at[0, MISC_F:MISC_F + N_HEADS].set(f_b)
    return _launch_mixer(
        _mlstm_body, "mlstm_mixer", u, bsz, seq, tb,
        [(2 * GROUP_W, COL_ML_QK), (2 * GROUP_W, COL_ML_VO), (128, COL_MISC)],
        (conv_w, conv_b[None, :], ib, fb, norm_g[None, :]),
        [(tb + 8, 2 * GROUP_W), (GROUP_W, GROUP_W), (1, GROUP_W), (1, 128), (5, tb, GROUP_W)])


def _stack_heads(x, masks):
    return jnp.concatenate([x * m for m in masks], axis=0)


def _rwkv_body(rkv_ref, lora_ref, mu_rkv_ref, mu_lora_ref, w0_ref, wup_ref, a0_ref, aup_ref,
               gup_ref, kk_ref, ka_ref, rk_ref, lng_ref, lnb_ref, o_ref,
               rbuf, lbuf, s_ref, obuf, *, tb):
    W = GROUP_W
    rbuf[8:8 + tb, :] = rkv_ref[...]
    lbuf[8:8 + tb, :] = lora_ref[...]
    rkv = rkv_ref[...]
    lora = lora_ref[...]
    rkv = rkv + (rbuf[pl.ds(7, tb), :] - rkv) * mu_rkv_ref[...]
    lora = lora + (lbuf[pl.ds(7, tb), :] - lora) * mu_lora_ref[...]
    rbuf[0:8, :] = rbuf[tb:tb + 8, :]
    lbuf[0:8, :] = lbuf[tb:tb + 8, :]

    r_all, k_all, v_all = rkv[:, 0:W], rkv[:, W:2 * W], rkv[:, 2 * W:3 * W]
    wpre = w0_ref[...] + _pdot(jnp.tanh(lora), wup_ref[...], 3)
    lw_all = -jnp.exp(-_softplus(-wpre) - 0.5)
    a_all = _sigmoid(a0_ref[...] + _pdot(lora, aup_ref[...], 3))
    g_all = _pdot(_sigmoid(lora), gup_ref[...], 3)
    segsum = _seg_matrix(W, HEAD_DIM, 1.0)
    segmean = _seg_matrix(W, HEAD_DIM, 1.0 / HEAD_DIM)
    kk = k_all * kk_ref[...]
    kk = kk / jnp.maximum(jnp.sqrt(_dot_rx(kk * kk, segsum, 2)), 1e-12)
    k2_all = k_all * (1.0 + (a_all - 1.0) * ka_ref[...])
    av_all = -kk
    bv_all = kk * a_all

    L = CHUNK
    HL = N_HEADS * L
    tri = _tri_incl(L)
    masks = [_head_mask(W, HEAD_DIM, h) for h in range(N_HEADS)]
    t_idx = _iota((L, HL), 0)
    s_idx = jnp.bitwise_and(_iota((L, HL), 1), L - 1)
    strict = s_idx < t_idx
    incl = s_idx <= t_idx
    eye = (_iota((HL, HL), 0) == _iota((HL, HL), 1)).astype(F32)

    def prepare(c):
        rows = slice(c * L, (c + 1) * L)
        lw = lw_all[rows]
        cum = _dot_lx(tri, lw, 3)
        cum_last = cum[L - 1:L, :]
        w_inv = jnp.exp(-cum)
        w_dec = jnp.exp(cum_last - cum)
        k2, v = k2_all[rows], v_all[rows]
        rt = r_all[rows] * jnp.exp(cum)
        at = av_all[rows] * jnp.exp(cum - lw)
        bt_bd = _stack_heads(bv_all[rows] * w_inv, masks)
        kt_bd = _stack_heads(k2 * w_inv, masks)
        v_bd = _stack_heads(v, masks)
        sc = _pdot(jnp.concatenate([at, rt], axis=0), jnp.concatenate([bt_bd, kt_bd], axis=0),
                   RWKV_P_SCORE, "nt")
        ab = jnp.where(strict, sc[0:L, 0:HL], 0.0)
        ak = jnp.where(strict, sc[0:L, HL:2 * HL], 0.0)
        rb = jnp.where(incl, sc[L:2 * L, 0:HL], 0.0)
        rk = jnp.where(incl, sc[L:2 * L, HL:2 * HL], 0.0)
        on_v = _pdot(jnp.concatenate([_stack_heads(ak, masks), rk], axis=0), v_bd, RWKV_P_RHS)
        dec_bd = jnp.concatenate([_stack_heads(bv_all[rows] * w_dec, masks),
                                  _stack_heads(k2 * w_dec, masks)], axis=0)
        return dict(rows=rows, p=_stack_heads(ab, masks), rb=rb, on_v=on_v, v_bd=v_bd,
                    dec_bd=dec_bd, s_decay=jnp.exp(cum_last),
                    lhs_s=jnp.concatenate([_stack_heads(at, masks), rt], axis=0))

    def inverse_steps(group):
        ps = [c["p"] for c in group]
        minvs = [eye + p for p in ps]
        for _ in range(5):
            ps = [_pdot(p, p, RWKV_P_INV) for p in ps]
            minvs = [m + _pdot(m, p, RWKV_P_INV) for m, p in zip(minvs, ps)]
            yield minvs

    def carry_state(c, minv):
        s_prev = s_ref[...]
        on_s = _pdot(c["lhs_s"], s_prev, RWKV_P_RHS, "nt")
        on = on_s + c["on_v"]
        sa_bd = _pdot(minv, on[0:HL], RWKV_P_RHS)
        obuf[c["rows"], :] = on[HL:HL + L] + _pdot(c["rb"], sa_bd, RWKV_P_OUT)
        s_ref[...] = s_prev * c["s_decay"] + _pdot(
            jnp.concatenate([sa_bd, c["v_bd"]], axis=0), c["dec_bd"], RWKV_P_STATE, "tn")

    def chunk(_):
        n, gsz = tb // L, RWKV_GROUP
        groups = [[prepare(c) for c in range(g, g + gsz)] for g in range(0, n, gsz)]
        minvs = list(inverse_steps(groups[0]))[-1]
        for gi, group in enumerate(groups):
            nxt = inverse_steps(groups[gi + 1]) if gi + 1 < len(groups) else iter(())
            nxt_minvs = None
            for c, minv in zip(group, minvs):
                carry_state(c, minv)
                nxt_minvs = next(nxt, nxt_minvs)
            for nxt_minvs in nxt:
                pass
            minvs = nxt_minvs

    def finish():
        o = obuf[...]
        mean = _dot_rx(o, segmean, 2)
        oc = o - mean
        var = _dot_rx(oc * oc, segmean, 2)
        on = oc * lax.rsqrt(var + RWKV_GN_EPS) * lng_ref[...] + lnb_ref[...]
        bonus = _dot_rx(r_all * k2_all * rk_ref[...], segsum, 2) * v_all
        o_ref[...] = ((on + bonus) * g_all).astype(o_ref.dtype)

    return 1, chunk, finish


def _rwkv_mixer(u, bsz, seq, mu, w0, w_up, a0, a_up, g_up, k_k, k_a, r_k, ln_g, ln_b):
    tb = RWKV_TB
    W = GROUP_W
    wup = jnp.zeros((128, W), F32).at[0:32].set(w_up)
    aup = jnp.zeros((128, W), F32).at[32:64].set(a_up)
    gup = jnp.zeros((128, W), F32).at[64:128].set(g_up)
    consts = (mu[None, :3 * W], mu[None, 3 * W:], w0[None, :], wup, a0[None, :], aup, gup,
              k_k[None, :], k_a[None, :], r_k.reshape(1, W), ln_g[None, :], ln_b[None, :])
    return _launch_mixer(
        _rwkv_body, "rwkv_mixer", u, bsz, seq, tb, [(3 * W, COL_RWKV), (128, COL_LORA)], consts,
        [(tb + 8, 3 * W), (tb + 8, 128), (W, W), (tb, W)])


def _outproj_body(x_ref, y0_ref, y1_ref, y2_ref, y3_ref, w_ref, g_ref, b_ref, rwt_ref, rb_ref,
                  of_ref, idx_ref, gate_ref, rank_ref, cnt_ref, carry_ref):
    W = GROUP_W
    mix = _dot(y0_ref[...], w_ref[0:W, :])
    mix = mix + _dot(y1_ref[...], w_ref[W:2 * W, :])
    mix = mix + _dot(y2_ref[...], w_ref[2 * W:3 * W, :])
    mix = mix + _dot(y3_ref[...], w_ref[3 * W:4 * W, :])
    out = _layer_norm(DEEPNORM_ALPHA * x_ref[...] + mix, g_ref[...], b_ref[...])
    of_ref[...] = out
    _route_tokens(out, rwt_ref, rb_ref, idx_ref, gate_ref, rank_ref, cnt_ref, carry_ref)


def _out_proj_ln_route(x, ys, w_bf, g, b, router_wt, router_b, tm=512):
    t, d = x.shape
    row = lambda i: (i, 0)
    const = lambda i: (0, 0)
    tok = lambda i: (0, i)
    return pl.pallas_call(
        _outproj_body,
        grid=(t // tm,),
        in_specs=[pl.BlockSpec((tm, d), row)] + [pl.BlockSpec((tm, GROUP_W), row)] * 4
        + [pl.BlockSpec((d, d), const), pl.BlockSpec((1, d), const), pl.BlockSpec((1, d), const),
           pl.BlockSpec((N_EXPERTS, d), const), pl.BlockSpec((N_EXPERTS, 128), const)],
        out_specs=[pl.BlockSpec((tm, d), row), pl.BlockSpec((TOP_K, tm), tok),
                   pl.BlockSpec((TOP_K, tm), tok), pl.BlockSpec((TOP_K, tm), tok),
                   pl.BlockSpec((N_EXPERTS, 128), const)],
        out_shape=[jax.ShapeDtypeStruct((t, d), F32),
                   jax.ShapeDtypeStruct((TOP_K, t), jnp.int32),
                   jax.ShapeDtypeStruct((TOP_K, t), F32),
                   jax.ShapeDtypeStruct((TOP_K, t), jnp.int32),
                   jax.ShapeDtypeStruct((N_EXPERTS, 128), F32)],
        scratch_shapes=[pltpu.VMEM((N_EXPERTS, 128), F32)],
        compiler_params=_cparams(("arbitrary",)),
        name="out_proj_ln_route",
    )(x, *ys, w_bf, g[None, :], b[None, :], router_wt,
      jnp.broadcast_to(router_b[:, None], (N_EXPERTS, 128)))


def _route_tokens(x, wt_ref, b_ref, idx_ref, gate_ref, rank_ref, cnt_ref, carry_ref):
    tr = x.shape[0]

    @pl.when(pl.program_id(0) == 0)
    def _():
        carry_ref[...] = jnp.zeros(carry_ref.shape, F32)

    logits = _pdot(wt_ref[...], x, 3, "nt") + b_ref[...][:, 0:1]
    e_iota = _iota((N_EXPERTS, tr), 0)
    work = logits
    onehot = jnp.zeros((N_EXPERTS, tr), F32)
    sels, vals, idxs = [], [], []
    for _ in range(TOP_K):
        m = jnp.max(work, axis=0, keepdims=True)
        idx = jnp.min(jnp.where(work == m, e_iota, N_EXPERTS), axis=0, keepdims=True)
        sel = e_iota == idx
        work = jnp.where(sel, -jnp.inf, work)
        onehot = onehot + sel.astype(F32)
        sels.append(sel)
        vals.append(m)
        idxs.append(idx)
    exps = [jnp.exp(v - vals[0]) for v in vals]
    tot = exps[0] + exps[1] + exps[2] + exps[3]
    upper = (_iota((tr, tr), 0) < _iota((tr, tr), 1)).astype(BF16)
    carry = carry_ref[...][:, 0:1]
    before = _dot(_bf(onehot), upper) + carry
    ranks = [jnp.sum(jnp.where(s, before, 0.0), axis=0, keepdims=True) for s in sels]
    idx_ref[...] = jnp.concatenate(idxs, axis=0)
    gate_ref[...] = jnp.concatenate([e / tot for e in exps], axis=0)
    rank_ref[...] = jnp.concatenate(ranks, axis=0).astype(jnp.int32)
    new_carry = carry + jnp.sum(onehot, axis=1, keepdims=True)
    carry_ref[...] = jnp.broadcast_to(new_carry, carry_ref.shape)
    cnt_ref[...] = jnp.broadcast_to(new_carry, cnt_ref.shape)


def _route_plan_body(idx_ref, rank_ref, cnt_ref, dest_ref, be_ref, nv_ref, ps_ref, pl_ref, vb_ref,
                     *, blk, nb_pad):
    E = N_EXPERTS
    cnt = cnt_ref[...][:, 0:1]
    padded = jnp.floor((cnt + (blk - 1)) / blk) * blk
    lower = (_iota((E, E), 1) < _iota((E, E), 0)).astype(F32)
    pstart = jnp.sum(lower * _col_to_row(padded), axis=1, keepdims=True)
    pend = pstart + padded
    ps_ref[...] = jnp.broadcast_to(pstart + cnt, ps_ref.shape).astype(jnp.int32)
    pl_ref[...] = jnp.broadcast_to(padded - cnt, pl_ref.shape).astype(jnp.int32)
    idx = idx_ref[...]
    dest = rank_ref[...].astype(F32)
    for e in range(E):
        dest = dest + jnp.where(idx == e, pstart[e:e + 1, 0:1], 0.0)
    dest_ref[...] = dest.astype(jnp.int32)
    blk_start = (_iota((1, nb_pad), 1) * blk).astype(F32)
    be = jnp.sum((pend <= blk_start).astype(F32), axis=0, keepdims=True)
    be_ref[...] = jnp.minimum(be, E - 1).astype(jnp.int32)
    nv_ref[...] = jnp.broadcast_to(pend[E - 1:E, 0:1] / blk, nv_ref.shape).astype(jnp.int32)
    owns = (pstart <= blk_start) & (blk_start < pend)
    rows = jnp.clip(pstart + cnt - blk_start, 0.0, float(blk))
    vb_ref[...] = jnp.sum(jnp.where(owns, rows, 0.0), axis=0, keepdims=True).astype(jnp.int32)


def _route_plan(idx_t, rank_t, counts, blk, nb_pad):
    t = idx_t.shape[1]
    return pl.pallas_call(
        functools.partial(_route_plan_body, blk=blk, nb_pad=nb_pad),
        out_shape=[jax.ShapeDtypeStruct((TOP_K, t), jnp.int32),
                   jax.ShapeDtypeStruct((1, nb_pad), jnp.int32),
                   jax.ShapeDtypeStruct((1, 128), jnp.int32),
                   jax.ShapeDtypeStruct((N_EXPERTS, 128), jnp.int32),
                   jax.ShapeDtypeStruct((N_EXPERTS, 128), jnp.int32),
                   jax.ShapeDtypeStruct((1, nb_pad), jnp.int32)],
        compiler_params=pltpu.CompilerParams(vmem_limit_bytes=VMEM_LIMIT),
        name="route_plan",
    )(idx_t, rank_t, counts)


def _dispatch_body(ps_ref, pl_ref, nv_ref, dest_ref, x_ref, xs_hbm, zbuf, sem, fsem,
                   *, td, blk, n_blocks):
    sizes = [blk >> (s + 1) for s in range(int(np.log2(blk)) - 3)]

    def pad_piece(e, s):
        n = pl_ref[e]
        first = ps_ref[e] + jnp.bitwise_and(n, 7)
        off = pl.multiple_of(first + jnp.bitwise_and(n, -2 * s), 8)
        return (jnp.bitwise_and(n, s) != 0,
                pltpu.make_async_copy(zbuf.at[pl.ds(0, s)], xs_hbm.at[pl.ds(off, s)], fsem))

    def pad_row(e, r):
        return (r < jnp.bitwise_and(pl_ref[e], 7),
                pltpu.make_async_copy(zbuf.at[pl.ds(0, 1)], xs_hbm.at[pl.ds(ps_ref[e] + r, 1)], fsem))

    def tail_copy(b):
        off = pl.multiple_of(b * blk, blk)
        return pltpu.make_async_copy(zbuf, xs_hbm.at[pl.ds(off, blk)], fsem)

    @pl.when(pl.program_id(0) == 0)
    def _():
        zbuf[...] = jnp.zeros(zbuf.shape, zbuf.dtype)

        def pads(start):
            def body(e, carry):
                for cond, cp in [pad_piece(e, s) for s in sizes] + [pad_row(e, r) for r in range(7)]:
                    @pl.when(cond)
                    def _():
                        cp.start() if start else cp.wait()
                return carry
            lax.fori_loop(0, N_EXPERTS, body, 0)

        def tails(start):
            def body(b, carry):
                tail_copy(b).start() if start else tail_copy(b).wait()
                return carry
            lax.fori_loop(nv_ref[0], n_blocks, body, 0)

        pads(True)
        tails(True)
        pads(False)
        tails(False)

    def issue(g, carry):
        for s in range(8):
            for k in range(TOP_K):
                row = dest_ref[g * (8 * TOP_K) + (s * TOP_K + k)]
                pltpu.make_async_copy(x_ref.at[g, pl.ds(s, 1)],
                                      xs_hbm.at[pl.ds(row, 1)], sem).start(priority=k % 2)
        return carry

    lax.fori_loop(0, td // 8, issue, 0)
    for _ in range(TOP_K):
        pltpu.make_async_copy(xs_hbm.at[pl.ds(0, td)], xs_hbm.at[pl.ds(0, td)], sem).wait()


def _dispatch(pad_start, pad_len, nv, dest_flat, x, n_rows, blk, td=1024):
    t, d = x.shape
    grid_spec = pltpu.PrefetchScalarGridSpec(
        num_scalar_prefetch=3,
        grid=(t // td,),
        in_specs=[pl.BlockSpec((td * TOP_K,), lambda i, *_: (i,), memory_space=pltpu.SMEM),
                  pl.BlockSpec((td // 8, 8, d), lambda i, *_: (i, 0, 0))],
        out_specs=pl.BlockSpec(memory_space=pl.ANY),
        scratch_shapes=[pltpu.VMEM((blk, d), x.dtype), pltpu.SemaphoreType.DMA(()),
                        pltpu.SemaphoreType.DMA(())],
    )
    return pl.pallas_call(
        functools.partial(_dispatch_body, td=td, blk=blk, n_blocks=n_rows // blk),
        grid_spec=grid_spec,
        out_shape=jax.ShapeDtypeStruct((n_rows, d), x.dtype),
        compiler_params=_cparams(("arbitrary",)),
        name="dispatch",
    )(pad_start, pad_len, nv, dest_flat, x.reshape(t // 8, 8, d))


def _expert_body(be_ref, nv_ref, vb_ref, xs_ref, wgu_ref, bgu_ref, wd_ref, bd_ref, ys_ref,
                 wgu_bf, wd_bf):
    b = pl.program_id(0)
    d = D_MODEL
    blk = xs_ref.shape[0]
    half = blk // 2
    valid = b < nv_ref[0]

    @pl.when(valid)
    def _():
        prev = be_ref[jnp.maximum(b - 1, 0)]

        @pl.when((b == 0) | (be_ref[b] != prev))
        def _():
            wgu_bf[...] = wgu_ref[...].astype(BF16)
            wd_bf[...] = wd_ref[...].astype(BF16)

    def ffn(rows):
        h = _dot(_bf(xs_ref[rows, :]), wgu_bf[...]) + bgu_ref[...]
        hg = jnp.minimum(h[:, 0:d], SWIGLU_LIMIT)
        hl = jnp.clip(h[:, d:2 * d], -SWIGLU_LIMIT, SWIGLU_LIMIT)
        act = (hl + 1.0) * (hg * _sigmoid(hg * SWIGLU_ALPHA))
        ys_ref[rows, :] = _dot(_bf(act), wd_bf[...]) + bd_ref[...]

    @pl.when(valid & (vb_ref[b] > half))
    def _():
        ffn(slice(0, blk))

    @pl.when(valid & (vb_ref[b] <= half))
    def _():
        ffn(slice(0, half))
        ys_ref[half:blk, :] = jnp.zeros((blk - half, d), F32)

    @pl.when(jnp.logical_not(valid))
    def _():
        ys_ref[...] = jnp.zeros(ys_ref.shape, F32)


def _experts(be, nv, vb, xs, w_gu, b_gu, w_down, b_down, layer, blk):
    n_rows, d = xs.shape
    nb = n_rows // blk

    def row(b, be_r, nv_r, vb_r):
        return (jnp.minimum(b, nv_r[0] - 1), 0)

    def wsel(b, be_r, nv_r, vb_r):
        return (layer, be_r[jnp.minimum(b, nv_r[0] - 1)], 0, 0)

    grid_spec = pltpu.PrefetchScalarGridSpec(
        num_scalar_prefetch=3,
        grid=(nb,),
        in_specs=[pl.BlockSpec((blk, d), row),
                  pl.BlockSpec((None, None, d, 2 * d), wsel),
                  pl.BlockSpec((None, None, 1, 2 * d), wsel),
                  pl.BlockSpec((None, None, d, d), wsel),
                  pl.BlockSpec((None, None, 1, d), wsel)],
        out_specs=pl.BlockSpec((blk, d), lambda b, be_r, nv_r, vb_r: (b, 0)),
        scratch_shapes=[pltpu.VMEM((d, 2 * d), BF16), pltpu.VMEM((d, d), BF16)],
    )
    return pl.pallas_call(
        _expert_body,
        grid_spec=grid_spec,
        out_shape=jax.ShapeDtypeStruct((n_rows, d), F32),
        compiler_params=_cparams(("arbitrary",)),
        name="experts",
    )(be, nv, vb, xs, w_gu, b_gu, w_down, b_down)


def _expert_prefetch_body(be_ref, nv_ref, vb_ref, xs_ref, wgu_hbm, bgu_ref, wd_hbm, bd_ref, ys_ref,
                          wgu_f, wd_f, wgu_bf, wd_bf, cur_ref, sem, *, layer):
    b = pl.program_id(0)
    d = D_MODEL
    blk = xs_ref.shape[0]
    half = blk // 2
    n_valid = nv_ref[0]
    valid = b < n_valid
    last = be_ref.shape[0] - 1

    def weight_copies(e, s):
        return (pltpu.make_async_copy(wgu_hbm.at[layer, e], wgu_f.at[s], sem.at[s]),
                pltpu.make_async_copy(wd_hbm.at[layer, e], wd_f.at[s], sem.at[s]))

    prev = be_ref[jnp.maximum(b - 1, 0)]

    @pl.when(valid & ((b == 0) | (be_ref[b] != prev)))
    def _():
        @pl.when(b == 0)
        def _():
            cur_ref[0] = 0
            for cp in weight_copies(be_ref[0], 0):
                cp.start()

        @pl.when(b > 0)
        def _():
            cur_ref[0] = 1 - cur_ref[0]

        nxt = lax.while_loop(
            lambda j: (j < n_valid) & (be_ref[jnp.minimum(j, last)] == be_ref[b]),
            lambda j: j + 1, b + 1)
        for s in range(2):
            @pl.when(cur_ref[0] == s)
            def _():
                for cp in weight_copies(be_ref[b], s):
                    cp.wait()
                wgu_bf[...] = wgu_f[s].astype(BF16)
                wd_bf[...] = wd_f[s].astype(BF16)

                @pl.when(nxt < n_valid)
                def _():
                    for cp in weight_copies(be_ref[jnp.minimum(nxt, last)], 1 - s):
                        cp.start()

    def ffn(rows):
        h = _dot(_bf(xs_ref[rows, :]), wgu_bf[...]) + bgu_ref[...]
        hg = jnp.minimum(h[:, 0:d], SWIGLU_LIMIT)
        hl = jnp.clip(h[:, d:2 * d], -SWIGLU_LIMIT, SWIGLU_LIMIT)
        act = (hl + 1.0) * (hg * _sigmoid(hg * SWIGLU_ALPHA))
        ys_ref[rows, :] = _dot(_bf(act), wd_bf[...]) + bd_ref[...]

    @pl.when(valid & (vb_ref[b] > half))
    def _():
        ffn(slice(0, blk))

    @pl.when(valid & (vb_ref[b] <= half))
    def _():
        ffn(slice(0, half))
        ys_ref[half:blk, :] = jnp.zeros((blk - half, d), F32)

    @pl.when(jnp.logical_not(valid))
    def _():
        ys_ref[...] = jnp.zeros(ys_ref.shape, F32)


def _experts_prefetch(be, nv, vb, xs, w_gu, b_gu, w_down, b_down, layer, blk):
    n_rows, d = xs.shape
    nb = n_rows // blk

    def row(b, be_r, nv_r, vb_r):
        return (jnp.minimum(b, nv_r[0] - 1), 0)

    def bsel(b, be_r, nv_r, vb_r):
        return (layer, be_r[jnp.minimum(b, nv_r[0] - 1)], 0, 0)

    grid_spec = pltpu.PrefetchScalarGridSpec(
        num_scalar_prefetch=3,
        grid=(nb,),
        in_specs=[pl.BlockSpec((blk, d), row),
                  pl.BlockSpec(memory_space=pl.ANY),
                  pl.BlockSpec((None, None, 1, 2 * d), bsel),
                  pl.BlockSpec(memory_space=pl.ANY),
                  pl.BlockSpec((None, None, 1, d), bsel)],
        out_specs=pl.BlockSpec((blk, d), lambda b, be_r, nv_r, vb_r: (b, 0)),
        scratch_shapes=[pltpu.VMEM((2, d, 2 * d), F32), pltpu.VMEM((2, d, d), F32),
                        pltpu.VMEM((d, 2 * d), BF16), pltpu.VMEM((d, d), BF16),
                        pltpu.SMEM((1,), jnp.int32), pltpu.SemaphoreType.DMA((2,))],
    )
    return pl.pallas_call(
        functools.partial(_expert_prefetch_body, layer=layer),
        grid_spec=grid_spec,
        out_shape=jax.ShapeDtypeStruct((n_rows, d), F32),
        compiler_params=_cparams(("arbitrary",)),
        name="experts",
    )(be, nv, vb, xs, w_gu, b_gu, w_down, b_down)


def _combine_body(dcur_ref, dnext_ref, ys_hbm, x_ref, gate_ref, p_ref, wg_ref, wp_ref, g_ref, b_ref,
                  of_ref, ob_ref, ybuf, sem, *, tc):
    i = pl.program_id(0)
    n = pl.num_programs(0)
    slot = lax.rem(i, 2)

    def issue(d_ref, s):
        def body(g, carry):
            for r in range(8):
                for k in range(TOP_K):
                    row = d_ref[g * (8 * TOP_K) + (r * TOP_K + k)]
                    pltpu.make_async_copy(ys_hbm.at[pl.ds(row, 1)], ybuf.at[s, k, g, pl.ds(r, 1)],
                                          sem.at[s]).start(priority=k % 2)
            return carry
        lax.fori_loop(0, tc // 8, body, 0)

    @pl.when(i == 0)
    def _():
        issue(dcur_ref, 0)

    for s in range(2):
        @pl.when((i + 1 < n) & (slot == 1 - s))
        def _():
            issue(dnext_ref, s)

    for k in range(TOP_K):
        pltpu.make_async_copy(ys_hbm.at[pl.ds(0, tc)], ys_hbm.at[pl.ds(0, tc)], sem.at[slot]).wait()

    gate = gate_ref[...]
    d = x_ref.shape[-1]
    ffn = ybuf[slot, 0].reshape(tc, d) * gate[:, 0:1]
    for k in range(1, TOP_K):
        ffn = ffn + ybuf[slot, k].reshape(tc, d) * gate[:, k:k + 1]
    h = DEEPNORM_ALPHA * x_ref[...] + ffn
    ple = _dot(_bf(p_ref[...]), wp_ref[...])
    h = h + _sigmoid(_dot(_bf(h), wg_ref[...])) * ple
    out = _layer_norm(h, g_ref[...], b_ref[...])
    of_ref[...] = out
    ob_ref[...] = out.astype(BF16)


def _combine(dest_flat, ys, x, gate, p, wg_bf, wp_bf, g, b, layer, tc=512):
    t, d = x.shape
    n = t // tc
    row = lambda i: (i, 0)
    const = lambda i: (0, 0)
    return pl.pallas_call(
        functools.partial(_combine_body, tc=tc),
        grid=(n,),
        in_specs=[pl.BlockSpec((tc * TOP_K,), lambda i: (i,), memory_space=pltpu.SMEM),
                  pl.BlockSpec((tc * TOP_K,), lambda i: (jnp.minimum(i + 1, n - 1),),
                               memory_space=pltpu.SMEM),
                  pl.BlockSpec(memory_space=pl.ANY),
                  pl.BlockSpec((tc, d), row),
                  pl.BlockSpec((tc, TOP_K), row),
                  pl.BlockSpec((None, tc, PLE_DIM), lambda i: (layer, i, 0)),
                  pl.BlockSpec((d, d), const),
                  pl.BlockSpec((PLE_DIM, d), const),
                  pl.BlockSpec((1, d), const),
                  pl.BlockSpec((1, d), const)],
        out_specs=[pl.BlockSpec((tc, d), row), pl.BlockSpec((tc, d), row)],
        out_shape=[jax.ShapeDtypeStruct((t, d), F32), jax.ShapeDtypeStruct((t, d), BF16)],
        scratch_shapes=[pltpu.VMEM((2, TOP_K, tc // 8, 8, d), F32), pltpu.SemaphoreType.DMA((2,))],
        compiler_params=_cparams(("arbitrary",)),
        name="combine",
    )(dest_flat, dest_flat, ys, x, gate, p, wg_bf, wp_bf, g[None, :], b[None, :])


def kernel(x, p, w_in, w_out, ln1_g, ln1_b, ssd_conv_w, ssd_conv_b, ssd_a_log, ssd_dt_bias, ssd_d, ssd_norm_g, rwkv_mu, rwkv_w0, rwkv_w_up, rwkv_a0, rwkv_a_up, rwkv_g_up, rwkv_k_k, rwkv_k_a, rwkv_r_k, rwkv_ln_g, rwkv_ln_b, gla_gate_up, gla_gate_b, gla_norm_g, mlstm_conv_w, mlstm_conv_b, mlstm_i_b, mlstm_f_b, mlstm_norm_g, router_w, router_b, exp_w_gu, exp_b_gu, exp_w_down, exp_b_down, ple_gate_w, ple_proj, ln2_g, ln2_b):
    bsz, seq, d = x.shape
    t = bsz * seq
    depth = w_in.shape[0]
    blk = EXPERT_BLK
    n_blocks = -(-(t * TOP_K + N_EXPERTS * (blk - 1)) // blk)
    n_rows = n_blocks * blk
    nb_pad = -(-n_blocks // 128) * 128

    xf = x.reshape(t, d)
    xb = xf.astype(BF16)
    p2 = p.reshape(depth, t, PLE_DIM)
    b_gu4 = exp_b_gu[:, :, None, :]
    b_down4 = exp_b_down[:, :, None, :]
    w_packed = _pack_w_in(w_in)
    for i in range(depth):
        u = _in_proj(xb, w_packed, i)
        y_ssd = _ssd_mixer(u, bsz, seq, ssd_conv_w[i], ssd_conv_b[i], ssd_a_log[i], ssd_dt_bias[i],
                           ssd_d[i], ssd_norm_g[i])
        y_rwkv = _rwkv_mixer(u, bsz, seq, rwkv_mu[i], rwkv_w0[i], rwkv_w_up[i], rwkv_a0[i],
                             rwkv_a_up[i], rwkv_g_up[i], rwkv_k_k[i], rwkv_k_a[i], rwkv_r_k[i],
                             rwkv_ln_g[i], rwkv_ln_b[i])
        y_gla = _gla_mixer(u, bsz, seq, gla_gate_up[i], gla_gate_b[i], gla_norm_g[i])
        y_ml = _mlstm_mixer(u, bsz, seq, mlstm_conv_w[i], mlstm_conv_b[i], mlstm_i_b[i],
                            mlstm_f_b[i], mlstm_norm_g[i])
        x1f, idx_t, gate_t, rank_t, counts = _out_proj_ln_route(
            xf, (y_ssd, y_rwkv, y_gla, y_ml), w_out[i].astype(BF16), ln1_g[i], ln1_b[i],
            router_w[i].T, router_b[i])
        dest_t, be, nv, pad_start, pad_len, vb = _route_plan(idx_t, rank_t, counts, blk, nb_pad)
        nv = nv.reshape(128)
        dest_flat = dest_t.T.reshape(t * TOP_K)
        xs = _dispatch(pad_start[:, 0], pad_len[:, 0], nv, dest_flat, x1f, n_rows, blk)
        ys = _experts_prefetch(be.reshape(nb_pad), nv, vb.reshape(nb_pad), xs, exp_w_gu, b_gu4,
                               exp_w_down, b_down4, i, blk)
        xf, xb = _combine(dest_flat, ys, x1f, gate_t.T, p2, ple_gate_w[i].astype(BF16),
                          ple_proj[i].astype(BF16), ln2_g[i], ln2_b[i], i)
    return xf.reshape(bsz, seq, d)
```
